```python
import math
import jax
import jax.numpy as jnp
from jax import lax
import numpy as np

D_MODEL = 2048
BATCH = 2
SEQ = 4096
DEPTH = 2
DEC_BATCH = 128
DEC_SEQ = 1
PAST_LEN = 2048
PAGE_SIZE = 128

N_MIXERS = 2
N_ATTN_LAYERS = (DEPTH + 1) // 2
N_RET_LAYERS = DEPTH // 2

WINDOWS = (128, 512, 2048)
DILATIONS = (1, 4, 16)
N_GROUPS = len(WINDOWS)
ATTN_HEAD_DIM = 128
ATTN_HEADS_PER_GROUP = D_MODEL // (2 * ATTN_HEAD_DIM)
ATTN_WIDTH = ATTN_HEADS_PER_GROUP * ATTN_HEAD_DIM
ATTN_QKV_WIDTH = N_GROUPS * 3 * ATTN_WIDTH
Q_BLOCK = 128
N_BUCKETS = 32
MAX_DISTANCE = 2048
NEG_INF = -1e30

RET_HEADS = 8
RET_KEY_DIM = D_MODEL // RET_HEADS
RET_VALUE_DIM = 2 * RET_KEY_DIM
RET_QK_WIDTH = RET_HEADS * RET_KEY_DIM
RET_V_WIDTH = RET_HEADS * RET_VALUE_DIM
RET_IN_WIDTH = 2 * RET_QK_WIDTH + 2 * RET_V_WIDTH
RET_CHUNK = 128
ROT_BASE = 10000.0

FFN_DIM = 5504
CONV_WIDTH = 3
NORM_EPS = 1e-6

kernel_name = 'hybrid_dilated_attn_retention_convffn_step'


def rms_norm(x, gain):
    xf = x.astype(jnp.float32)
    y = xf * lax.rsqrt(jnp.mean(xf * xf, axis=-1, keepdims=True) + NORM_EPS)
    return (y * gain.astype(jnp.float32)).astype(x.dtype)


def ada_modulation(c, w_ada, b_ada):
    m = jax.nn.silu(c) @ w_ada + b_ada
    return jnp.split(m[:, None, :], 6, axis=-1)


def t5_causal_bucket(dist):
    max_exact = N_BUCKETS // 2
    d = jnp.maximum(dist, 1).astype(jnp.float32)
    large = max_exact + (jnp.log(d / max_exact) / math.log(MAX_DISTANCE / max_exact)
                         * (N_BUCKETS - max_exact)).astype(jnp.int32)
    return jnp.where(dist < max_exact, dist, jnp.minimum(large, N_BUCKETS - 1))


def group_bias(rel_bias, g):
    n_keys = WINDOWS[g] // DILATIONS[g] + 1
    dist = DILATIONS[g] * jnp.arange(n_keys, dtype=jnp.int32)
    cols = rel_bias[:, g * ATTN_HEADS_PER_GROUP:(g + 1) * ATTN_HEADS_PER_GROUP]
    return jnp.take(cols, t5_causal_bucket(dist), axis=0).T.astype(jnp.float32)


def dilated_group_attend(q, kv_all, q_index, dilation, bias):
    n_keys = bias.shape[-1]
    j = jnp.arange(n_keys, dtype=jnp.int32)
    idx = q_index[:, None] - dilation * j[None, :]
    valid = idx >= 0
    kvg = jnp.take(kv_all, jnp.maximum(idx, 0), axis=1)
    logits = jnp.einsum('bthd,btjhd->bhtj', q, kvg[:, :, :, 0], preferred_element_type=jnp.float32)
    logits = logits * (ATTN_HEAD_DIM ** -0.5) + bias[None, :, None, :]
    logits = jnp.where(valid[None, None], logits, NEG_INF)
    m = jnp.max(logits, axis=-1, keepdims=True)
    p = jnp.exp(logits - m)
    s = jnp.sum(p, axis=-1, keepdims=True)
    o = jnp.einsum('bhtj,btjhd->bthd', p / s, kvg[:, :, :, 1].astype(jnp.float32))
    lse = jnp.transpose((m + jnp.log(s))[..., 0], (0, 2, 1))
    return o, lse


def dilated_mixture(q, kv_alls, q_indices, rel_bias):
    outs, lses = [], []
    for g in range(N_GROUPS):
        o, lse = dilated_group_attend(q[:, :, g], kv_alls[g], q_indices[g], DILATIONS[g], group_bias(rel_bias, g))
        outs.append(o)
        lses.append(lse)
    w = jax.nn.softmax(jnp.stack(lses, axis=0), axis=0)
    return jnp.sum(w[..., None] * jnp.stack(outs, axis=0), axis=0)


def dilated_attention_mixer(h, prev_kv, w_qkv, q_gain, k_gain, w_o, rel_bias):
    B, T, _ = h.shape
    qkv = (h @ w_qkv).reshape(B, T, N_GROUPS, 3, ATTN_HEADS_PER_GROUP, ATTN_HEAD_DIM)
    q = rms_norm(qkv[:, :, :, 0], q_gain[:, None, :])
    k = rms_norm(qkv[:, :, :, 1], k_gain[:, None, :])
    v = qkv[:, :, :, 2]
    kv_alls, offsets, new_kv = [], [], []
    for g in range(N_GROUPS):
        kv_new = jnp.stack([k[:, :, g], v[:, :, g]], axis=2)
        kv_all = jnp.concatenate([prev_kv[g].astype(h.dtype), kv_new], axis=1)
        kv_alls.append(kv_all)
        offsets.append(prev_kv[g].shape[1])
        keep = min(WINDOWS[g], kv_all.shape[1])
        new_kv.append(kv_all[:, kv_all.shape[1] - keep:])

    def attend(q_blk, t0):
        ar = t0 + jnp.arange(q_blk.shape[1], dtype=jnp.int32)
        return dilated_mixture(q_blk, kv_alls, [off + ar for off in offsets], rel_bias)

    if T > Q_BLOCK:
        nb = T // Q_BLOCK
        q_blocks = jnp.moveaxis(q.reshape(B, nb, Q_BLOCK, *q.shape[2:]), 1, 0)
        o = lax.map(lambda xs: attend(xs[0], xs[1] * Q_BLOCK), (q_blocks, jnp.arange(nb, dtype=jnp.int32)))
        o = jnp.moveaxis(o, 0, 1)
    else:
        o = attend(q, 0)
    o = o.reshape(B, T, ATTN_WIDTH).astype(h.dtype)
    return o @ w_o, new_kv


def retnet_rotate(x, positions):
    half = x.shape[-1] // 2
    inv_freq = 1.0 / (ROT_BASE ** jnp.linspace(0.0, 1.0, half, dtype=jnp.float32))
    ang = positions.astype(jnp.float32)[:, None] * inv_freq[None, :]
    cos = jnp.cos(ang)[None, :, None, :]
    sin = jnp.sin(ang)[None, :, None, :]
    xp = x.astype(jnp.float32).reshape(*x.shape[:-1], half, 2)
    x1, x2 = xp[..., 0], xp[..., 1]
    return jnp.stack([x1 * cos - x2 * sin, x1 * sin + x2 * cos], axis=-1).reshape(x.shape)


def retention_chunkwise(q, k, v, state, log_gamma):
    B, T = q.shape[:2]
    C = min(T, RET_CHUNK)
    nc = T // C
    pos = jnp.arange(C, dtype=jnp.float32)
    diff = pos[:, None] - pos[None, :]
    decay = jnp.where(diff >= 0, jnp.exp(diff[None] * log_gamma[:, None, None]), 0.0)
    q_decay = jnp.exp((pos[:, None] + 1.0) * log_gamma[None, :])
    k_decay = jnp.exp((C - 1.0 - pos)[:, None] * log_gamma[None, :])
    chunk_decay = jnp.exp(C * log_gamma)

    def step(S, xs):
        qc, kc, vc = xs
        scores = jnp.einsum('bihk,bjhk->bhij', qc, kc) * decay[None]
        o = jnp.einsum('bhij,bjhv->bihv', scores, vc)
        o = o + jnp.einsum('bihk,bhkv->bihv', qc * q_decay[None, :, :, None], S)
        S = chunk_decay[None, :, None, None] * S + jnp.einsum('bjhk,bjhv->bhkv', kc * k_decay[None, :, :, None], vc)
        return S, o

    to_chunks = lambda a: jnp.moveaxis(a.reshape(B, nc, C, *a.shape[2:]), 1, 0)
    S, o = lax.scan(step, state, (to_chunks(q), to_chunks(k), to_chunks(v)))
    return jnp.moveaxis(o, 0, 1).reshape(B, T, *v.shape[2:]), S


def retention_mixer(h, state, positions, w_in, gn_gain, w_o):
    B, T, _ = h.shape
    proj = h @ w_in
    q, k, v, gate = jnp.split(proj, [RET_QK_WIDTH, 2 * RET_QK_WIDTH, 2 * RET_QK_WIDTH + RET_V_WIDTH], axis=-1)
    q = retnet_rotate(q.reshape(B, T, RET_HEADS, RET_KEY_DIM), positions)
    k = retnet_rotate(k.reshape(B, T, RET_HEADS, RET_KEY_DIM), positions) * (RET_KEY_DIM ** -0.5)
    v = v.reshape(B, T, RET_HEADS, RET_VALUE_DIM).astype(jnp.float32)
    log_gamma = jnp.log1p(-jnp.exp2(-5.0 - jnp.arange(RET_HEADS, dtype=jnp.float32)))
    o, new_state = retention_chunkwise(q, k, v, state.astype(jnp.float32), log_gamma)
    mu = jnp.mean(o, axis=-1, keepdims=True)
    var = jnp.mean(jnp.square(o - mu), axis=-1, keepdims=True)
    o = (o - mu) * lax.rsqrt(var + NORM_EPS) * gn_gain.reshape(RET_HEADS, RET_VALUE_DIM).astype(jnp.float32)
    y = (jax.nn.silu(gate.astype(jnp.float32)) * o.reshape(B, T, RET_V_WIDTH)).astype(h.dtype)
    return y @ w_o, new_state.astype(state.dtype)


def conv_ffn(h, prev_rows, w_up, conv_w, conv_b, w_down):
    T = h.shape[1]
    u = h @ w_up
    u_all = jnp.concatenate([prev_rows.astype(u.dtype), u], axis=1)
    z = conv_b.astype(u.dtype)
    for i in range(CONV_WIDTH):
        z = z + conv_w[i].astype(u.dtype) * u_all[:, i:i + T]
    gate, val = jnp.split(z, 2, axis=-1)
    return (jax.nn.silu(gate) * val) @ w_down, u_all[:, T:]


def run_trunk(x, c, kv_w128, kv_w512, kv_w2048, ret_state, conv_state, positions,
              rel_bias, w_ada, b_ada, norm_mix, norm_ffn, attn_w_qkv, attn_q_gain, attn_k_gain, attn_w_o,
              ret_w_in, ret_gn_gain, ret_w_o, ffn_w_up, ffn_conv_w, ffn_conv_b, ffn_w_down):
    prev_kv = (kv_w128, kv_w512, kv_w2048)
    new_kv = [[] for _ in range(N_GROUPS)]
    new_ret, new_conv = [], []
    for layer in range(DEPTH):
        shift1, scale1, gate1, shift2, scale2, gate2 = ada_modulation(c, w_ada[layer], b_ada[layer])
        h = rms_norm(x, norm_mix[layer]) * (1.0 + scale1) + shift1
        if layer % N_MIXERS == 0:
            a = layer // N_MIXERS
            mix, kv = dilated_attention_mixer(h, [p[a] for p in prev_kv], attn_w_qkv[a], attn_q_gain[a],
                                              attn_k_gain[a], attn_w_o[a], rel_bias)
            for g in range(N_GROUPS):
                new_kv[g].append(kv[g])
        else:
            r = layer // N_MIXERS
            mix, st = retention_mixer(h, ret_state[r], positions, ret_w_in[r], ret_gn_gain[r], ret_w_o[r])
            new_ret.append(st)
        x = x + gate1 * mix
        h = rms_norm(x, norm_ffn[layer]) * (1.0 + scale2) + shift2
        f, rows = conv_ffn(h, conv_state[layer], ffn_w_up[layer], ffn_conv_w[layer], ffn_conv_b[layer], ffn_w_down[layer])
        new_conv.append(rows)
        x = x + gate2 * f
    return (x, jnp.stack(new_kv[0]), jnp.stack(new_kv[1]), jnp.stack(new_kv[2]),
            jnp.stack(new_ret), jnp.stack(new_conv))


def setup_inputs(seed: int = 0) -> dict:
    key = jax.random.key(seed)
    ks = jax.random.split(key, 26)
    f32 = jnp.float32
    nrm = lambda k, shape, scale: scale * jax.random.normal(k, shape, f32)
    hg, hd = ATTN_HEADS_PER_GROUP, ATTN_HEAD_DIM
    gamma = 1.0 - 2.0 ** (-5.0 - np.arange(RET_HEADS))
    ret_scale = jnp.asarray(np.sqrt(np.minimum(1.0 / (1.0 - gamma), PAST_LEN)) * RET_KEY_DIM ** -0.5, f32)
    return {
        'x_prompt': nrm(ks[0], (BATCH, SEQ, D_MODEL), 1.0),
        'x_sample': nrm(ks[1], (DEC_BATCH, DEC_SEQ, D_MODEL), 1.0),
        'cache_attn_kv_w128': nrm(ks[2], (N_ATTN_LAYERS, DEC_BATCH, min(WINDOWS[0], PAST_LEN), 2, hg, hd), 1.0),
        'cache_attn_kv_w512': nrm(ks[3], (N_ATTN_LAYERS, DEC_BATCH, min(WINDOWS[1], PAST_LEN), 2, hg, hd), 1.0),
        'cache_attn_kv_w2048': nrm(ks[4], (N_ATTN_LAYERS, DEC_BATCH, min(WINDOWS[2], PAST_LEN), 2, hg, hd), 1.0),
        'state_ret': nrm(ks[5], (N_RET_LAYERS, DEC_BATCH, RET_HEADS, RET_KEY_DIM, RET_VALUE_DIM), 1.0)
                     * ret_scale[None, None, :, None, None],
        'state_conv': nrm(ks[6], (DEPTH, DEC_BATCH, CONV_WIDTH - 1, 2 * FFN_DIM), 1.0),
        'c_prompt': nrm(ks[7], (BATCH, D_MODEL), 1.0),
        'c_sample': nrm(ks[8], (DEC_BATCH, D_MODEL), 1.0),
        'rel_bias': nrm(ks[9], (N_BUCKETS, N_GROUPS * hg), 0.5),
        'w_ada': nrm(ks[10], (DEPTH, D_MODEL, 6 * D_MODEL), D_MODEL ** -0.5),
        'b_ada': nrm(ks[11], (DEPTH, 6 * D_MODEL), 0.02),
        'norm_mix': 1.0 + nrm(ks[12], (DEPTH, D_MODEL), 0.02),
        'norm_ffn': 1.0 + nrm(ks[13], (DEPTH, D_MODEL), 0.02),
        'attn_w_qkv': nrm(ks[14], (N_ATTN_LAYERS, D_MODEL, ATTN_QKV_WIDTH), D_MODEL ** -0.5),
        'attn_q_gain': 1.0 + nrm(ks[15], (N_ATTN_LAYERS, N_GROUPS, hd), 0.02),
        'attn_k_gain': 1.0 + nrm(ks[16], (N_ATTN_LAYERS, N_GROUPS, hd), 0.02),
        'attn_w_o': nrm(ks[17], (N_ATTN_LAYERS, ATTN_WIDTH, D_MODEL), ATTN_WIDTH ** -0.5),
        'ret_w_in': nrm(ks[18], (N_RET_LAYERS, D_MODEL, RET_IN_WIDTH), D_MODEL ** -0.5),
        'ret_gn_gain': 1.0 + nrm(ks[19], (N_RET_LAYERS, RET_V_WIDTH), 0.02),
        'ret_w_o': nrm(ks[20], (N_RET_LAYERS, RET_V_WIDTH, D_MODEL), RET_V_WIDTH ** -0.5),
        'ffn_w_up': nrm(ks[21], (DEPTH, D_MODEL, 2 * FFN_DIM), D_MODEL ** -0.5),
        'ffn_conv_w': nrm(ks[22], (DEPTH, CONV_WIDTH, 2 * FFN_DIM), CONV_WIDTH ** -0.5),
        'ffn_conv_b': nrm(ks[23], (DEPTH, 2 * FFN_DIM), 0.02),
        'ffn_w_down': nrm(ks[24], (DEPTH, FFN_DIM, D_MODEL), FFN_DIM ** -0.5),
    }


def reference(x_prompt, x_sample, cache_attn_kv_w128, cache_attn_kv_w512, cache_attn_kv_w2048,
              state_ret, state_conv, c_prompt, c_sample, rel_bias, w_ada, b_ada, norm_mix, norm_ffn,
              attn_w_qkv, attn_q_gain, attn_k_gain, attn_w_o, ret_w_in, ret_gn_gain, ret_w_o,
              ffn_w_up, ffn_conv_w, ffn_conv_b, ffn_w_down):
    weights = (rel_bias, w_ada, b_ada, norm_mix, norm_ffn, attn_w_qkv, attn_q_gain, attn_k_gain, attn_w_o,
               ret_w_in, ret_gn_gain, ret_w_o, ffn_w_up, ffn_conv_w, ffn_conv_b, ffn_w_down)
    dt = x_prompt.dtype
    bp, sp = x_prompt.shape[0], x_prompt.shape[1]
    empty_kv = jnp.zeros((N_ATTN_LAYERS, bp, 0, 2, ATTN_HEADS_PER_GROUP, ATTN_HEAD_DIM), dt)
    zero_ret = jnp.zeros((N_RET_LAYERS, bp, RET_HEADS, RET_KEY_DIM, RET_VALUE_DIM), dt)
    zero_conv = jnp.zeros((DEPTH, bp, CONV_WIDTH - 1, 2 * FFN_DIM), dt)
    y_prompt, p_kv_w128, p_kv_w512, p_kv_w2048, p_ret, p_conv = run_trunk(
        x_prompt, c_prompt, empty_kv, empty_kv, empty_kv, zero_ret, zero_conv,
        jnp.arange(sp, dtype=jnp.int32), *weights)
    y_sample, s_kv_w128, s_kv_w512, s_kv_w2048, s_ret, s_conv = run_trunk(
        x_sample, c_sample, cache_attn_kv_w128, cache_attn_kv_w512, cache_attn_kv_w2048, state_ret, state_conv,
        PAST_LEN + jnp.arange(x_sample.shape[1], dtype=jnp.int32), *weights)
    return (y_prompt, y_sample, p_kv_w128, p_kv_w512, p_kv_w2048, p_ret, p_conv,
            s_kv_w128, s_kv_w512, s_kv_w2048, s_ret, s_conv)
```

```python
import functools
import math

import jax
import jax.numpy as jnp
from jax import lax
from jax.experimental import pallas as pl
from jax.experimental.pallas import tpu as pltpu

D_MODEL = 2048
DEPTH = 2
N_MIXERS = 2

WINDOWS = (128, 512, 2048)
DILATIONS = (1, 4, 16)
N_GROUPS = 3
HEAD_DIM = 128
HEADS = 8
ATTN_WIDTH = HEADS * HEAD_DIM
N_KEYS = 129
KEY_BLOCK = 128
N_BUCKETS = 32
MAX_DISTANCE = 2048
NEG_INF = -1e30

RET_HEADS = 8
RET_DK = 256
RET_DV = 512
RET_QK_WIDTH = RET_HEADS * RET_DK
RET_V_WIDTH = RET_HEADS * RET_DV
RET_CHUNK = 128
ROT_BASE = 10000.0

FFN_DIM = 5504
FFN_TILE = 128
N_FFN_TILES = FFN_DIM // FFN_TILE
NORM_EPS = 1e-6

F32 = jnp.float32
MXU_DTYPE = jnp.bfloat16
VMEM_LIMIT_BYTES = 56 * 1024 * 1024


def _params(*semantics):
    return pltpu.CompilerParams(dimension_semantics=semantics, vmem_limit_bytes=VMEM_LIMIT_BYTES)


def _silu(x):
    return x * jax.nn.sigmoid(x)


def _ada_kernel(c_ref, w_ref, b_ref, o_ref):
    a = _silu(c_ref[...]).astype(MXU_DTYPE)
    w = w_ref[...].astype(MXU_DTYPE)
    o_ref[...] = jnp.dot(a, w, preferred_element_type=F32) + b_ref[...]


def _ada_modulation(c, w_ada, b_ada):
    rows = c.shape[0]
    tn = 1024
    return pl.pallas_call(
        _ada_kernel,
        grid=(DEPTH, 6 * D_MODEL // tn),
        in_specs=[pl.BlockSpec((rows, D_MODEL), lambda l, j: (0, 0)),
                  pl.BlockSpec((None, D_MODEL, tn), lambda l, j: (l, 0, j)),
                  pl.BlockSpec((None, 1, tn), lambda l, j: (l, 0, j))],
        out_specs=pl.BlockSpec((None, rows, tn), lambda l, j: (l, 0, j)),
        out_shape=jax.ShapeDtypeStruct((DEPTH, rows, 6 * D_MODEL), F32),
        compiler_params=_params("arbitrary", "arbitrary"),
        name="ada_modulation",
    )(c, w_ada, b_ada.reshape(DEPTH, 1, 6 * D_MODEL))


def _norm_mod_kernel(x_ref, g_ref, shift_ref, scale_ref, o_ref):
    x = x_ref[...]
    y = x * lax.rsqrt(jnp.mean(x * x, axis=-1, keepdims=True) + NORM_EPS) * g_ref[...]
    o_ref[...] = (y * (1.0 + scale_ref[...]) + shift_ref[...]).astype(o_ref.dtype)


def _mod_spec(mods, tm, col, n_lead):
    per_row = mods.shape[1] > 1
    rb = tm if per_row else 1

    def index(*ids):
        b, i = ids[n_lead], ids[n_lead + 1]
        return (b, i if per_row else 0, col)

    return pl.BlockSpec((None, rb, D_MODEL), index)


def _norm_mod(x, gain, mods, shift_col, scale_col, tm):
    bx, t, d = x.shape
    return pl.pallas_call(
        _norm_mod_kernel,
        grid=(bx, t // tm),
        in_specs=[pl.BlockSpec((None, tm, d), lambda b, i: (b, i, 0)),
                  pl.BlockSpec((1, d), lambda b, i: (0, 0)),
                  _mod_spec(mods, tm, shift_col, 0),
                  _mod_spec(mods, tm, scale_col, 0)],
        out_specs=pl.BlockSpec((None, tm, d), lambda b, i: (b, i, 0)),
        out_shape=jax.ShapeDtypeStruct((bx, t, d), MXU_DTYPE),
        compiler_params=_params("arbitrary", "arbitrary"),
        name="norm_mod",
    )(x, gain.reshape(1, d), mods, mods)


def _cast_weight_once(w_ref, wbf_ref):
    @pl.when((pl.program_id(1) == 0) & (pl.program_id(2) == 0))
    def _():
        wbf_ref[...] = w_ref[...].astype(wbf_ref.dtype)


def _matmul_kernel(a_ref, w_ref, o_ref, wbf_ref):
    _cast_weight_once(w_ref, wbf_ref)
    o_ref[...] = jnp.dot(a_ref[...], wbf_ref[...], preferred_element_type=F32).astype(o_ref.dtype)


def _matmul(a, w, tm, tn, out_dtype=F32):
    bx, t, k = a.shape
    n = w.shape[1]
    return pl.pallas_call(
        _matmul_kernel,
        grid=(n // tn, bx, t // tm),
        in_specs=[pl.BlockSpec((None, tm, k), lambda j, b, i: (b, i, 0)),
                  pl.BlockSpec((k, tn), lambda j, b, i: (0, j))],
        out_specs=pl.BlockSpec((None, tm, tn), lambda j, b, i: (b, i, j)),
        out_shape=jax.ShapeDtypeStruct((bx, t, n), out_dtype),
        scratch_shapes=[pltpu.VMEM((k, tn), MXU_DTYPE)],
        compiler_params=_params("arbitrary", "arbitrary", "arbitrary"),
        name="matmul",
    )(a, w)


def _resid_matmul_kernel(a_ref, w_ref, x_ref, gate_ref, o_ref, wbf_ref):
    _cast_weight_once(w_ref, wbf_ref)
    acc = jnp.dot(a_ref[...], wbf_ref[...], preferred_element_type=F32)
    o_ref[...] = x_ref[...] + gate_ref[...] * acc


def _resid_matmul(a, w, x, mods, gate_col, tm, tn):
    bx, t, k = a.shape
    n = w.shape[1]
    cols_per_group = D_MODEL // tn
    per_row = mods.shape[1] > 1
    rb = tm if per_row else 1
    gate_spec = pl.BlockSpec(
        (None, rb, tn), lambda j, b, i: (b, i if per_row else 0, gate_col * cols_per_group + j))
    return pl.pallas_call(
        _resid_matmul_kernel,
        grid=(n // tn, bx, t // tm),
        in_specs=[pl.BlockSpec((None, tm, k), lambda j, b, i: (b, i, 0)),
                  pl.BlockSpec((k, tn), lambda j, b, i: (0, j)),
                  pl.BlockSpec((None, tm, tn), lambda j, b, i: (b, i, j)),
                  gate_spec],
        out_specs=pl.BlockSpec((None, tm, tn), lambda j, b, i: (b, i, j)),
        out_shape=jax.ShapeDtypeStruct((bx, t, n), F32),
        scratch_shapes=[pltpu.VMEM((k, tn), MXU_DTYPE)],
        compiler_params=_params("arbitrary", "arbitrary", "arbitrary"),
        name="resid_matmul",
    )(a, w, x, mods)


def _qkv_kernel(a_ref, w_ref, gain_ref, o_ref, wbf_ref, y_ref, *, dilation, tm):
    _cast_weight_once(w_ref, wbf_ref)
    acc = jnp.dot(a_ref[...], wbf_ref[...], preferred_element_type=F32)
    part = pl.program_id(0)
    gain = gain_ref[...]

    @pl.when(part < 2)
    def _():
        for h in range(HEADS):
            xh = acc[:, h * HEAD_DIM:(h + 1) * HEAD_DIM]
            y_ref[h] = xh * lax.rsqrt(jnp.mean(xh * xh, axis=-1, keepdims=True) + NORM_EPS) * gain

    @pl.when(part == 2)
    def _():
        for h in range(HEADS):
            y_ref[h] = acc[:, h * HEAD_DIM:(h + 1) * HEAD_DIM]

    rows = tm // dilation
    for c in range(dilation):
        for h in range(HEADS):
            if dilation == 1:
                o_ref[c, h] = y_ref[h]
            else:
                o_ref[c, h] = y_ref[h, pl.ds(c, rows, stride=dilation), :]


def _qkv_project(h, w_qkv, gains, group, dilation, tm):
    bx, t, d = h.shape
    td = t // dilation
    kern = functools.partial(_qkv_kernel, dilation=dilation, tm=tm)
    return pl.pallas_call(
        kern,
        grid=(3, bx, t // tm),
        in_specs=[pl.BlockSpec((None, tm, d), lambda p, b, i: (b, i, 0)),
                  pl.BlockSpec((d, ATTN_WIDTH), lambda p, b, i: (0, group * 3 + p)),
                  pl.BlockSpec((None, 1, HEAD_DIM), lambda p, b, i: (p, 0, 0))],
        out_specs=pl.BlockSpec((None, None, dilation, HEADS, tm // dilation, HEAD_DIM),
                               lambda p, b, i: (b, p, 0, 0, i, 0)),
        out_shape=jax.ShapeDtypeStruct((bx, 3, dilation, HEADS, td, HEAD_DIM), F32),
        scratch_shapes=[pltpu.VMEM((d, ATTN_WIDTH), MXU_DTYPE), pltpu.VMEM((HEADS, tm, HEAD_DIM), F32)],
        compiler_params=_params("arbitrary", "arbitrary", "arbitrary"),
        name=f"qkv_project_g{group}",
    )(h, w_qkv, gains)


def _t5_causal_bucket(dist):
    max_exact = N_BUCKETS // 2
    d = jnp.maximum(dist, 1).astype(F32)
    large = max_exact + (jnp.log(d / max_exact) / math.log(MAX_DISTANCE / max_exact)
                         * (N_BUCKETS - max_exact)).astype(jnp.int32)
    return jnp.where(dist < max_exact, dist, jnp.minimum(large, N_BUCKETS - 1))


def _group_bias(rel_bias, g):
    dist = DILATIONS[g] * jnp.arange(N_KEYS, dtype=jnp.int32)
    cols = rel_bias[:, g * HEADS:(g + 1) * HEADS]
    return jnp.take(cols, _t5_causal_bucket(dist), axis=0).T.astype(F32)


def _band_bias(gb):
    a = jnp.arange(KEY_BLOCK, dtype=jnp.int32)[:, None]
    c = jnp.arange(2 * KEY_BLOCK, dtype=jnp.int32)[None, :]
    rel = a + KEY_BLOCK - c
    valid = (rel >= 0) & (rel <= KEY_BLOCK)
    vals = jnp.take(gb, jnp.clip(rel, 0, KEY_BLOCK), axis=1)
    return jnp.where(valid[None], vals, NEG_INF)


def _window_attn_kernel(q_ref, kp_ref, kc_ref, vp_ref, vc_ref, bias_ref, o_ref, lse_ref,
                        *, dilation, heads):
    i = pl.program_id(1)
    c = pl.program_id(3)
    col = lax.broadcasted_iota(jnp.int32, (KEY_BLOCK, 2 * KEY_BLOCK), 1)
    no_prev = (i == 0) & (col < KEY_BLOCK)
    for h in range(heads):
        q = q_ref[h].astype(MXU_DTYPE)
        k = jnp.concatenate([kp_ref[h], kc_ref[h]], axis=0).astype(MXU_DTYPE)
        v = jnp.concatenate([vp_ref[h], vc_ref[h]], axis=0).astype(MXU_DTYPE)
        s = lax.dot_general(q, k, (((1,), (1,)), ((), ())), preferred_element_type=F32)
        s = s * (HEAD_DIM ** -0.5) + bias_ref[h]
        s = jnp.where(no_prev, NEG_INF, s)
        m = jnp.max(s, axis=-1, keepdims=True)
        p = jnp.exp(s - m)
        l = jnp.sum(p, axis=-1, keepdims=True)
        o = jnp.dot(p.astype(MXU_DTYPE), v, preferred_element_type=F32) / l
        lse = jnp.broadcast_to(m + jnp.log(l), (KEY_BLOCK, HEAD_DIM))
        if dilation == 1:
            o_ref[h] = o
            lse_ref[h] = lse
        else:
            o_ref[h, pl.ds(c, KEY_BLOCK, stride=dilation), :] = o
            lse_ref[h, pl.ds(c, KEY_BLOCK, stride=dilation), :] = lse


def _window_attention(qkv, band_bias, dilation):
    bx, _, _, _, td, _ = qkv.shape
    t = td * dilation
    nblk = td // KEY_BLOCK
    hb = 2
    heads = HEADS // hb

    def spec(part, prev):
        def index(b, i, hq, c):
            return (b, part, c, hq, jnp.maximum(i - 1, 0) if prev else i, 0)
        return pl.BlockSpec((None, None, None, heads, KEY_BLOCK, HEAD_DIM), index)

    out_spec = pl.BlockSpec((None, heads, KEY_BLOCK * dilation, HEAD_DIM), lambda b, i, hq, c: (b, hq, i, 0))
    kern = functools.partial(_window_attn_kernel, dilation=dilation, heads=heads)
    return pl.pallas_call(
        kern,
        grid=(bx, nblk, hb, dilation),
        in_specs=[spec(0, False), spec(1, True), spec(1, False), spec(2, True), spec(2, False),
                  pl.BlockSpec((heads, KEY_BLOCK, 2 * KEY_BLOCK), lambda b, i, hq, c: (hq, 0, 0))],
        out_specs=[out_spec, out_spec],
        out_shape=[jax.ShapeDtypeStruct((bx, HEADS, t, HEAD_DIM), F32)] * 2,
        compiler_params=_params("arbitrary", "arbitrary", "arbitrary", "arbitrary"),
        name=f"window_attention_d{dilation}",
    )(qkv, qkv, qkv, qkv, qkv, band_bias)


def _step_attn_kernel(q_ref, kn_ref, vn_ref, kv_ref, bias0_ref, bias_ref, o_ref, lse_ref,
                      m_ref, l_ref, acc_ref):
    j = pl.program_id(1)
    scale = HEAD_DIM ** -0.5
    q = q_ref[...]

    @pl.when(j == 0)
    def _():
        s0 = jnp.sum(q * kn_ref[...], axis=-1, keepdims=True) * scale + bias0_ref[...]
        m_ref[...] = jnp.broadcast_to(s0, m_ref.shape)
        l_ref[...] = jnp.ones(l_ref.shape, F32)
        acc_ref[...] = vn_ref[...]

    s = jnp.sum(q * kv_ref[:, 0], axis=-1, keepdims=True) * scale + bias_ref[...]
    m_old = m_ref[...]
    m_new = jnp.maximum(m_old, s)
    alpha = jnp.exp(m_old - m_new)
    p = jnp.exp(s - m_new)
    l_ref[...] = alpha * l_ref[...] + p
    acc_ref[...] = alpha * acc_ref[...] + p * kv_ref[:, 1]
    m_ref[...] = m_new

    @pl.when(j == pl.num_programs(1) - 1)
    def _():
        o_ref[...] = acc_ref[...] / l_ref[...]
        lse_ref[...] = m_ref[...] + jnp.log(l_ref[...])


def _step_attention(q, k_new, v_new, cache, layer, gb, dilation, bt):
    b = q.shape[0]
    nk = N_KEYS - 1
    bias = jnp.broadcast_to(gb.T[:, :, None], (N_KEYS, HEADS, HEAD_DIM))
    qspec = pl.BlockSpec((bt, HEADS, HEAD_DIM), lambda bi, j: (bi, 0, 0))
    return pl.pallas_call(
        _step_attn_kernel,
        grid=(b // bt, nk),
        in_specs=[qspec, qspec, qspec,
                  pl.BlockSpec((None, bt, None, 2, HEADS, HEAD_DIM),
                               lambda bi, j: (layer, bi, j * dilation, 0, 0, 0)),
                  pl.BlockSpec((None, HEADS, HEAD_DIM), lambda bi, j: (0, 0, 0)),
                  pl.BlockSpec((None, HEADS, HEAD_DIM), lambda bi, j: (nk - j, 0, 0))],
        out_specs=[qspec, qspec],
        out_shape=[jax.ShapeDtypeStruct((b, HEADS, HEAD_DIM), F32)] * 2,
        scratch_shapes=[pltpu.VMEM((bt, HEADS, HEAD_DIM), F32)] * 3,
        compiler_params=_params("arbitrary", "arbitrary"),
        name=f"step_attention_d{dilation}",
    )(q, k_new, v_new, cache, bias, bias)


def _attn_out_kernel(o0_ref, o1_ref, o2_ref, l0_ref, l1_ref, l2_ref, w_ref, x_ref, gate_ref, out_ref,
                     wbf_ref):
    _cast_weight_once(w_ref, wbf_ref)
    heads = []
    for h in range(HEADS):
        l0, l1, l2 = l0_ref[h], l1_ref[h], l2_ref[h]
        m = jnp.maximum(jnp.maximum(l0, l1), l2)
        e0, e1, e2 = jnp.exp(l0 - m), jnp.exp(l1 - m), jnp.exp(l2 - m)
        tot = e0 + e1 + e2
        merged = (e0 / tot) * o0_ref[h] + (e1 / tot) * o1_ref[h] + (e2 / tot) * o2_ref[h]
        heads.append(merged.astype(MXU_DTYPE))
    acc = jnp.dot(jnp.concatenate(heads, axis=-1), wbf_ref[...], preferred_element_type=F32)
    out_ref[...] = x_ref[...] + gate_ref[...] * acc


def _attn_out(outs, lses, w_o, x, mods, gate_col, tm):
    bx, t, d = x.shape
    per_row = mods.shape[1] > 1
    rb = tm if per_row else 1
    aspec = pl.BlockSpec((None, HEADS, tm, HEAD_DIM), lambda j, b, i: (b, 0, i, 0))
    xspec = pl.BlockSpec((None, tm, d), lambda j, b, i: (b, i, 0))
    return pl.pallas_call(
        _attn_out_kernel,
        grid=(1, bx, t // tm),
        in_specs=[aspec] * 6 + [
            pl.BlockSpec((ATTN_WIDTH, d), lambda j, b, i: (0, 0)),
            xspec,
            pl.BlockSpec((None, rb, d), lambda j, b, i: (b, i if per_row else 0, gate_col))],
        out_specs=xspec,
        out_shape=jax.ShapeDtypeStruct((bx, t, d), F32),
        scratch_shapes=[pltpu.VMEM((ATTN_WIDTH, d), MXU_DTYPE)],
        compiler_params=_params("arbitrary", "arbitrary", "arbitrary"),
        name="attn_out",
    )(*outs, *lses, w_o, x, mods)


SHIFT_CHUNKS = 8


def _cache_shift_kernel(prev_ref, k_ref, v_ref, out_ref, sems, *, layer):
    _, b, w = prev_ref.shape[:3]
    bc = b // SHIFT_CHUNKS
    copies = []
    for c in range(SHIFT_CHUNKS):
        copies.append(pltpu.make_async_copy(
            prev_ref.at[layer, pl.ds(c * bc, bc), pl.ds(1, w - 1)],
            out_ref.at[pl.ds(c * bc, bc), pl.ds(0, w - 1)],
            sems.at[c]))
    copies.append(pltpu.make_async_copy(k_ref, out_ref.at[:, w - 1, 0], sems.at[SHIFT_CHUNKS]))
    copies.append(pltpu.make_async_copy(v_ref, out_ref.at[:, w - 1, 1], sems.at[SHIFT_CHUNKS + 1]))
    for cp in copies:
        cp.start()
    for cp in copies:
        cp.wait()


def _cache_shift(prev, layer, k_new, v_new):
    return pl.pallas_call(
        functools.partial(_cache_shift_kernel, layer=layer),
        in_specs=[pl.BlockSpec(memory_space=pl.ANY),
                  pl.BlockSpec(memory_space=pltpu.VMEM),
                  pl.BlockSpec(memory_space=pltpu.VMEM)],
        out_specs=pl.BlockSpec(memory_space=pl.ANY),
        out_shape=jax.ShapeDtypeStruct(prev.shape[1:], prev.dtype),
        scratch_shapes=[pltpu.SemaphoreType.DMA((SHIFT_CHUNKS + 2,))],
        compiler_params=pltpu.CompilerParams(vmem_limit_bytes=VMEM_LIMIT_BYTES),
        name="cache_shift",
    )(prev, k_new, v_new)


def _rotation_tables(positions):
    half = RET_DK // 2
    inv_freq = 1.0 / (ROT_BASE ** jnp.linspace(0.0, 1.0, half, dtype=F32))
    ang = positions.astype(F32)[:, None] * inv_freq[None, :]
    cos, sin = jnp.cos(ang), jnp.sin(ang)
    cos_rep = jnp.stack([cos, cos], axis=-1).reshape(-1, RET_DK)
    sin_signed = jnp.stack([-sin, sin], axis=-1).reshape(-1, RET_DK)
    return cos_rep, sin_signed


def _rotate_pairs(x, cos_rep, sin_signed):
    lanes = 128
    even = lax.broadcasted_iota(jnp.int32, (x.shape[0], lanes), 1) % 2 == 0
    parts = []
    for s in range(x.shape[1] // lanes):
        xs = x[:, s * lanes:(s + 1) * lanes]
        parts.append(jnp.where(even, pltpu.roll(xs, lanes - 1, 1), pltpu.roll(xs, 1, 1)))
    swapped = jnp.concatenate(parts, axis=-1)
    return x * cos_rep + swapped * sin_signed


def _log_gamma():
    return jnp.log1p(-jnp.exp2(-5.0 - jnp.arange(RET_HEADS, dtype=F32)))


def _group_norm_gate(o, gain, gate):
    mu = jnp.mean(o, axis=-1, keepdims=True)
    var = jnp.mean(jnp.square(o - mu), axis=-1, keepdims=True)
    return _silu(gate) * ((o - mu) * lax.rsqrt(var + NORM_EPS) * gain)


def _retention_kernel(q_ref, k_ref, v_ref, gate_ref, cos_ref, sin_ref, decay_ref, qdec_ref, kdec_ref,
                      cdec_ref, gain_ref, y_ref, s_out_ref, s_ref):
    c = pl.program_id(2)

    @pl.when(c == 0)
    def _():
        s_ref[...] = jnp.zeros(s_ref.shape, F32)

    cos, sin = cos_ref[...], sin_ref[...]
    q = _rotate_pairs(q_ref[...], cos, sin)
    k = _rotate_pairs(k_ref[...], cos, sin) * (RET_DK ** -0.5)
    v = v_ref[...].astype(MXU_DTYPE)
    state = s_ref[...]
    scores = lax.dot_general(q.astype(MXU_DTYPE), k.astype(MXU_DTYPE), (((1,), (1,)), ((), ())),
                             preferred_element_type=F32) * decay_ref[...]
    o = jnp.dot(scores.astype(MXU_DTYPE), v, preferred_element_type=F32)
    o = o + jnp.dot((q * qdec_ref[...]).astype(MXU_DTYPE), state.astype(MXU_DTYPE),
                    preferred_element_type=F32)
    kd_t = jnp.transpose(k * kdec_ref[...]).astype(MXU_DTYPE)
    s_ref[...] = cdec_ref[...] * state + jnp.dot(kd_t, v, preferred_element_type=F32)
    y_ref[...] = _group_norm_gate(o, gain_ref[...], gate_ref[...]).astype(y_ref.dtype)

    @pl.when(c == pl.num_programs(2) - 1)
    def _():
        s_out_ref[...] = s_ref[...]


def _retention_sequence(proj, cos_rep, sin_signed, gn_gain):
    bx, t, _ = proj.shape
    cw = RET_CHUNK
    lg = _log_gamma()
    pos = jnp.arange(cw, dtype=F32)
    diff = pos[:, None] - pos[None, :]
    decay = jnp.where(diff >= 0, jnp.exp(diff[None] * lg[:, None, None]), 0.0)
    q_decay = jnp.exp((pos[:, None] + 1.0) * lg[None, :]).T[:, :, None]
    k_decay = jnp.exp((cw - 1.0 - pos)[:, None] * lg[None, :]).T[:, :, None]
    chunk_decay = jnp.exp(cw * lg).reshape(RET_HEADS, 1, 1)
    nqk = RET_QK_WIDTH // RET_DK
    nv0 = 2 * RET_QK_WIDTH // RET_DV
    ng0 = nv0 + RET_HEADS
    return pl.pallas_call(
        _retention_kernel,
        grid=(bx, RET_HEADS, t // cw),
        in_specs=[pl.BlockSpec((None, cw, RET_DK), lambda b, h, c: (b, c, h)),
                  pl.BlockSpec((None, cw, RET_DK), lambda b, h, c: (b, c, nqk + h)),
                  pl.BlockSpec((None, cw, RET_DV), lambda b, h, c: (b, c, nv0 + h)),
                  pl.BlockSpec((None, cw, RET_DV), lambda b, h, c: (b, c, ng0 + h)),
                  pl.BlockSpec((cw, RET_DK), lambda b, h, c: (c, 0)),
                  pl.BlockSpec((cw, RET_DK), lambda b, h, c: (c, 0)),
                  pl.BlockSpec((None, cw, cw), lambda b, h, c: (h, 0, 0)),
                  pl.BlockSpec((None, cw, 1), lambda b, h, c: (h, 0, 0)),
                  pl.BlockSpec((None, cw, 1), lambda b, h, c: (h, 0, 0)),
                  pl.BlockSpec((None, 1, 1), lambda b, h, c: (h, 0, 0)),
                  pl.BlockSpec((None, 1, RET_DV), lambda b, h, c: (h, 0, 0))],
        out_specs=[pl.BlockSpec((None, cw, RET_DV), lambda b, h, c: (b, c, h)),
                   pl.BlockSpec((None, None, RET_DK, RET_DV), lambda b, h, c: (b, h, 0, 0))],
        out_shape=[jax.ShapeDtypeStruct((bx, t, RET_V_WIDTH), MXU_DTYPE),
                   jax.ShapeDtypeStruct((bx, RET_HEADS, RET_DK, RET_DV), F32)],
        scratch_shapes=[pltpu.VMEM((RET_DK, RET_DV), F32)],
        compiler_params=_params("arbitrary", "arbitrary", "arbitrary"),
        name="retention_sequence",
    )(proj, proj, proj, proj, cos_rep, sin_signed, decay, q_decay, k_decay, chunk_decay,
      gn_gain.reshape(RET_HEADS, 1, RET_DV))


def _retention_step_kernel(q_ref, k_ref, v_ref, gate_ref, cos_ref, sin_ref, gamma_ref, gain_ref, s_ref,
                           y_ref, s_out_ref):
    cos, sin = cos_ref[...], sin_ref[...]
    q = _rotate_pairs(q_ref[...], cos, sin)
    k = _rotate_pairs(k_ref[...], cos, sin) * (RET_DK ** -0.5)
    qk = jnp.sum(q * k, axis=-1, keepdims=True)
    q_t = jnp.transpose(q * gamma_ref[...])
    k_t = jnp.transpose(k)
    v = v_ref[...]
    gamma = gamma_ref[...]
    rows = []
    for h in range(RET_HEADS):
        state = s_ref[h]
        vh = v[h:h + 1, :]
        rows.append(qk[h:h + 1, :] * vh + jnp.sum(q_t[:, h:h + 1] * state, axis=0, keepdims=True))
        s_out_ref[h] = gamma[h:h + 1, :] * state + k_t[:, h:h + 1] * vh
    o = jnp.concatenate(rows, axis=0)
    y_ref[...] = _group_norm_gate(o, gain_ref[...], gate_ref[...]).astype(y_ref.dtype)


def _retention_step(q, k, v, gate, state, layer, cos_rep, sin_signed, gn_gain):
    b = q.shape[0]
    gamma = jnp.exp(_log_gamma()).reshape(RET_HEADS, 1)
    qspec = pl.BlockSpec((None, RET_HEADS, RET_DK), lambda i: (i, 0, 0))
    vspec = pl.BlockSpec((None, RET_HEADS, RET_DV), lambda i: (i, 0, 0))
    sspec = pl.BlockSpec((None, RET_HEADS, RET_DK, RET_DV), lambda i: (i, 0, 0, 0))
    sspec_in = pl.BlockSpec((None, None, RET_HEADS, RET_DK, RET_DV), lambda i: (layer, i, 0, 0, 0))
    return pl.pallas_call(
        _retention_step_kernel,
        grid=(b,),
        in_specs=[qspec, qspec, vspec, vspec,
                  pl.BlockSpec((1, RET_DK), lambda i: (0, 0)),
                  pl.BlockSpec((1, RET_DK), lambda i: (0, 0)),
                  pl.BlockSpec((RET_HEADS, 1), lambda i: (0, 0)),
                  pl.BlockSpec((RET_HEADS, RET_DV), lambda i: (0, 0)),
                  sspec_in],
        out_specs=[vspec, sspec],
        out_shape=[jax.ShapeDtypeStruct((b, RET_HEADS, RET_DV), MXU_DTYPE),
                   jax.ShapeDtypeStruct(state.shape[1:], F32)],
        compiler_params=_params("arbitrary"),
        name="retention_step",
    )(q, k, v, gate, cos_rep, sin_signed, gamma, gn_gain.reshape(RET_HEADS, RET_DV), state)


def _cast_ffn_weights(wg_ref, wv_ref, wbf_ref):
    @pl.when((pl.program_id(1) == 0) & (pl.program_id(2) == 0))
    def _():
        wbf_ref[:, :FFN_TILE] = wg_ref[...].astype(wbf_ref.dtype)
        wbf_ref[:, FFN_TILE:] = wv_ref[...].astype(wbf_ref.dtype)


def _ffn_up_seq_kernel(a_ref, wg_ref, wv_ref, cwg_ref, cwv_ref, cbg_ref, cbv_ref,
                       g_ref, rows_g_ref, rows_v_ref, wbf_ref, u_ref, *, tm):
    i = pl.program_id(2)
    _cast_ffn_weights(wg_ref, wv_ref, wbf_ref)

    @pl.when(i == 0)
    def _():
        u_ref[0:8, :] = jnp.zeros((8, 2 * FFN_TILE), F32)

    u = jnp.dot(a_ref[...], wbf_ref[...], preferred_element_type=F32)
    u_ref[8:8 + tm, :] = u
    cw = jnp.concatenate([cwg_ref[...], cwv_ref[...]], axis=-1)
    cb = jnp.concatenate([cbg_ref[...], cbv_ref[...]], axis=-1)
    z = cb + cw[0:1] * u_ref[6:6 + tm, :]
    z = z + cw[1:2] * u_ref[7:7 + tm, :]
    z = z + cw[2:3] * u
    g_ref[...] = (_silu(z[:, :FFN_TILE]) * z[:, FFN_TILE:]).astype(g_ref.dtype)
    u_ref[0:8, :] = u_ref[tm:tm + 8, :]

    @pl.when(i == pl.num_programs(2) - 1)
    def _():
        rows_g_ref[...] = u[tm - 2:tm, :FFN_TILE]
        rows_v_ref[...] = u[tm - 2:tm, FFN_TILE:]


def _ffn_up_sequence(h, w_up, conv_w, conv_b, tm):
    bx, t, d = h.shape
    nt = N_FFN_TILES
    kern = functools.partial(_ffn_up_seq_kernel, tm=tm)
    conv_b = conv_b.reshape(1, 2 * FFN_DIM)
    g, rows_g, rows_v = pl.pallas_call(
        kern,
        grid=(nt, bx, t // tm),
        in_specs=[pl.BlockSpec((None, tm, d), lambda j, b, i: (b, i, 0)),
                  pl.BlockSpec((d, FFN_TILE), lambda j, b, i: (0, j)),
                  pl.BlockSpec((d, FFN_TILE), lambda j, b, i: (0, nt + j)),
                  pl.BlockSpec((3, FFN_TILE), lambda j, b, i: (0, j)),
                  pl.BlockSpec((3, FFN_TILE), lambda j, b, i: (0, nt + j)),
                  pl.BlockSpec((1, FFN_TILE), lambda j, b, i: (0, j)),
                  pl.BlockSpec((1, FFN_TILE), lambda j, b, i: (0, nt + j))],
        out_specs=[pl.BlockSpec((None, tm, FFN_TILE), lambda j, b, i: (b, i, j)),
                   pl.BlockSpec((None, 2, FFN_TILE), lambda j, b, i: (b, 0, j)),
                   pl.BlockSpec((None, 2, FFN_TILE), lambda j, b, i: (b, 0, j))],
        out_shape=[jax.ShapeDtypeStruct((bx, t, FFN_DIM), MXU_DTYPE),
                   jax.ShapeDtypeStruct((bx, 2, FFN_DIM), F32),
                   jax.ShapeDtypeStruct((bx, 2, FFN_DIM), F32)],
        scratch_shapes=[pltpu.VMEM((d, 2 * FFN_TILE), MXU_DTYPE), pltpu.VMEM((tm + 8, 2 * FFN_TILE), F32)],
        compiler_params=_params("arbitrary", "arbitrary", "arbitrary"),
        name="ffn_up_sequence",
    )(h, w_up, w_up, conv_w, conv_w, conv_b, conv_b)
    return g, jnp.concatenate([rows_g, rows_v], axis=-1)


def _ffn_up_step_kernel(a_ref, wg_ref, wv_ref, cwg_ref, cwv_ref, cbg_ref, cbv_ref, p0g_ref, p0v_ref,
                        p1g_ref, p1v_ref, g_ref, ug_ref, uv_ref):
    w = jnp.concatenate([wg_ref[...], wv_ref[...]], axis=-1).astype(MXU_DTYPE)
    u = jnp.dot(a_ref[...], w, preferred_element_type=F32)
    cw = jnp.concatenate([cwg_ref[...], cwv_ref[...]], axis=-1)
    cb = jnp.concatenate([cbg_ref[...], cbv_ref[...]], axis=-1)
    p0 = jnp.concatenate([p0g_ref[...], p0v_ref[...]], axis=-1)
    p1 = jnp.concatenate([p1g_ref[...], p1v_ref[...]], axis=-1)
    z = cb + cw[0:1] * p0
    z = z + cw[1:2] * p1
    z = z + cw[2:3] * u
    g_ref[...] = (_silu(z[:, :FFN_TILE]) * z[:, FFN_TILE:]).astype(g_ref.dtype)
    ug_ref[...] = u[:, :FFN_TILE]
    uv_ref[...] = u[:, FFN_TILE:]


def _ffn_up_step(h, w_up, conv_w, conv_b, prev0, prev1):
    b, d = h.shape
    nt = N_FFN_TILES
    conv_b = conv_b.reshape(1, 2 * FFN_DIM)
    lo = lambda j: (0, j)
    hi = lambda j: (0, nt + j)
    g, ug, uv = pl.pallas_call(
        _ffn_up_step_kernel,
        grid=(nt,),
        in_specs=[pl.BlockSpec((b, d), lambda j: (0, 0)),
                  pl.BlockSpec((d, FFN_TILE), lo), pl.BlockSpec((d, FFN_TILE), hi),
                  pl.BlockSpec((3, FFN_TILE), lo), pl.BlockSpec((3, FFN_TILE), hi),
                  pl.BlockSpec((1, FFN_TILE), lo), pl.BlockSpec((1, FFN_TILE), hi),
                  pl.BlockSpec((b, FFN_TILE), lo), pl.BlockSpec((b, FFN_TILE), hi),
                  pl.BlockSpec((b, FFN_TILE), lo), pl.BlockSpec((b, FFN_TILE), hi)],
        out_specs=[pl.BlockSpec((b, FFN_TILE), lo)] * 3,
        out_shape=[jax.ShapeDtypeStruct((b, FFN_DIM), MXU_DTYPE),
                   jax.ShapeDtypeStruct((b, FFN_DIM), F32),
                   jax.ShapeDtypeStruct((b, FFN_DIM), F32)],
        compiler_params=_params("arbitrary"),
        name="ffn_up_step",
    )(h, w_up, w_up, conv_w, conv_w, conv_b, conv_b, prev0, prev0, prev1, prev1)
    return g, jnp.concatenate([ug, uv], axis=-1)


def _qk_gains(q_gain, k_gain, g):
    return jnp.stack([q_gain[g], k_gain[g], jnp.ones_like(q_gain[g])])[:, None, :]


def _sequence_trunk(x, mods, w, tm):
    bx, t, d = x.shape
    new_kv, new_ret, new_conv = [[] for _ in range(N_GROUPS)], [], []
    for layer in range(DEPTH):
        m = mods[layer]
        h = _norm_mod(x, w["norm_mix"][layer], m, 0, 1, tm)
        if layer % N_MIXERS == 0:
            a = layer // N_MIXERS
            outs, lses = [], []
            for g in range(N_GROUPS):
                dil = DILATIONS[g]
                qkv = _qkv_project(h, w["attn_w_qkv"][a], _qk_gains(w["attn_q_gain"][a], w["attn_k_gain"][a], g),
                                   g, dil, tm)
                o, lse = _window_attention(qkv, _band_bias(_group_bias(w["rel_bias"], g)), dil)
                outs.append(o)
                lses.append(lse)
                keep = min(WINDOWS[g], t)
                rows = keep // dil
                kv = qkv[:, 1:3, :, :, t // dil - rows:, :]
                kv = jnp.transpose(kv, (0, 4, 2, 1, 3, 5))
                new_kv[g].append(kv.reshape(bx, keep, 2, HEADS, HEAD_DIM))
            x = _attn_out(outs, lses, w["attn_w_o"][a], x, m, 2, min(tm, 256))
        else:
            r = layer // N_MIXERS
            proj = _matmul(h, w["ret_w_in"][r], tm, 1024)
            cos_rep, sin_signed = _rotation_tables(jnp.arange(t, dtype=jnp.int32))
            y, state = _retention_sequence(proj, cos_rep, sin_signed, w["ret_gn_gain"][r])
            new_ret.append(state)
            x = _resid_matmul(y, w["ret_w_o"][r], x, m, 2, tm, 512)
        h = _norm_mod(x, w["norm_ffn"][layer], m, 3, 4, tm)
        g_act, rows = _ffn_up_sequence(h, w["ffn_w_up"][layer], w["ffn_conv_w"][layer], w["ffn_conv_b"][layer],
                                       min(t, 2048))
        new_conv.append(rows)
        x = _resid_matmul(g_act, w["ffn_w_down"][layer], x, m, 5, tm, 512)
    return x, [jnp.stack(kv) for kv in new_kv], jnp.stack(new_ret), jnp.stack(new_conv)


def _step_trunk(x, mods, caches, ret_state, conv_state, position, w):
    b, d = x.shape
    x = x[None]
    new_kv, new_ret, new_conv = [[] for _ in range(N_GROUPS)], [], []
    for layer in range(DEPTH):
        m = mods[layer]
        h = _norm_mod(x, w["norm_mix"][layer], m, 0, 1, b)
        if layer % N_MIXERS == 0:
            a = layer // N_MIXERS
            outs, lses = [], []
            for g in range(N_GROUPS):
                qkv = _qkv_project(h, w["attn_w_qkv"][a], _qk_gains(w["attn_q_gain"][a], w["attn_k_gain"][a], g),
                                   g, 1, b)
                qkv = jnp.transpose(qkv.reshape(3, HEADS, b, HEAD_DIM), (0, 2, 1, 3))
                o, lse = _step_attention(qkv[0], qkv[1], qkv[2], caches[g], a, _group_bias(w["rel_bias"], g),
                                         DILATIONS[g], min(b, 64))
                outs.append(jnp.transpose(o, (1, 0, 2))[None])
                lses.append(jnp.transpose(lse, (1, 0, 2))[None])
                new_kv[g].append(_cache_shift(caches[g], a, qkv[1], qkv[2]))
            x = _attn_out(outs, lses, w["attn_w_o"][a], x, m, 2, b)
        else:
            r = layer // N_MIXERS
            proj = _matmul(h, w["ret_w_in"][r], b, 1024)[0]
            q = proj[:, :RET_QK_WIDTH].reshape(b, RET_HEADS, RET_DK)
            k = proj[:, RET_QK_WIDTH:2 * RET_QK_WIDTH].reshape(b, RET_HEADS, RET_DK)
            v = proj[:, 2 * RET_QK_WIDTH:2 * RET_QK_WIDTH + RET_V_WIDTH].reshape(b, RET_HEADS, RET_DV)
            gate = proj[:, 2 * RET_QK_WIDTH + RET_V_WIDTH:].reshape(b, RET_HEADS, RET_DV)
            cos_rep, sin_signed = _rotation_tables(position)
            y, state = _retention_step(q, k, v, gate, ret_state, r, cos_rep, sin_signed, w["ret_gn_gain"][r])
            new_ret.append(state)
            x = _resid_matmul(y.reshape(1, b, RET_V_WIDTH), w["ret_w_o"][r], x, m, 2, b, 512)
        h = _norm_mod(x, w["norm_ffn"][layer], m, 3, 4, b)
        g_act, u_new = _ffn_up_step(h[0], w["ffn_w_up"][layer], w["ffn_conv_w"][layer], w["ffn_conv_b"][layer],
                                    conv_state[layer, :, 0], conv_state[layer, :, 1])
        new_conv.append(jnp.stack([conv_state[layer, :, 1], u_new], axis=1))
        x = _resid_matmul(g_act[None], w["ffn_w_down"][layer], x, m, 5, b, 512)
    return x[0], [jnp.stack(kv) for kv in new_kv], jnp.stack(new_ret), jnp.stack(new_conv)


def kernel(x_prompt, x_sample, cache_attn_kv_w128, cache_attn_kv_w512, cache_attn_kv_w2048, state_ret,
           state_conv, c_prompt, c_sample, rel_bias, w_ada, b_ada, norm_mix, norm_ffn, attn_w_qkv,
           attn_q_gain, attn_k_gain, attn_w_o, ret_w_in, ret_gn_gain, ret_w_o, ffn_w_up, ffn_conv_w,
           ffn_conv_b, ffn_w_down):
    w = dict(rel_bias=rel_bias, norm_mix=norm_mix, norm_ffn=norm_ffn, attn_w_qkv=attn_w_qkv,
             attn_q_gain=attn_q_gain, attn_k_gain=attn_k_gain, attn_w_o=attn_w_o, ret_w_in=ret_w_in,
             ret_gn_gain=ret_gn_gain, ret_w_o=ret_w_o, ffn_w_up=ffn_w_up, ffn_conv_w=ffn_conv_w,
             ffn_conv_b=ffn_conv_b, ffn_w_down=ffn_w_down)
    bp, sp, d = x_prompt.shape
    bs = x_sample.shape[0]
    past_len = cache_attn_kv_w2048.shape[2]

    rows = bp + bs
    pad = (-rows) % 8
    c_all = jnp.concatenate([c_prompt, c_sample, jnp.zeros((pad, d), F32)], axis=0)
    mods = _ada_modulation(c_all, w_ada, b_ada)
    mods_p = mods[:, :bp].reshape(DEPTH, bp, 1, 6 * d)
    mods_s = mods[:, bp:rows].reshape(DEPTH, 1, bs, 6 * d)

    y_p, kv_p, ret_p, conv_p = _sequence_trunk(x_prompt, mods_p, w, 512)
    caches = (cache_attn_kv_w128, cache_attn_kv_w512, cache_attn_kv_w2048)
    position = past_len + jnp.arange(1, dtype=jnp.int32)
    y_s, kv_s, ret_s, conv_s = _step_trunk(x_sample[:, 0], mods_s, caches, state_ret, state_conv, position, w)

    return (y_p, y_s[:, None, :], kv_p[0], kv_p[1], kv_p[2], ret_p, conv_p,
            kv_s[0], kv_s[1], kv_s[2], ret_s, conv_s)
```

```python
import functools
import math

import jax
import jax.numpy as jnp
from jax import lax
from jax.experimental import pallas as pl
from jax.experimental.pallas import tpu as pltpu

D_MODEL = 2048
DEPTH = 2
N_MIXERS = 2

WINDOWS = (128, 512, 2048)
DILATIONS = (1, 4, 16)
N_GROUPS = 3
HEAD_DIM = 128
HEADS = 8
ATTN_WIDTH = HEADS * HEAD_DIM
N_KEYS = 129
KEY_BLOCK = 128
N_BUCKETS = 32
MAX_DISTANCE = 2048
NEG_INF = -1e30

RET_HEADS = 8
RET_DK = 256
RET_DV = 512
RET_QK_WIDTH = RET_HEADS * RET_DK
RET_V_WIDTH = RET_HEADS * RET_DV
RET_CHUNK = 128
RET_HEAD_GROUP = 2
ROT_BASE = 10000.0

FFN_DIM = 5504
FFN_TILE = 128
N_FFN_TILES = FFN_DIM // FFN_TILE
NORM_EPS = 1e-6

F32 = jnp.float32
MXU_DTYPE = jnp.bfloat16
VMEM_LIMIT_BYTES = 56 * 1024 * 1024


def _params(*semantics):
    return pltpu.CompilerParams(dimension_semantics=semantics, vmem_limit_bytes=VMEM_LIMIT_BYTES)


def _silu(x):
    return x * jax.nn.sigmoid(x)


def _ada_kernel(c_ref, w_ref, b_ref, o_ref):
    a = _silu(c_ref[...]).astype(MXU_DTYPE)
    w = w_ref[...].astype(MXU_DTYPE)
    o_ref[...] = jnp.dot(a, w, preferred_element_type=F32) + b_ref[...]


def _ada_modulation(c, w_ada, b_ada):
    rows = c.shape[0]
    tn = 1024
    return pl.pallas_call(
        _ada_kernel,
        grid=(DEPTH, 6 * D_MODEL // tn),
        in_specs=[pl.BlockSpec((rows, D_MODEL), lambda l, j: (0, 0)),
                  pl.BlockSpec((None, D_MODEL, tn), lambda l, j: (l, 0, j)),
                  pl.BlockSpec((None, 1, tn), lambda l, j: (l, 0, j))],
        out_specs=pl.BlockSpec((None, rows, tn), lambda l, j: (l, 0, j)),
        out_shape=jax.ShapeDtypeStruct((DEPTH, rows, 6 * D_MODEL), F32),
        compiler_params=_params("arbitrary", "arbitrary"),
        name="ada_modulation",
    )(c, w_ada, b_ada.reshape(DEPTH, 1, 6 * D_MODEL))


def _norm_mod_kernel(x_ref, g_ref, shift_ref, scale_ref, o_ref):
    x = x_ref[...]
    y = x * lax.rsqrt(jnp.mean(x * x, axis=-1, keepdims=True) + NORM_EPS) * g_ref[...]
    o_ref[...] = (y * (1.0 + scale_ref[...]) + shift_ref[...]).astype(o_ref.dtype)


def _mod_spec(mods, tm, col, n_lead):
    per_row = mods.shape[1] > 1
    rb = tm if per_row else 1

    def index(*ids):
        b, i = ids[n_lead], ids[n_lead + 1]
        return (b, i if per_row else 0, col)

    return pl.BlockSpec((None, rb, D_MODEL), index)


def _norm_mod(x, gain, mods, shift_col, scale_col, tm):
    bx, t, d = x.shape
    return pl.pallas_call(
        _norm_mod_kernel,
        grid=(bx, t // tm),
        in_specs=[pl.BlockSpec((None, tm, d), lambda b, i: (b, i, 0)),
                  pl.BlockSpec((1, d), lambda b, i: (0, 0)),
                  _mod_spec(mods, tm, shift_col, 0),
                  _mod_spec(mods, tm, scale_col, 0)],
        out_specs=pl.BlockSpec((None, tm, d), lambda b, i: (b, i, 0)),
        out_shape=jax.ShapeDtypeStruct((bx, t, d), MXU_DTYPE),
        compiler_params=_params("arbitrary", "arbitrary"),
        name="norm_mod",
    )(x, gain.reshape(1, d), mods, mods)


def _cast_weight_once(w_ref, wbf_ref):
    @pl.when((pl.program_id(1) == 0) & (pl.program_id(2) == 0))
    def _():
        wbf_ref[...] = w_ref[...].astype(wbf_ref.dtype)


def _matmul_kernel(a_ref, w_ref, o_ref, wbf_ref):
    _cast_weight_once(w_ref, wbf_ref)
    o_ref[...] = jnp.dot(a_ref[...], wbf_ref[...], preferred_element_type=F32).astype(o_ref.dtype)


def _matmul(a, w, tm, tn, out_dtype=F32):
    bx, t, k = a.shape
    n = w.shape[1]
    return pl.pallas_call(
        _matmul_kernel,
        grid=(n // tn, bx, t // tm),
        in_specs=[pl.BlockSpec((None, tm, k), lambda j, b, i: (b, i, 0)),
                  pl.BlockSpec((k, tn), lambda j, b, i: (0, j))],
        out_specs=pl.BlockSpec((None, tm, tn), lambda j, b, i: (b, i, j)),
        out_shape=jax.ShapeDtypeStruct((bx, t, n), out_dtype),
        scratch_shapes=[pltpu.VMEM((k, tn), MXU_DTYPE)],
        compiler_params=_params("arbitrary", "arbitrary", "arbitrary"),
        name="matmul",
    )(a, w)


def _resid_matmul_kernel(a_ref, w_ref, x_ref, gate_ref, o_ref, wbf_ref):
    _cast_weight_once(w_ref, wbf_ref)
    acc = jnp.dot(a_ref[...], wbf_ref[...], preferred_element_type=F32)
    o_ref[...] = x_ref[...] + gate_ref[...] * acc


def _resid_matmul(a, w, x, mods, gate_col, tm, tn):
    bx, t, k = a.shape
    n = w.shape[1]
    cols_per_group = D_MODEL // tn
    per_row = mods.shape[1] > 1
    rb = tm if per_row else 1
    gate_spec = pl.BlockSpec(
        (None, rb, tn), lambda j, b, i: (b, i if per_row else 0, gate_col * cols_per_group + j))
    return pl.pallas_call(
        _resid_matmul_kernel,
        grid=(n // tn, bx, t // tm),
        in_specs=[pl.BlockSpec((None, tm, k), lambda j, b, i: (b, i, 0)),
                  pl.BlockSpec((k, tn), lambda j, b, i: (0, j)),
                  pl.BlockSpec((None, tm, tn), lambda j, b, i: (b, i, j)),
                  gate_spec],
        out_specs=pl.BlockSpec((None, tm, tn), lambda j, b, i: (b, i, j)),
        out_shape=jax.ShapeDtypeStruct((bx, t, n), F32),
        scratch_shapes=[pltpu.VMEM((k, tn), MXU_DTYPE)],
        compiler_params=_params("arbitrary", "arbitrary", "arbitrary"),
        name="resid_matmul",
    )(a, w, x, mods)


def _qkv_kernel(a_ref, w_ref, gain_ref, o_ref, wbf_ref, y_ref, *, dilation, tm):
    _cast_weight_once(w_ref, wbf_ref)
    acc = jnp.dot(a_ref[...], wbf_ref[...], preferred_element_type=F32)
    part = pl.program_id(0)
    gain = gain_ref[...]

    @pl.when(part < 2)
    def _():
        for h in range(HEADS):
            xh = acc[:, h * HEAD_DIM:(h + 1) * HEAD_DIM]
            y_ref[h] = xh * lax.rsqrt(jnp.mean(xh * xh, axis=-1, keepdims=True) + NORM_EPS) * gain

    @pl.when(part == 2)
    def _():
        for h in range(HEADS):
            y_ref[h] = acc[:, h * HEAD_DIM:(h + 1) * HEAD_DIM]

    rows = tm // dilation
    for c in range(dilation):
        for h in range(HEADS):
            if dilation == 1:
                o_ref[c, h] = y_ref[h]
            else:
                o_ref[c, h] = y_ref[h, pl.ds(c, rows, stride=dilation), :]


def _qkv_project(h, w_qkv, gains, group, dilation, tm):
    bx, t, d = h.shape
    td = t // dilation
    kern = functools.partial(_qkv_kernel, dilation=dilation, tm=tm)
    return pl.pallas_call(
        kern,
        grid=(3, bx, t // tm),
        in_specs=[pl.BlockSpec((None, tm, d), lambda p, b, i: (b, i, 0)),
                  pl.BlockSpec((d, ATTN_WIDTH), lambda p, b, i: (0, group * 3 + p)),
                  pl.BlockSpec((None, 1, HEAD_DIM), lambda p, b, i: (p, 0, 0))],
        out_specs=pl.BlockSpec((None, None, dilation, HEADS, tm // dilation, HEAD_DIM),
                               lambda p, b, i: (b, p, 0, 0, i, 0)),
        out_shape=jax.ShapeDtypeStruct((bx, 3, dilation, HEADS, td, HEAD_DIM), F32),
        scratch_shapes=[pltpu.VMEM((d, ATTN_WIDTH), MXU_DTYPE), pltpu.VMEM((HEADS, tm, HEAD_DIM), F32)],
        compiler_params=_params("arbitrary", "arbitrary", "arbitrary"),
        name=f"qkv_project_g{group}",
    )(h, w_qkv, gains)


def _t5_causal_bucket(dist):
    max_exact = N_BUCKETS // 2
    d = jnp.maximum(dist, 1).astype(F32)
    large = max_exact + (jnp.log(d / max_exact) / math.log(MAX_DISTANCE / max_exact)
                         * (N_BUCKETS - max_exact)).astype(jnp.int32)
    return jnp.where(dist < max_exact, dist, jnp.minimum(large, N_BUCKETS - 1))


def _bucket_lookup(rel_bias, g, bucket, fill):
    cols = rel_bias[:, g * HEADS:(g + 1) * HEADS].astype(F32)
    expand = (slice(None),) + (None,) * bucket.ndim
    out = jnp.full((HEADS,) + bucket.shape, fill, F32)
    for b in range(N_BUCKETS):
        out = jnp.where(bucket[None] == b, cols[b][expand], out)
    return out


def _group_bias(rel_bias, g):
    dist = DILATIONS[g] * jnp.arange(N_KEYS, dtype=jnp.int32)
    return _bucket_lookup(rel_bias, g, _t5_causal_bucket(dist), 0.0)


def _band_bias(rel_bias, g):
    a = jnp.arange(KEY_BLOCK, dtype=jnp.int32)[:, None]
    c = jnp.arange(2 * KEY_BLOCK, dtype=jnp.int32)[None, :]
    rel = a + KEY_BLOCK - c
    valid = (rel >= 0) & (rel <= KEY_BLOCK)
    bucket = jnp.where(valid, _t5_causal_bucket(DILATIONS[g] * jnp.clip(rel, 0, KEY_BLOCK)), -1)
    return _bucket_lookup(rel_bias, g, bucket, NEG_INF)


def _window_attn_kernel(q_ref, kp_ref, kc_ref, vp_ref, vc_ref, bias_ref, o_ref, lse_ref,
                        *, dilation, heads):
    i = pl.program_id(1)
    c = pl.program_id(3)
    col = lax.broadcasted_iota(jnp.int32, (KEY_BLOCK, 2 * KEY_BLOCK), 1)
    no_prev = (i == 0) & (col < KEY_BLOCK)
    for h in range(heads):
        q = q_ref[h].astype(MXU_DTYPE)
        k = jnp.concatenate([kp_ref[h], kc_ref[h]], axis=0).astype(MXU_DTYPE)
        v = jnp.concatenate([vp_ref[h], vc_ref[h]], axis=0).astype(MXU_DTYPE)
        s = lax.dot_general(q, k, (((1,), (1,)), ((), ())), preferred_element_type=F32)
        s = s * (HEAD_DIM ** -0.5) + bias_ref[h]
        s = jnp.where(no_prev, NEG_INF, s)
        m = jnp.max(s, axis=-1, keepdims=True)
        p = jnp.exp(s - m)
        l = jnp.sum(p, axis=-1, keepdims=True)
        o = jnp.dot(p.astype(MXU_DTYPE), v, preferred_element_type=F32) / l
        lse = jnp.broadcast_to(m + jnp.log(l), (KEY_BLOCK, HEAD_DIM))
        if dilation == 1:
            o_ref[h] = o
            lse_ref[h] = lse
        else:
            o_ref[h, pl.ds(c, KEY_BLOCK, stride=dilation), :] = o
            lse_ref[h, pl.ds(c, KEY_BLOCK, stride=dilation), :] = lse


def _window_attention(qkv, band_bias, dilation):
    bx, _, _, _, td, _ = qkv.shape
    t = td * dilation
    nblk = td // KEY_BLOCK
    hb = 1
    heads = HEADS // hb

    def spec(part, prev):
        def index(b, i, hq, c):
            return (b, part, c, hq, jnp.maximum(i - 1, 0) if prev else i, 0)
        return pl.BlockSpec((None, None, None, heads, KEY_BLOCK, HEAD_DIM), index)

    out_spec = pl.BlockSpec((None, heads, KEY_BLOCK * dilation, HEAD_DIM), lambda b, i, hq, c: (b, hq, i, 0))
    kern = functools.partial(_window_attn_kernel, dilation=dilation, heads=heads)
    return pl.pallas_call(
        kern,
        grid=(bx, nblk, hb, dilation),
        in_specs=[spec(0, False), spec(1, True), spec(1, False), spec(2, True), spec(2, False),
                  pl.BlockSpec((heads, KEY_BLOCK, 2 * KEY_BLOCK), lambda b, i, hq, c: (hq, 0, 0))],
        out_specs=[out_spec, out_spec],
        out_shape=[jax.ShapeDtypeStruct((bx, HEADS, t, HEAD_DIM), F32)] * 2,
        compiler_params=_params("arbitrary", "arbitrary", "arbitrary", "arbitrary"),
        name=f"window_attention_d{dilation}",
    )(qkv, qkv, qkv, qkv, qkv, band_bias)


STEP_ROWS = 8
STEP_SEQS = 8


def _step_attn_kernel(*refs):
    q_ref, kn_ref, vn_ref = refs[:3]
    kv_refs = refs[3:3 + STEP_ROWS]
    bias0_ref, bias_ref, o_ref, lse_ref, m_ref, l_ref, acc_ref = refs[3 + STEP_ROWS:]
    j = pl.program_id(1)
    scale = HEAD_DIM ** -0.5

    def chunk(ci, carry):
        sl = pl.ds(pl.multiple_of(ci * STEP_SEQS, STEP_SEQS), STEP_SEQS)
        q = q_ref[sl]

        @pl.when(j == 0)
        def _():
            s0 = jnp.sum(q * kn_ref[sl], axis=-1, keepdims=True) * scale + bias0_ref[...]
            m_ref[sl] = jnp.broadcast_to(s0, q.shape)
            l_ref[sl] = jnp.ones(q.shape, F32)
            acc_ref[sl] = vn_ref[sl]

        scores = [jnp.sum(q * kv_refs[r][sl, 0], axis=-1, keepdims=True) * scale + bias_ref[r]
                  for r in range(STEP_ROWS)]
        m_old = m_ref[sl]
        m_new = m_old
        for s in scores:
            m_new = jnp.maximum(m_new, s)
        alpha = jnp.exp(m_old - m_new)
        l_new = alpha * l_ref[sl]
        acc = alpha * acc_ref[sl]
        for r in range(STEP_ROWS):
            p = jnp.exp(scores[r] - m_new)
            l_new = l_new + p
            acc = acc + p * kv_refs[r][sl, 1]
        m_ref[sl] = m_new
        l_ref[sl] = l_new
        acc_ref[sl] = acc
        return carry

    n_chunks = q_ref.shape[0] // STEP_SEQS
    lax.fori_loop(0, n_chunks, chunk, 0, unroll=2 if n_chunks % 2 == 0 else 1)

    @pl.when(j == pl.num_programs(1) - 1)
    def _():
        o_ref[...] = acc_ref[...] / l_ref[...]
        lse_ref[...] = m_ref[...] + jnp.log(l_ref[...])


def _step_attention(q, k_new, v_new, cache, layer, gb, dilation, bt):
    b = q.shape[0]
    nk = N_KEYS - 1
    lanes = jnp.broadcast_to(gb.T[:, :, None], (N_KEYS, HEADS, HEAD_DIM))
    bias0 = lanes[0]
    bias_rows = lanes[:0:-1]
    qspec = pl.BlockSpec((bt, HEADS, HEAD_DIM), lambda bi, j: (bi, 0, 0))

    def row_spec(r):
        return pl.BlockSpec((None, bt, None, 2, HEADS, HEAD_DIM),
                            lambda bi, j: (layer, bi, (j * STEP_ROWS + r) * dilation, 0, 0, 0))

    return pl.pallas_call(
        _step_attn_kernel,
        grid=(b // bt, nk // STEP_ROWS),
        in_specs=[qspec, qspec, qspec] + [row_spec(r) for r in range(STEP_ROWS)] + [
            pl.BlockSpec((HEADS, HEAD_DIM), lambda bi, j: (0, 0)),
            pl.BlockSpec((STEP_ROWS, HEADS, HEAD_DIM), lambda bi, j: (j, 0, 0))],
        out_specs=[qspec, qspec],
        out_shape=[jax.ShapeDtypeStruct((b, HEADS, HEAD_DIM), F32)] * 2,
        scratch_shapes=[pltpu.VMEM((bt, HEADS, HEAD_DIM), F32)] * 3,
        compiler_params=_params("arbitrary", "arbitrary"),
        name=f"step_attention_d{dilation}",
    )(q, k_new, v_new, *([cache] * STEP_ROWS), bias0, bias_rows)


def _attn_out_kernel(o0_ref, o1_ref, o2_ref, l0_ref, l1_ref, l2_ref, w_ref, x_ref, gate_ref, out_ref,
                     wbf_ref):
    _cast_weight_once(w_ref, wbf_ref)
    heads = []
    for h in range(HEADS):
        l0, l1, l2 = l0_ref[h], l1_ref[h], l2_ref[h]
        m = jnp.maximum(jnp.maximum(l0, l1), l2)
        e0, e1, e2 = jnp.exp(l0 - m), jnp.exp(l1 - m), jnp.exp(l2 - m)
        tot = e0 + e1 + e2
        merged = (e0 / tot) * o0_ref[h] + (e1 / tot) * o1_ref[h] + (e2 / tot) * o2_ref[h]
        heads.append(merged.astype(MXU_DTYPE))
    acc = jnp.dot(jnp.concatenate(heads, axis=-1), wbf_ref[...], preferred_element_type=F32)
    out_ref[...] = x_ref[...] + gate_ref[...] * acc


def _attn_out(outs, lses, w_o, x, mods, gate_col, tm):
    bx, t, d = x.shape
    per_row = mods.shape[1] > 1
    rb = tm if per_row else 1
    aspec = pl.BlockSpec((None, HEADS, tm, HEAD_DIM), lambda j, b, i: (b, 0, i, 0))
    xspec = pl.BlockSpec((None, tm, d), lambda j, b, i: (b, i, 0))
    return pl.pallas_call(
        _attn_out_kernel,
        grid=(1, bx, t // tm),
        in_specs=[aspec] * 6 + [
            pl.BlockSpec((ATTN_WIDTH, d), lambda j, b, i: (0, 0)),
            xspec,
            pl.BlockSpec((None, rb, d), lambda j, b, i: (b, i if per_row else 0, gate_col))],
        out_specs=xspec,
        out_shape=jax.ShapeDtypeStruct((bx, t, d), F32),
        scratch_shapes=[pltpu.VMEM((ATTN_WIDTH, d), MXU_DTYPE)],
        compiler_params=_params("arbitrary", "arbitrary", "arbitrary"),
        name="attn_out",
    )(*outs, *lses, w_o, x, mods)


SHIFT_CHUNKS = 8


KV_ROWS = 2 * HEADS


def _cache_shift_kernel(prev_ref, new_ref, out_ref, sems, *, layer):
    b, rows, _ = out_ref.shape
    keep = rows - KV_ROWS
    bc = b // SHIFT_CHUNKS
    copies = []
    for c in range(SHIFT_CHUNKS):
        copies.append(pltpu.make_async_copy(
            prev_ref.at[pl.ds(layer * b + c * bc, bc), pl.ds(KV_ROWS, keep)],
            out_ref.at[pl.ds(c * bc, bc), pl.ds(0, keep)],
            sems.at[c]))
    copies.append(pltpu.make_async_copy(new_ref, out_ref.at[:, pl.ds(keep, KV_ROWS)], sems.at[SHIFT_CHUNKS]))
    for cp in copies:
        cp.start()
    for cp in copies:
        cp.wait()


def _cache_shift(prev, layer, k_new, v_new):
    nl, b, w = prev.shape[:3]
    prev_rows = prev.reshape(nl * b, w * KV_ROWS, HEAD_DIM)
    new_rows = jnp.concatenate([k_new, v_new], axis=1)
    out = pl.pallas_call(
        functools.partial(_cache_shift_kernel, layer=layer),
        in_specs=[pl.BlockSpec(memory_space=pl.ANY),
                  pl.BlockSpec(memory_space=pltpu.VMEM)],
        out_specs=pl.BlockSpec(memory_space=pl.ANY),
        out_shape=jax.ShapeDtypeStruct((b, w * KV_ROWS, HEAD_DIM), prev.dtype),
        scratch_shapes=[pltpu.SemaphoreType.DMA((SHIFT_CHUNKS + 1,))],
        compiler_params=pltpu.CompilerParams(vmem_limit_bytes=VMEM_LIMIT_BYTES),
        name="cache_shift",
    )(prev_rows, new_rows)
    return out.reshape(b, w, 2, HEADS, HEAD_DIM)


def _rotation_tables(positions):
    half = RET_DK // 2
    inv_freq = 1.0 / (ROT_BASE ** jnp.linspace(0.0, 1.0, half, dtype=F32))
    ang = positions.astype(F32)[:, None] * inv_freq[None, :]
    cos, sin = jnp.cos(ang), jnp.sin(ang)
    cos_rep = jnp.stack([cos, cos], axis=-1).reshape(-1, RET_DK)
    sin_signed = jnp.stack([-sin, sin], axis=-1).reshape(-1, RET_DK)
    return cos_rep, sin_signed


def _rotate_pairs(x, cos_rep, sin_signed):
    lanes = 128
    even = lax.broadcasted_iota(jnp.int32, (x.shape[0], lanes), 1) % 2 == 0
    parts = []
    for s in range(x.shape[1] // lanes):
        xs = x[:, s * lanes:(s + 1) * lanes]
        parts.append(jnp.where(even, pltpu.roll(xs, lanes - 1, 1), pltpu.roll(xs, 1, 1)))
    swapped = jnp.concatenate(parts, axis=-1)
    return x * cos_rep + swapped * sin_signed


def _log_gamma():
    return jnp.log1p(-jnp.exp2(-5.0 - jnp.arange(RET_HEADS, dtype=F32)))


def _group_norm_gate(o, gain, gate):
    mu = jnp.mean(o, axis=-1, keepdims=True)
    var = jnp.mean(jnp.square(o - mu), axis=-1, keepdims=True)
    return _silu(gate) * ((o - mu) * lax.rsqrt(var + NORM_EPS) * gain)


def _retention_kernel(q_ref, k_ref, v_ref, gate_ref, cos_ref, sin_ref, decay_ref, qdec_ref, kdec_ref,
                      cdec_ref, gain_ref, y_ref, s_out_ref, s_ref):
    c = pl.program_id(1)
    nb, hg = s_ref.shape[:2]

    @pl.when(c == 0)
    def _():
        s_ref[...] = jnp.zeros(s_ref.shape, F32)

    cos, sin = cos_ref[...], sin_ref[...]
    for b in range(nb):
        for hh in range(hg):
            ksl = slice(hh * RET_DK, (hh + 1) * RET_DK)
            vsl = slice(hh * RET_DV, (hh + 1) * RET_DV)
            q = _rotate_pairs(q_ref[b, :, ksl], cos, sin)
            k = _rotate_pairs(k_ref[b, :, ksl], cos, sin) * (RET_DK ** -0.5)
            v = v_ref[b, :, vsl].astype(MXU_DTYPE)
            state = s_ref[b, hh]
            scores = lax.dot_general(q.astype(MXU_DTYPE), k.astype(MXU_DTYPE), (((1,), (1,)), ((), ())),
                                     preferred_element_type=F32) * decay_ref[hh]
            o = jnp.dot(scores.astype(MXU_DTYPE), v, preferred_element_type=F32)
            o = o + jnp.dot((q * qdec_ref[hh]).astype(MXU_DTYPE), state.astype(MXU_DTYPE),
                            preferred_element_type=F32)
            kd_t = jnp.transpose(k * kdec_ref[hh]).astype(MXU_DTYPE)
            s_ref[b, hh] = cdec_ref[hh] * state + jnp.dot(kd_t, v, preferred_element_type=F32)
            y_ref[b, :, vsl] = _group_norm_gate(o, gain_ref[hh], gate_ref[b, :, vsl]).astype(y_ref.dtype)

    @pl.when(c == pl.num_programs(1) - 1)
    def _():
        s_out_ref[...] = s_ref[...]


def _retention_sequence(proj, cos_rep, sin_signed, gn_gain):
    bx, t, _ = proj.shape
    cw = RET_CHUNK
    lg = _log_gamma()
    pos = jnp.arange(cw, dtype=F32)
    diff = pos[:, None] - pos[None, :]
    decay = jnp.where(diff >= 0, jnp.exp(diff[None] * lg[:, None, None]), 0.0)
    q_decay = jnp.exp((pos[:, None] + 1.0) * lg[None, :]).T[:, :, None]
    k_decay = jnp.exp((cw - 1.0 - pos)[:, None] * lg[None, :]).T[:, :, None]
    chunk_decay = jnp.exp(cw * lg).reshape(RET_HEADS, 1, 1)
    hg = RET_HEAD_GROUP
    nqk = RET_QK_WIDTH // (hg * RET_DK)
    nv0 = 2 * RET_QK_WIDTH // (hg * RET_DV)
    ng0 = nv0 + RET_HEADS // hg
    return pl.pallas_call(
        _retention_kernel,
        grid=(RET_HEADS // hg, t // cw),
        in_specs=[pl.BlockSpec((bx, cw, hg * RET_DK), lambda h, c: (0, c, h)),
                  pl.BlockSpec((bx, cw, hg * RET_DK), lambda h, c: (0, c, nqk + h)),
                  pl.BlockSpec((bx, cw, hg * RET_DV), lambda h, c: (0, c, nv0 + h)),
                  pl.BlockSpec((bx, cw, hg * RET_DV), lambda h, c: (0, c, ng0 + h)),
                  pl.BlockSpec((cw, RET_DK), lambda h, c: (c, 0)),
                  pl.BlockSpec((cw, RET_DK), lambda h, c: (c, 0)),
                  pl.BlockSpec((hg, cw, cw), lambda h, c: (h, 0, 0)),
                  pl.BlockSpec((hg, cw, 1), lambda h, c: (h, 0, 0)),
                  pl.BlockSpec((hg, cw, 1), lambda h, c: (h, 0, 0)),
                  pl.BlockSpec((hg, 1, 1), lambda h, c: (h, 0, 0)),
                  pl.BlockSpec((hg, 1, RET_DV), lambda h, c: (h, 0, 0))],
        out_specs=[pl.BlockSpec((bx, cw, hg * RET_DV), lambda h, c: (0, c, h)),
                   pl.BlockSpec((bx, hg, RET_DK, RET_DV), lambda h, c: (0, h, 0, 0))],
        out_shape=[jax.ShapeDtypeStruct((bx, t, RET_V_WIDTH), MXU_DTYPE),
                   jax.ShapeDtypeStruct((bx, RET_HEADS, RET_DK, RET_DV), F32)],
        scratch_shapes=[pltpu.VMEM((bx, hg, RET_DK, RET_DV), F32)],
        compiler_params=_params("arbitrary", "arbitrary"),
        name="retention_sequence",
    )(proj, proj, proj, proj, cos_rep, sin_signed, decay, q_decay, k_decay, chunk_decay,
      gn_gain.reshape(RET_HEADS, 1, RET_DV))


def _retention_step_kernel(q_ref, k_ref, v_ref, gate_ref, cos_ref, sin_ref, gamma_ref, gain_ref, s_ref,
                           y_ref, s_out_ref):
    cos, sin = cos_ref[...], sin_ref[...]
    q = _rotate_pairs(q_ref[...], cos, sin)
    k = _rotate_pairs(k_ref[...], cos, sin) * (RET_DK ** -0.5)
    qk = jnp.sum(q * k, axis=-1, keepdims=True)
    q_t = jnp.transpose(q * gamma_ref[...])
    k_t = jnp.transpose(k)
    v = v_ref[...]
    gamma = gamma_ref[...]
    rows = []
    for h in range(RET_HEADS):
        state = s_ref[h]
        vh = v[h:h + 1, :]
        rows.append(qk[h:h + 1, :] * vh + jnp.sum(q_t[:, h:h + 1] * state, axis=0, keepdims=True))
        s_out_ref[h] = gamma[h:h + 1, :] * state + k_t[:, h:h + 1] * vh
    o = jnp.concatenate(rows, axis=0)
    y_ref[...] = _group_norm_gate(o, gain_ref[...], gate_ref[...]).astype(y_ref.dtype)


def _retention_step(q, k, v, gate, state, layer, cos_rep, sin_signed, gn_gain):
    b = q.shape[0]
    gamma = jnp.exp(_log_gamma()).reshape(RET_HEADS, 1)
    qspec = pl.BlockSpec((None, RET_HEADS, RET_DK), lambda i: (i, 0, 0))
    vspec = pl.BlockSpec((None, RET_HEADS, RET_DV), lambda i: (i, 0, 0))
    sspec = pl.BlockSpec((None, RET_HEADS, RET_DK, RET_DV), lambda i: (i, 0, 0, 0))
    sspec_in = pl.BlockSpec((None, None, RET_HEADS, RET_DK, RET_DV), lambda i: (layer, i, 0, 0, 0))
    return pl.pallas_call(
        _retention_step_kernel,
        grid=(b,),
        in_specs=[qspec, qspec, vspec, vspec,
                  pl.BlockSpec((1, RET_DK), lambda i: (0, 0)),
                  pl.BlockSpec((1, RET_DK), lambda i: (0, 0)),
                  pl.BlockSpec((RET_HEADS, 1), lambda i: (0, 0)),
                  pl.BlockSpec((RET_HEADS, RET_DV), lambda i: (0, 0)),
                  sspec_in],
        out_specs=[vspec, sspec],
        out_shape=[jax.ShapeDtypeStruct((b, RET_HEADS, RET_DV), MXU_DTYPE),
                   jax.ShapeDtypeStruct(state.shape[1:], F32)],
        compiler_params=_params("arbitrary"),
        name="retention_step",
    )(q, k, v, gate, cos_rep, sin_signed, gamma, gn_gain.reshape(RET_HEADS, RET_DV), state)


def _ffn_up_seq_kernel(a_ref, wg_ref, wv_ref, cwg_ref, cwv_ref, cbg_ref, cbv_ref,
                       g_ref, rows_g_ref, rows_v_ref, wbf_ref, u_ref, *, t, chunk):
    wbf_ref[:, :FFN_TILE] = wg_ref[...].astype(wbf_ref.dtype)
    wbf_ref[:, FFN_TILE:] = wv_ref[...].astype(wbf_ref.dtype)
    u_ref[0:8, :] = jnp.zeros((8, 2 * FFN_TILE), F32)
    cw = jnp.concatenate([cwg_ref[...], cwv_ref[...]], axis=-1)
    cb = jnp.concatenate([cbg_ref[...], cbv_ref[...]], axis=-1)
    for s in range(t // chunk):
        r0 = s * chunk
        u = jnp.dot(a_ref[r0:r0 + chunk, :], wbf_ref[...], preferred_element_type=F32)
        u_ref[8 + r0:8 + r0 + chunk, :] = u
        z = cb + cw[0:1] * u_ref[6 + r0:6 + r0 + chunk, :]
        z = z + cw[1:2] * u_ref[7 + r0:7 + r0 + chunk, :]
        z = z + cw[2:3] * u
        g_ref[r0:r0 + chunk, :] = (_silu(z[:, :FFN_TILE]) * z[:, FFN_TILE:]).astype(g_ref.dtype)
    rows_g_ref[...] = u_ref[6 + t:8 + t, :FFN_TILE]
    rows_v_ref[...] = u_ref[6 + t:8 + t, FFN_TILE:]


def _ffn_up_sequence(h, w_up, conv_w, conv_b):
    bx, t, d = h.shape
    nt = N_FFN_TILES
    kern = functools.partial(_ffn_up_seq_kernel, t=t, chunk=min(t, 512))
    conv_b = conv_b.reshape(1, 2 * FFN_DIM)
    g, rows_g, rows_v = pl.pallas_call(
        kern,
        grid=(bx, nt),
        in_specs=[pl.BlockSpec((None, t, d), lambda b, j: (b, 0, 0)),
                  pl.BlockSpec((d, FFN_TILE), lambda b, j: (0, j)),
                  pl.BlockSpec((d, FFN_TILE), lambda b, j: (0, nt + j)),
                  pl.BlockSpec((3, FFN_TILE), lambda b, j: (0, j)),
                  pl.BlockSpec((3, FFN_TILE), lambda b, j: (0, nt + j)),
                  pl.BlockSpec((1, FFN_TILE), lambda b, j: (0, j)),
                  pl.BlockSpec((1, FFN_TILE), lambda b, j: (0, nt + j))],
        out_specs=[pl.BlockSpec((None, t, FFN_TILE), lambda b, j: (b, 0, j)),
                   pl.BlockSpec((None, 2, FFN_TILE), lambda b, j: (b, 0, j)),
                   pl.BlockSpec((None, 2, FFN_TILE), lambda b, j: (b, 0, j))],
        out_shape=[jax.ShapeDtypeStruct((bx, t, FFN_DIM), MXU_DTYPE),
                   jax.ShapeDtypeStruct((bx, 2, FFN_DIM), F32),
                   jax.ShapeDtypeStruct((bx, 2, FFN_DIM), F32)],
        scratch_shapes=[pltpu.VMEM((d, 2 * FFN_TILE), MXU_DTYPE), pltpu.VMEM((t + 8, 2 * FFN_TILE), F32)],
        compiler_params=_params("arbitrary", "arbitrary"),
        name="ffn_up_sequence",
    )(h, w_up, w_up, conv_w, conv_w, conv_b, conv_b)
    return g, jnp.concatenate([rows_g, rows_v], axis=-1)


def _ffn_up_step_kernel(a_ref, wg_ref, wv_ref, cwg_ref, cwv_ref, cbg_ref, cbv_ref, p0g_ref, p0v_ref,
                        p1g_ref, p1v_ref, g_ref, ug_ref, uv_ref):
    w = jnp.concatenate([wg_ref[...], wv_ref[...]], axis=-1).astype(MXU_DTYPE)
    u = jnp.dot(a_ref[...], w, preferred_element_type=F32)
    cw = jnp.concatenate([cwg_ref[...], cwv_ref[...]], axis=-1)
    cb = jnp.concatenate([cbg_ref[...], cbv_ref[...]], axis=-1)
    p0 = jnp.concatenate([p0g_ref[...], p0v_ref[...]], axis=-1)
    p1 = jnp.concatenate([p1g_ref[...], p1v_ref[...]], axis=-1)
    z = cb + cw[0:1] * p0
    z = z + cw[1:2] * p1
    z = z + cw[2:3] * u
    g_ref[...] = (_silu(z[:, :FFN_TILE]) * z[:, FFN_TILE:]).astype(g_ref.dtype)
    ug_ref[...] = u[:, :FFN_TILE]
    uv_ref[...] = u[:, FFN_TILE:]


def _ffn_up_step(h, w_up, conv_w, conv_b, prev0, prev1):
    b, d = h.shape
    nt = N_FFN_TILES
    conv_b = conv_b.reshape(1, 2 * FFN_DIM)
    lo = lambda j: (0, j)
    hi = lambda j: (0, nt + j)
    g, ug, uv = pl.pallas_call(
        _ffn_up_step_kernel,
        grid=(nt,),
        in_specs=[pl.BlockSpec((b, d), lambda j: (0, 0)),
                  pl.BlockSpec((d, FFN_TILE), lo), pl.BlockSpec((d, FFN_TILE), hi),
                  pl.BlockSpec((3, FFN_TILE), lo), pl.BlockSpec((3, FFN_TILE), hi),
                  pl.BlockSpec((1, FFN_TILE), lo), pl.BlockSpec((1, FFN_TILE), hi),
                  pl.BlockSpec((b, FFN_TILE), lo), pl.BlockSpec((b, FFN_TILE), hi),
                  pl.BlockSpec((b, FFN_TILE), lo), pl.BlockSpec((b, FFN_TILE), hi)],
        out_specs=[pl.BlockSpec((b, FFN_TILE), lo)] * 3,
        out_shape=[jax.ShapeDtypeStruct((b, FFN_DIM), MXU_DTYPE),
                   jax.ShapeDtypeStruct((b, FFN_DIM), F32),
                   jax.ShapeDtypeStruct((b, FFN_DIM), F32)],
        compiler_params=_params("arbitrary"),
        name="ffn_up_step",
    )(h, w_up, w_up, conv_w, conv_w, conv_b, conv_b, prev0, prev0, prev1, prev1)
    return g, jnp.concatenate([ug, uv], axis=-1)


def _qk_gains(q_gain, k_gain, g):
    return jnp.stack([q_gain[g], k_gain[g], jnp.ones_like(q_gain[g])])[:, None, :]


def _sequence_trunk(x, mods, w, tm):
    bx, t, d = x.shape
    new_kv, new_ret, new_conv = [[] for _ in range(N_GROUPS)], [], []
    for layer in range(DEPTH):
        m = mods[layer]
        h = _norm_mod(x, w["norm_mix"][layer], m, 0, 1, tm)
        if layer % N_MIXERS == 0:
            a = layer // N_MIXERS
            outs, lses = [], []
            for g in range(N_GROUPS):
                dil = DILATIONS[g]
                qkv = _qkv_project(h, w["attn_w_qkv"][a], _qk_gains(w["attn_q_gain"][a], w["attn_k_gain"][a], g),
                                   g, dil, tm)
                o, lse = _window_attention(qkv, _band_bias(w["rel_bias"], g), dil)
                outs.append(o)
                lses.append(lse)
                keep = min(WINDOWS[g], t)
                rows = keep // dil
                kv = qkv[:, 1:3, :, :, t // dil - rows:, :]
                kv = jnp.transpose(kv, (0, 4, 2, 1, 3, 5))
                new_kv[g].append(kv.reshape(bx, keep, 2, HEADS, HEAD_DIM))
            x = _attn_out(outs, lses, w["attn_w_o"][a], x, m, 2, min(tm, 256))
        else:
            r = layer // N_MIXERS
            proj = _matmul(h, w["ret_w_in"][r], tm, 1024)
            cos_rep, sin_signed = _rotation_tables(jnp.arange(t, dtype=jnp.int32))
            y, state = _retention_sequence(proj, cos_rep, sin_signed, w["ret_gn_gain"][r])
            new_ret.append(state)
            x = _resid_matmul(y, w["ret_w_o"][r], x, m, 2, tm, 512)
        h = _norm_mod(x, w["norm_ffn"][layer], m, 3, 4, tm)
        g_act, rows = _ffn_up_sequence(h, w["ffn_w_up"][layer], w["ffn_conv_w"][layer], w["ffn_conv_b"][layer])
        new_conv.append(rows)
        x = _resid_matmul(g_act, w["ffn_w_down"][layer], x, m, 5, tm, 512)
    return x, [jnp.stack(kv) for kv in new_kv], jnp.stack(new_ret), jnp.stack(new_conv)


def _step_trunk(x, mods, caches, ret_state, conv_state, position, w):
    b, d = x.shape
    x = x[None]
    new_kv, new_ret, new_conv = [[] for _ in range(N_GROUPS)], [], []
    for layer in range(DEPTH):
        m = mods[layer]
        h = _norm_mod(x, w["norm_mix"][layer], m, 0, 1, b)
        if layer % N_MIXERS == 0:
            a = layer // N_MIXERS
            outs, lses = [], []
            for g in range(N_GROUPS):
                qkv = _qkv_project(h, w["attn_w_qkv"][a], _qk_gains(w["attn_q_gain"][a], w["attn_k_gain"][a], g),
                                   g, 1, b)
                qkv = jnp.transpose(qkv.reshape(3, HEADS, b, HEAD_DIM), (0, 2, 1, 3))
                o, lse = _step_attention(qkv[0], qkv[1], qkv[2], caches[g], a, _group_bias(w["rel_bias"], g),
                                         DILATIONS[g], b)
                outs.append(jnp.transpose(o, (1, 0, 2))[None])
                lses.append(jnp.transpose(lse, (1, 0, 2))[None])
                new_kv[g].append(_cache_shift(caches[g], a, qkv[1], qkv[2]))
            x = _attn_out(outs, lses, w["attn_w_o"][a], x, m, 2, b)
        else:
            r = layer // N_MIXERS
            proj = _matmul(h, w["ret_w_in"][r], b, 1024)[0]
            q = proj[:, :RET_QK_WIDTH].reshape(b, RET_HEADS, RET_DK)
            k = proj[:, RET_QK_WIDTH:2 * RET_QK_WIDTH].reshape(b, RET_HEADS, RET_DK)
            v = proj[:, 2 * RET_QK_WIDTH:2 * RET_QK_WIDTH + RET_V_WIDTH].reshape(b, RET_HEADS, RET_DV)
            gate = proj[:, 2 * RET_QK_WIDTH + RET_V_WIDTH:].reshape(b, RET_HEADS, RET_DV)
            cos_rep, sin_signed = _rotation_tables(position)
            y, state = _retention_step(q, k, v, gate, ret_state, r, cos_rep, sin_signed, w["ret_gn_gain"][r])
            new_ret.append(state)
            x = _resid_matmul(y.reshape(1, b, RET_V_WIDTH), w["ret_w_o"][r], x, m, 2, b, 512)
        h = _norm_mod(x, w["norm_ffn"][layer], m, 3, 4, b)
        g_act, u_new = _ffn_up_step(h[0], w["ffn_w_up"][layer], w["ffn_conv_w"][layer], w["ffn_conv_b"][layer],
                                    conv_state[layer, :, 0], conv_state[layer, :, 1])
        new_conv.append(jnp.stack([conv_state[layer, :, 1], u_new], axis=1))
        x = _resid_matmul(g_act[None], w["ffn_w_down"][layer], x, m, 5, b, 512)
    return x[0], [jnp.stack(kv) for kv in new_kv], jnp.stack(new_ret), jnp.stack(new_conv)


def kernel(x_prompt, x_sample, cache_attn_kv_w128, cache_attn_kv_w512, cache_attn_kv_w2048, state_ret,
           state_conv, c_prompt, c_sample, rel_bias, w_ada, b_ada, norm_mix, norm_ffn, attn_w_qkv,
           attn_q_gain, attn_k_gain, attn_w_o, ret_w_in, ret_gn_gain, ret_w_o, ffn_w_up, ffn_conv_w,
           ffn_conv_b, ffn_w_down):
    w = dict(rel_bias=rel_bias, norm_mix=norm_mix, norm_ffn=norm_ffn, attn_w_qkv=attn_w_qkv,
             attn_q_gain=attn_q_gain, attn_k_gain=attn_k_gain, attn_w_o=attn_w_o, ret_w_in=ret_w_in,
             ret_gn_gain=ret_gn_gain, ret_w_o=ret_w_o, ffn_w_up=ffn_w_up, ffn_conv_w=ffn_conv_w,
             ffn_conv_b=ffn_conv_b, ffn_w_down=ffn_w_down)
    bp, sp, d = x_prompt.shape
    bs = x_sample.shape[0]
    past_len = cache_attn_kv_w2048.shape[2]

    rows = bp + bs
    pad = (-rows) % 8
    c_all = jnp.concatenate([c_prompt, c_sample, jnp.zeros((pad, d), F32)], axis=0)
    mods = _ada_modulation(c_all, w_ada, b_ada)
    mods_p = mods[:, :bp].reshape(DEPTH, bp, 1, 6 * d)
    mods_s = mods[:, bp:rows].reshape(DEPTH, 1, bs, 6 * d)

    y_p, kv_p, ret_p, conv_p = _sequence_trunk(x_prompt, mods_p, w, 512)
    caches = (cache_attn_kv_w128, cache_attn_kv_w512, cache_attn_kv_w2048)
    position = past_len + jnp.arange(1, dtype=jnp.int32)
    y_s, kv_s, ret_s, conv_s = _step_trunk(x_sample[:, 0], mods_s, caches, state_ret, state_conv, position, w)

    return (y_p, y_s[:, None, :], kv_p[0], kv_p[1], kv_p[2], ret_p, conv_p,
            kv_s[0], kv_s[1], kv_s[2], ret_s, conv_s)
```

```python
import functools
import math

import jax
import jax.numpy as jnp
from jax import lax
from jax.experimental import pallas as pl
from jax.experimental.pallas import tpu as pltpu

D_MODEL = 2048
DEPTH = 2
N_MIXERS = 2

WINDOWS = (128, 512, 2048)
DILATIONS = (1, 4, 16)
N_GROUPS = 3
HEAD_DIM = 128
HEADS = 8
ATTN_WIDTH = HEADS * HEAD_DIM
N_KEYS = 129
KEY_BLOCK = 128
N_BUCKETS = 32
MAX_DISTANCE = 2048
NEG_INF = -1e30

RET_HEADS = 8
RET_DK = 256
RET_DV = 512
RET_QK_WIDTH = RET_HEADS * RET_DK
RET_V_WIDTH = RET_HEADS * RET_DV
RET_CHUNK = 128
RET_HEAD_GROUP = 2
ROT_BASE = 10000.0

FFN_DIM = 5504
FFN_TILE = 128
N_FFN_TILES = FFN_DIM // FFN_TILE
NORM_EPS = 1e-6

F32 = jnp.float32
MXU_DTYPE = jnp.bfloat16
VMEM_LIMIT_BYTES = 56 * 1024 * 1024


def _params(*semantics):
    return pltpu.CompilerParams(dimension_semantics=semantics, vmem_limit_bytes=VMEM_LIMIT_BYTES)


def _silu(x):
    return x * jax.nn.sigmoid(x)


def _ada_kernel(c_ref, w_ref, b_ref, o_ref):
    a = _silu(c_ref[...]).astype(MXU_DTYPE)
    w = w_ref[...].astype(MXU_DTYPE)
    o_ref[...] = jnp.dot(a, w, preferred_element_type=F32) + b_ref[...]


def _ada_modulation(c, w_ada, b_ada):
    rows = c.shape[0]
    tn = 1024
    return pl.pallas_call(
        _ada_kernel,
        grid=(DEPTH, 6 * D_MODEL // tn),
        in_specs=[pl.BlockSpec((rows, D_MODEL), lambda l, j: (0, 0)),
                  pl.BlockSpec((None, D_MODEL, tn), lambda l, j: (l, 0, j)),
                  pl.BlockSpec((None, 1, tn), lambda l, j: (l, 0, j))],
        out_specs=pl.BlockSpec((None, rows, tn), lambda l, j: (l, 0, j)),
        out_shape=jax.ShapeDtypeStruct((DEPTH, rows, 6 * D_MODEL), F32),
        compiler_params=_params("arbitrary", "arbitrary"),
        name="ada_modulation",
    )(c, w_ada, b_ada.reshape(DEPTH, 1, 6 * D_MODEL))


def _norm_mod_kernel(x_ref, g_ref, shift_ref, scale_ref, o_ref):
    x = x_ref[...]
    y = x * lax.rsqrt(jnp.mean(x * x, axis=-1, keepdims=True) + NORM_EPS) * g_ref[...]
    o_ref[...] = (y * (1.0 + scale_ref[...]) + shift_ref[...]).astype(o_ref.dtype)


def _mod_spec(mods, tm, col, n_lead):
    per_row = mods.shape[1] > 1
    rb = tm if per_row else 1

    def index(*ids):
        b, i = ids[n_lead], ids[n_lead + 1]
        return (b, i if per_row else 0, col)

    return pl.BlockSpec((None, rb, D_MODEL), index)


def _norm_mod(x, gain, mods, shift_col, scale_col, tm):
    bx, t, d = x.shape
    return pl.pallas_call(
        _norm_mod_kernel,
        grid=(bx, t // tm),
        in_specs=[pl.BlockSpec((None, tm, d), lambda b, i: (b, i, 0)),
                  pl.BlockSpec((1, d), lambda b, i: (0, 0)),
                  _mod_spec(mods, tm, shift_col, 0),
                  _mod_spec(mods, tm, scale_col, 0)],
        out_specs=pl.BlockSpec((None, tm, d), lambda b, i: (b, i, 0)),
        out_shape=jax.ShapeDtypeStruct((bx, t, d), MXU_DTYPE),
        compiler_params=_params("arbitrary", "arbitrary"),
        name="norm_mod",
    )(x, gain.reshape(1, d), mods, mods)


def _cast_weight_once(w_ref, wbf_ref):
    @pl.when((pl.program_id(1) == 0) & (pl.program_id(2) == 0))
    def _():
        wbf_ref[...] = w_ref[...].astype(wbf_ref.dtype)


def _matmul_kernel(a_ref, w_ref, o_ref, wbf_ref):
    _cast_weight_once(w_ref, wbf_ref)
    o_ref[...] = jnp.dot(a_ref[...], wbf_ref[...], preferred_element_type=F32).astype(o_ref.dtype)


def _matmul(a, w, tm, tn, out_dtype=F32):
    bx, t, k = a.shape
    n = w.shape[1]
    return pl.pallas_call(
        _matmul_kernel,
        grid=(n // tn, bx, t // tm),
        in_specs=[pl.BlockSpec((None, tm, k), lambda j, b, i: (b, i, 0)),
                  pl.BlockSpec((k, tn), lambda j, b, i: (0, j))],
        out_specs=pl.BlockSpec((None, tm, tn), lambda j, b, i: (b, i, j)),
        out_shape=jax.ShapeDtypeStruct((bx, t, n), out_dtype),
        scratch_shapes=[pltpu.VMEM((k, tn), MXU_DTYPE)],
        compiler_params=_params("arbitrary", "arbitrary", "arbitrary"),
        name="matmul",
    )(a, w)


def _resid_matmul_kernel(a_ref, w_ref, x_ref, gate_ref, o_ref, wbf_ref):
    _cast_weight_once(w_ref, wbf_ref)
    acc = jnp.dot(a_ref[...], wbf_ref[...], preferred_element_type=F32)
    o_ref[...] = x_ref[...] + gate_ref[...] * acc


def _resid_matmul(a, w, x, mods, gate_col, tm, tn):
    bx, t, k = a.shape
    n = w.shape[1]
    cols_per_group = D_MODEL // tn
    per_row = mods.shape[1] > 1
    rb = tm if per_row else 1
    gate_spec = pl.BlockSpec(
        (None, rb, tn), lambda j, b, i: (b, i if per_row else 0, gate_col * cols_per_group + j))
    return pl.pallas_call(
        _resid_matmul_kernel,
        grid=(n // tn, bx, t // tm),
        in_specs=[pl.BlockSpec((None, tm, k), lambda j, b, i: (b, i, 0)),
                  pl.BlockSpec((k, tn), lambda j, b, i: (0, j)),
                  pl.BlockSpec((None, tm, tn), lambda j, b, i: (b, i, j)),
                  gate_spec],
        out_specs=pl.BlockSpec((None, tm, tn), lambda j, b, i: (b, i, j)),
        out_shape=jax.ShapeDtypeStruct((bx, t, n), F32),
        scratch_shapes=[pltpu.VMEM((k, tn), MXU_DTYPE)],
        compiler_params=_params("arbitrary", "arbitrary", "arbitrary"),
        name="resid_matmul",
    )(a, w, x, mods)


def _qkv_kernel(a_ref, w_ref, gain_ref, o_ref, wbf_ref, y_ref, *, dilation, tm):
    _cast_weight_once(w_ref, wbf_ref)
    acc = jnp.dot(a_ref[...], wbf_ref[...], preferred_element_type=F32)
    part = pl.program_id(0)
    gain = gain_ref[...]

    @pl.when(part < 2)
    def _():
        for h in range(HEADS):
            xh = acc[:, h * HEAD_DIM:(h + 1) * HEAD_DIM]
            y_ref[h] = xh * lax.rsqrt(jnp.mean(xh * xh, axis=-1, keepdims=True) + NORM_EPS) * gain

    @pl.when(part == 2)
    def _():
        for h in range(HEADS):
            y_ref[h] = acc[:, h * HEAD_DIM:(h + 1) * HEAD_DIM]

    rows = tm // dilation
    for c in range(dilation):
        for h in range(HEADS):
            if dilation == 1:
                o_ref[c, h] = y_ref[h]
            else:
                o_ref[c, h] = y_ref[h, pl.ds(c, rows, stride=dilation), :]


def _qkv_project(h, w_qkv, gains, group, dilation, tm):
    bx, t, d = h.shape
    td = t // dilation
    kern = functools.partial(_qkv_kernel, dilation=dilation, tm=tm)
    return pl.pallas_call(
        kern,
        grid=(3, bx, t // tm),
        in_specs=[pl.BlockSpec((None, tm, d), lambda p, b, i: (b, i, 0)),
                  pl.BlockSpec((d, ATTN_WIDTH), lambda p, b, i: (0, group * 3 + p)),
                  pl.BlockSpec((None, 1, HEAD_DIM), lambda p, b, i: (p, 0, 0))],
        out_specs=pl.BlockSpec((None, None, dilation, HEADS, tm // dilation, HEAD_DIM),
                               lambda p, b, i: (b, p, 0, 0, i, 0)),
        out_shape=jax.ShapeDtypeStruct((bx, 3, dilation, HEADS, td, HEAD_DIM), F32),
        scratch_shapes=[pltpu.VMEM((d, ATTN_WIDTH), MXU_DTYPE), pltpu.VMEM((HEADS, tm, HEAD_DIM), F32)],
        compiler_params=_params("arbitrary", "arbitrary", "arbitrary"),
        name=f"qkv_project_g{group}",
    )(h, w_qkv, gains)


def _t5_causal_bucket(dist):
    max_exact = N_BUCKETS // 2
    d = jnp.maximum(dist, 1).astype(F32)
    large = max_exact + (jnp.log(d / max_exact) / math.log(MAX_DISTANCE / max_exact)
                         * (N_BUCKETS - max_exact)).astype(jnp.int32)
    return jnp.where(dist < max_exact, dist, jnp.minimum(large, N_BUCKETS - 1))


def _bucket_lookup(rel_bias, g, bucket, fill):
    cols = rel_bias[:, g * HEADS:(g + 1) * HEADS].astype(F32)
    expand = (slice(None),) + (None,) * bucket.ndim
    out = jnp.full((HEADS,) + bucket.shape, fill, F32)
    for b in range(N_BUCKETS):
        out = jnp.where(bucket[None] == b, cols[b][expand], out)
    return out


def _group_bias(rel_bias, g):
    dist = DILATIONS[g] * jnp.arange(N_KEYS, dtype=jnp.int32)
    return _bucket_lookup(rel_bias, g, _t5_causal_bucket(dist), 0.0)


def _band_bias(rel_bias, g):
    a = jnp.arange(KEY_BLOCK, dtype=jnp.int32)[:, None]
    c = jnp.arange(2 * KEY_BLOCK, dtype=jnp.int32)[None, :]
    rel = a + KEY_BLOCK - c
    valid = (rel >= 0) & (rel <= KEY_BLOCK)
    bucket = jnp.where(valid, _t5_causal_bucket(DILATIONS[g] * jnp.clip(rel, 0, KEY_BLOCK)), -1)
    return _bucket_lookup(rel_bias, g, bucket, NEG_INF)


def _window_attn_kernel(q_ref, kp_ref, kc_ref, vp_ref, vc_ref, bias_ref, o_ref, lse_ref,
                        *, dilation, heads):
    i = pl.program_id(1)
    c = pl.program_id(3)
    col = lax.broadcasted_iota(jnp.int32, (KEY_BLOCK, 2 * KEY_BLOCK), 1)
    no_prev = (i == 0) & (col < KEY_BLOCK)
    for h in range(heads):
        q = q_ref[h].astype(MXU_DTYPE)
        k = jnp.concatenate([kp_ref[h], kc_ref[h]], axis=0).astype(MXU_DTYPE)
        v = jnp.concatenate([vp_ref[h], vc_ref[h]], axis=0).astype(MXU_DTYPE)
        s = lax.dot_general(q, k, (((1,), (1,)), ((), ())), preferred_element_type=F32)
        s = s * (HEAD_DIM ** -0.5) + bias_ref[h]
        s = jnp.where(no_prev, NEG_INF, s)
        m = jnp.max(s, axis=-1, keepdims=True)
        p = jnp.exp(s - m)
        l = jnp.sum(p, axis=-1, keepdims=True)
        o = jnp.dot(p.astype(MXU_DTYPE), v, preferred_element_type=F32) / l
        lse = jnp.broadcast_to(m + jnp.log(l), (KEY_BLOCK, HEAD_DIM))
        if dilation == 1:
            o_ref[h] = o
            lse_ref[h] = lse
        else:
            o_ref[h, pl.ds(c, KEY_BLOCK, stride=dilation), :] = o
            lse_ref[h, pl.ds(c, KEY_BLOCK, stride=dilation), :] = lse


def _window_attention(qkv, band_bias, dilation):
    bx, _, _, _, td, _ = qkv.shape
    t = td * dilation
    nblk = td // KEY_BLOCK
    hb = 1
    heads = HEADS // hb

    def spec(part, prev):
        def index(b, i, hq, c):
            return (b, part, c, hq, jnp.maximum(i - 1, 0) if prev else i, 0)
        return pl.BlockSpec((None, None, None, heads, KEY_BLOCK, HEAD_DIM), index)

    out_spec = pl.BlockSpec((None, heads, KEY_BLOCK * dilation, HEAD_DIM), lambda b, i, hq, c: (b, hq, i, 0))
    kern = functools.partial(_window_attn_kernel, dilation=dilation, heads=heads)
    return pl.pallas_call(
        kern,
        grid=(bx, nblk, hb, dilation),
        in_specs=[spec(0, False), spec(1, True), spec(1, False), spec(2, True), spec(2, False),
                  pl.BlockSpec((heads, KEY_BLOCK, 2 * KEY_BLOCK), lambda b, i, hq, c: (hq, 0, 0))],
        out_specs=[out_spec, out_spec],
        out_shape=[jax.ShapeDtypeStruct((bx, HEADS, t, HEAD_DIM), F32)] * 2,
        compiler_params=_params("arbitrary", "arbitrary", "arbitrary", "arbitrary"),
        name=f"window_attention_d{dilation}",
    )(qkv, qkv, qkv, qkv, qkv, band_bias)


STEP_ROWS = 8
STEP_SEQS = 8


def _step_attn_kernel(*refs):
    q_ref, kn_ref, vn_ref = refs[:3]
    kv_refs = refs[3:3 + STEP_ROWS]
    bias0_ref, bias_ref, o_ref, lse_ref, m_ref, l_ref, acc_ref = refs[3 + STEP_ROWS:]
    j = pl.program_id(1)
    scale = HEAD_DIM ** -0.5

    def chunk(ci, carry):
        sl = pl.ds(pl.multiple_of(ci * STEP_SEQS, STEP_SEQS), STEP_SEQS)
        q = q_ref[sl]

        @pl.when(j == 0)
        def _():
            s0 = jnp.sum(q * kn_ref[sl], axis=-1, keepdims=True) * scale + bias0_ref[...]
            m_ref[sl] = jnp.broadcast_to(s0, q.shape)
            l_ref[sl] = jnp.ones(q.shape, F32)
            acc_ref[sl] = vn_ref[sl]

        scores = [jnp.sum(q * kv_refs[r][sl, 0], axis=-1, keepdims=True) * scale + bias_ref[r]
                  for r in range(STEP_ROWS)]
        m_old = m_ref[sl]
        m_new = m_old
        for s in scores:
            m_new = jnp.maximum(m_new, s)
        alpha = jnp.exp(m_old - m_new)
        l_new = alpha * l_ref[sl]
        acc = alpha * acc_ref[sl]
        for r in range(STEP_ROWS):
            p = jnp.exp(scores[r] - m_new)
            l_new = l_new + p
            acc = acc + p * kv_refs[r][sl, 1]
        m_ref[sl] = m_new
        l_ref[sl] = l_new
        acc_ref[sl] = acc
        return carry

    n_chunks = q_ref.shape[0] // STEP_SEQS
    lax.fori_loop(0, n_chunks, chunk, 0, unroll=2 if n_chunks % 2 == 0 else 1)

    @pl.when(j == pl.num_programs(1) - 1)
    def _():
        o_ref[...] = acc_ref[...] / l_ref[...]
        lse_ref[...] = m_ref[...] + jnp.log(l_ref[...])


def _step_attention(q, k_new, v_new, cache, layer, gb, dilation, bt):
    b = q.shape[0]
    nk = N_KEYS - 1
    lanes = jnp.broadcast_to(gb.T[:, :, None], (N_KEYS, HEADS, HEAD_DIM))
    bias0 = lanes[0]
    bias_rows = lanes[:0:-1]
    qspec = pl.BlockSpec((bt, HEADS, HEAD_DIM), lambda bi, j: (bi, 0, 0))

    def row_spec(r):
        return pl.BlockSpec((None, bt, None, 2, HEADS, HEAD_DIM),
                            lambda bi, j: (layer, bi, (j * STEP_ROWS + r) * dilation, 0, 0, 0))

    return pl.pallas_call(
        _step_attn_kernel,
        grid=(b // bt, nk // STEP_ROWS),
        in_specs=[qspec, qspec, qspec] + [row_spec(r) for r in range(STEP_ROWS)] + [
            pl.BlockSpec((HEADS, HEAD_DIM), lambda bi, j: (0, 0)),
            pl.BlockSpec((STEP_ROWS, HEADS, HEAD_DIM), lambda bi, j: (j, 0, 0))],
        out_specs=[qspec, qspec],
        out_shape=[jax.ShapeDtypeStruct((b, HEADS, HEAD_DIM), F32)] * 2,
        scratch_shapes=[pltpu.VMEM((bt, HEADS, HEAD_DIM), F32)] * 3,
        compiler_params=_params("arbitrary", "arbitrary"),
        name=f"step_attention_d{dilation}",
    )(q, k_new, v_new, *([cache] * STEP_ROWS), bias0, bias_rows)


def _attn_out_kernel(o0_ref, o1_ref, o2_ref, l0_ref, l1_ref, l2_ref, w_ref, x_ref, gate_ref, out_ref,
                     wbf_ref):
    _cast_weight_once(w_ref, wbf_ref)
    heads = []
    for h in range(HEADS):
        l0, l1, l2 = l0_ref[h], l1_ref[h], l2_ref[h]
        m = jnp.maximum(jnp.maximum(l0, l1), l2)
        e0, e1, e2 = jnp.exp(l0 - m), jnp.exp(l1 - m), jnp.exp(l2 - m)
        tot = e0 + e1 + e2
        merged = (e0 / tot) * o0_ref[h] + (e1 / tot) * o1_ref[h] + (e2 / tot) * o2_ref[h]
        heads.append(merged.astype(MXU_DTYPE))
    acc = jnp.dot(jnp.concatenate(heads, axis=-1), wbf_ref[...], preferred_element_type=F32)
    out_ref[...] = x_ref[...] + gate_ref[...] * acc


def _attn_out(outs, lses, w_o, x, mods, gate_col, tm):
    bx, t, d = x.shape
    per_row = mods.shape[1] > 1
    rb = tm if per_row else 1
    aspec = pl.BlockSpec((None, HEADS, tm, HEAD_DIM), lambda j, b, i: (b, 0, i, 0))
    xspec = pl.BlockSpec((None, tm, d), lambda j, b, i: (b, i, 0))
    return pl.pallas_call(
        _attn_out_kernel,
        grid=(1, bx, t // tm),
        in_specs=[aspec] * 6 + [
            pl.BlockSpec((ATTN_WIDTH, d), lambda j, b, i: (0, 0)),
            xspec,
            pl.BlockSpec((None, rb, d), lambda j, b, i: (b, i if per_row else 0, gate_col))],
        out_specs=xspec,
        out_shape=jax.ShapeDtypeStruct((bx, t, d), F32),
        scratch_shapes=[pltpu.VMEM((ATTN_WIDTH, d), MXU_DTYPE)],
        compiler_params=_params("arbitrary", "arbitrary", "arbitrary"),
        name="attn_out",
    )(*outs, *lses, w_o, x, mods)


KV_ROWS = 2 * HEADS
SHIFT_BUFFERS = 3
SHIFT_CHUNK_BYTES = 8 * 1024 * 1024


def _shift_chunking(b, keep_rows):
    row_bytes = HEAD_DIM * 4
    pieces = 1
    while (keep_rows // pieces) * row_bytes > SHIFT_CHUNK_BYTES or keep_rows % (8 * pieces):
        pieces += 1
    rows = keep_rows // pieces
    seqs = 1
    while seqs * 2 * rows * row_bytes <= SHIFT_CHUNK_BYTES // 2 and b % (seqs * 2) == 0:
        seqs *= 2
    return seqs, rows, pieces


def _cache_shift_kernel(prev_ref, new_ref, out_ref, buf_ref, in_sems, out_sems, new_sem,
                        *, layer, seqs, rows, pieces):
    b, total_rows, _ = out_ref.shape
    n_chunks = (b // seqs) * pieces

    def load(k, slot):
        block, piece = k // pieces, k % pieces
        src = prev_ref.at[pl.ds(layer * b + block * seqs, seqs), pl.ds(KV_ROWS + piece * rows, rows)]
        return pltpu.make_async_copy(src, buf_ref.at[slot], in_sems.at[slot])

    def store(k, slot):
        block, piece = k // pieces, k % pieces
        dst = out_ref.at[pl.ds(block * seqs, seqs), pl.ds(piece * rows, rows)]
        return pltpu.make_async_copy(buf_ref.at[slot], dst, out_sems.at[slot])

    new_copy = pltpu.make_async_copy(new_ref, out_ref.at[:, pl.ds(total_rows - KV_ROWS, KV_ROWS)], new_sem)
    new_copy.start()
    load(0, 0).start()

    def step(k, carry):
        slot = k % SHIFT_BUFFERS
        nxt = k + 1
        nxt_slot = nxt % SHIFT_BUFFERS

        @pl.when(nxt < n_chunks)
        def _():
            @pl.when(nxt >= SHIFT_BUFFERS)
            def _():
                store(nxt - SHIFT_BUFFERS, nxt_slot).wait()
            load(nxt, nxt_slot).start()

        load(k, slot).wait()
        store(k, slot).start()
        return carry

    lax.fori_loop(0, n_chunks, step, 0)
    for k in range(max(0, n_chunks - SHIFT_BUFFERS), n_chunks):
        store(k, k % SHIFT_BUFFERS).wait()
    new_copy.wait()


def _cache_shift(prev, layer, k_new, v_new):
    nl, b, w = prev.shape[:3]
    prev_rows = prev.reshape(nl * b, w * KV_ROWS, HEAD_DIM)
    new_rows = jnp.concatenate([k_new, v_new], axis=1)
    seqs, rows, pieces = _shift_chunking(b, (w - 1) * KV_ROWS)
    out = pl.pallas_call(
        functools.partial(_cache_shift_kernel, layer=layer, seqs=seqs, rows=rows, pieces=pieces),
        in_specs=[pl.BlockSpec(memory_space=pl.ANY),
                  pl.BlockSpec(memory_space=pltpu.VMEM)],
        out_specs=pl.BlockSpec(memory_space=pl.ANY),
        out_shape=jax.ShapeDtypeStruct((b, w * KV_ROWS, HEAD_DIM), prev.dtype),
        scratch_shapes=[pltpu.VMEM((SHIFT_BUFFERS, seqs, rows, HEAD_DIM), prev.dtype),
                        pltpu.SemaphoreType.DMA((SHIFT_BUFFERS,)),
                        pltpu.SemaphoreType.DMA((SHIFT_BUFFERS,)),
                        pltpu.SemaphoreType.DMA(())],
        compiler_params=pltpu.CompilerParams(vmem_limit_bytes=VMEM_LIMIT_BYTES),
        name="cache_shift",
    )(prev_rows, new_rows)
    return out.reshape(b, w, 2, HEADS, HEAD_DIM)


def _rotation_tables(positions):
    half = RET_DK // 2
    inv_freq = 1.0 / (ROT_BASE ** jnp.linspace(0.0, 1.0, half, dtype=F32))
    ang = positions.astype(F32)[:, None] * inv_freq[None, :]
    cos, sin = jnp.cos(ang), jnp.sin(ang)
    cos_rep = jnp.stack([cos, cos], axis=-1).reshape(-1, RET_DK)
    sin_signed = jnp.stack([-sin, sin], axis=-1).reshape(-1, RET_DK)
    return cos_rep, sin_signed


def _rotate_pairs(x, cos_rep, sin_signed):
    lanes = 128
    even = lax.broadcasted_iota(jnp.int32, (x.shape[0], lanes), 1) % 2 == 0
    parts = []
    for s in range(x.shape[1] // lanes):
        xs = x[:, s * lanes:(s + 1) * lanes]
        parts.append(jnp.where(even, pltpu.roll(xs, lanes - 1, 1), pltpu.roll(xs, 1, 1)))
    swapped = jnp.concatenate(parts, axis=-1)
    return x * cos_rep + swapped * sin_signed


def _log_gamma():
    return jnp.log1p(-jnp.exp2(-5.0 - jnp.arange(RET_HEADS, dtype=F32)))


def _group_norm_gate(o, gain, gate):
    mu = jnp.mean(o, axis=-1, keepdims=True)
    var = jnp.mean(jnp.square(o - mu), axis=-1, keepdims=True)
    return _silu(gate) * ((o - mu) * lax.rsqrt(var + NORM_EPS) * gain)


def _retention_kernel(q_ref, k_ref, v_ref, gate_ref, cos_ref, sin_ref, decay_ref, qdec_ref, kdec_ref,
                      cdec_ref, gain_ref, y_ref, s_out_ref, s_ref):
    c = pl.program_id(1)
    nb, hg = s_ref.shape[:2]

    @pl.when(c == 0)
    def _():
        s_ref[...] = jnp.zeros(s_ref.shape, F32)

    cos, sin = cos_ref[...], sin_ref[...]
    for b in range(nb):
        for hh in range(hg):
            ksl = slice(hh * RET_DK, (hh + 1) * RET_DK)
            vsl = slice(hh * RET_DV, (hh + 1) * RET_DV)
            q = _rotate_pairs(q_ref[b, :, ksl], cos, sin)
            k = _rotate_pairs(k_ref[b, :, ksl], cos, sin) * (RET_DK ** -0.5)
            v = v_ref[b, :, vsl].astype(MXU_DTYPE)
            state = s_ref[b, hh]
            scores = lax.dot_general(q.astype(MXU_DTYPE), k.astype(MXU_DTYPE), (((1,), (1,)), ((), ())),
                                     preferred_element_type=F32) * decay_ref[hh]
            o = jnp.dot(scores.astype(MXU_DTYPE), v, preferred_element_type=F32)
            o = o + jnp.dot((q * qdec_ref[hh]).astype(MXU_DTYPE), state.astype(MXU_DTYPE),
                            preferred_element_type=F32)
            kd_t = jnp.transpose(k * kdec_ref[hh]).astype(MXU_DTYPE)
            s_ref[b, hh] = cdec_ref[hh] * state + jnp.dot(kd_t, v, preferred_element_type=F32)
            y_ref[b, :, vsl] = _group_norm_gate(o, gain_ref[hh], gate_ref[b, :, vsl]).astype(y_ref.dtype)

    @pl.when(c == pl.num_programs(1) - 1)
    def _():
        s_out_ref[...] = s_ref[...]


def _retention_sequence(proj, cos_rep, sin_signed, gn_gain):
    bx, t, _ = proj.shape
    cw = RET_CHUNK
    lg = _log_gamma()
    pos = jnp.arange(cw, dtype=F32)
    diff = pos[:, None] - pos[None, :]
    decay = jnp.where(diff >= 0, jnp.exp(diff[None] * lg[:, None, None]), 0.0)
    q_decay = jnp.exp((pos[:, None] + 1.0) * lg[None, :]).T[:, :, None]
    k_decay = jnp.exp((cw - 1.0 - pos)[:, None] * lg[None, :]).T[:, :, None]
    chunk_decay = jnp.exp(cw * lg).reshape(RET_HEADS, 1, 1)
    hg = RET_HEAD_GROUP
    nqk = RET_QK_WIDTH // (hg * RET_DK)
    nv0 = 2 * RET_QK_WIDTH // (hg * RET_DV)
    ng0 = nv0 + RET_HEADS // hg
    return pl.pallas_call(
        _retention_kernel,
        grid=(RET_HEADS // hg, t // cw),
        in_specs=[pl.BlockSpec((bx, cw, hg * RET_DK), lambda h, c: (0, c, h)),
                  pl.BlockSpec((bx, cw, hg * RET_DK), lambda h, c: (0, c, nqk + h)),
                  pl.BlockSpec((bx, cw, hg * RET_DV), lambda h, c: (0, c, nv0 + h)),
                  pl.BlockSpec((bx, cw, hg * RET_DV), lambda h, c: (0, c, ng0 + h)),
                  pl.BlockSpec((cw, RET_DK), lambda h, c: (c, 0)),
                  pl.BlockSpec((cw, RET_DK), lambda h, c: (c, 0)),
                  pl.BlockSpec((hg, cw, cw), lambda h, c: (h, 0, 0)),
                  pl.BlockSpec((hg, cw, 1), lambda h, c: (h, 0, 0)),
                  pl.BlockSpec((hg, cw, 1), lambda h, c: (h, 0, 0)),
                  pl.BlockSpec((hg, 1, 1), lambda h, c: (h, 0, 0)),
                  pl.BlockSpec((hg, 1, RET_DV), lambda h, c: (h, 0, 0))],
        out_specs=[pl.BlockSpec((bx, cw, hg * RET_DV), lambda h, c: (0, c, h)),
                   pl.BlockSpec((bx, hg, RET_DK, RET_DV), lambda h, c: (0, h, 0, 0))],
        out_shape=[jax.ShapeDtypeStruct((bx, t, RET_V_WIDTH), MXU_DTYPE),
                   jax.ShapeDtypeStruct((bx, RET_HEADS, RET_DK, RET_DV), F32)],
        scratch_shapes=[pltpu.VMEM((bx, hg, RET_DK, RET_DV), F32)],
        compiler_params=_params("arbitrary", "arbitrary"),
        name="retention_sequence",
    )(proj, proj, proj, proj, cos_rep, sin_signed, decay, q_decay, k_decay, chunk_decay,
      gn_gain.reshape(RET_HEADS, 1, RET_DV))


def _retention_step_kernel(q_ref, k_ref, v_ref, gate_ref, cos_ref, sin_ref, gamma_ref, gain_ref, s_ref,
                           y_ref, s_out_ref):
    cos, sin = cos_ref[...], sin_ref[...]
    q = _rotate_pairs(q_ref[...], cos, sin)
    k = _rotate_pairs(k_ref[...], cos, sin) * (RET_DK ** -0.5)
    qk = jnp.sum(q * k, axis=-1, keepdims=True)
    q_t = jnp.transpose(q * gamma_ref[...])
    k_t = jnp.transpose(k)
    v = v_ref[...]
    gamma = gamma_ref[...]
    rows = []
    for h in range(RET_HEADS):
        state = s_ref[h]
        vh = v[h:h + 1, :]
        rows.append(qk[h:h + 1, :] * vh + jnp.sum(q_t[:, h:h + 1] * state, axis=0, keepdims=True))
        s_out_ref[h] = gamma[h:h + 1, :] * state + k_t[:, h:h + 1] * vh
    o = jnp.concatenate(rows, axis=0)
    y_ref[...] = _group_norm_gate(o, gain_ref[...], gate_ref[...]).astype(y_ref.dtype)


def _retention_step(q, k, v, gate, state, layer, cos_rep, sin_signed, gn_gain):
    b = q.shape[0]
    gamma = jnp.exp(_log_gamma()).reshape(RET_HEADS, 1)
    qspec = pl.BlockSpec((None, RET_HEADS, RET_DK), lambda i: (i, 0, 0))
    vspec = pl.BlockSpec((None, RET_HEADS, RET_DV), lambda i: (i, 0, 0))
    sspec = pl.BlockSpec((None, RET_HEADS, RET_DK, RET_DV), lambda i: (i, 0, 0, 0))
    sspec_in = pl.BlockSpec((None, None, RET_HEADS, RET_DK, RET_DV), lambda i: (layer, i, 0, 0, 0))
    return pl.pallas_call(
        _retention_step_kernel,
        grid=(b,),
        in_specs=[qspec, qspec, vspec, vspec,
                  pl.BlockSpec((1, RET_DK), lambda i: (0, 0)),
                  pl.BlockSpec((1, RET_DK), lambda i: (0, 0)),
                  pl.BlockSpec((RET_HEADS, 1), lambda i: (0, 0)),
                  pl.BlockSpec((RET_HEADS, RET_DV), lambda i: (0, 0)),
                  sspec_in],
        out_specs=[vspec, sspec],
        out_shape=[jax.ShapeDtypeStruct((b, RET_HEADS, RET_DV), MXU_DTYPE),
                   jax.ShapeDtypeStruct(state.shape[1:], F32)],
        compiler_params=_params("arbitrary"),
        name="retention_step",
    )(q, k, v, gate, cos_rep, sin_signed, gamma, gn_gain.reshape(RET_HEADS, RET_DV), state)


def _ffn_up_seq_kernel(a_ref, wg_ref, wv_ref, cwg_ref, cwv_ref, cbg_ref, cbv_ref,
                       g_ref, rows_g_ref, rows_v_ref, wbf_ref, u_ref, *, t, chunk):
    wbf_ref[:, :FFN_TILE] = wg_ref[...].astype(wbf_ref.dtype)
    wbf_ref[:, FFN_TILE:] = wv_ref[...].astype(wbf_ref.dtype)
    u_ref[0:8, :] = jnp.zeros((8, 2 * FFN_TILE), F32)
    cw = jnp.concatenate([cwg_ref[...], cwv_ref[...]], axis=-1)
    cb = jnp.concatenate([cbg_ref[...], cbv_ref[...]], axis=-1)
    for s in range(t // chunk):
        r0 = s * chunk
        u = jnp.dot(a_ref[r0:r0 + chunk, :], wbf_ref[...], preferred_element_type=F32)
        u_ref[8 + r0:8 + r0 + chunk, :] = u
        z = cb + cw[0:1] * u_ref[6 + r0:6 + r0 + chunk, :]
        z = z + cw[1:2] * u_ref[7 + r0:7 + r0 + chunk, :]
        z = z + cw[2:3] * u
        g_ref[r0:r0 + chunk, :] = (_silu(z[:, :FFN_TILE]) * z[:, FFN_TILE:]).astype(g_ref.dtype)
    rows_g_ref[...] = u_ref[6 + t:8 + t, :FFN_TILE]
    rows_v_ref[...] = u_ref[6 + t:8 + t, FFN_TILE:]


def _ffn_up_sequence(h, w_up, conv_w, conv_b):
    bx, t, d = h.shape
    nt = N_FFN_TILES
    kern = functools.partial(_ffn_up_seq_kernel, t=t, chunk=min(t, 512))
    conv_b = conv_b.reshape(1, 2 * FFN_DIM)
    g, rows_g, rows_v = pl.pallas_call(
        kern,
        grid=(bx, nt),
        in_specs=[pl.BlockSpec((None, t, d), lambda b, j: (b, 0, 0)),
                  pl.BlockSpec((d, FFN_TILE), lambda b, j: (0, j)),
                  pl.BlockSpec((d, FFN_TILE), lambda b, j: (0, nt + j)),
                  pl.BlockSpec((3, FFN_TILE), lambda b, j: (0, j)),
                  pl.BlockSpec((3, FFN_TILE), lambda b, j: (0, nt + j)),
                  pl.BlockSpec((1, FFN_TILE), lambda b, j: (0, j)),
                  pl.BlockSpec((1, FFN_TILE), lambda b, j: (0, nt + j))],
        out_specs=[pl.BlockSpec((None, t, FFN_TILE), lambda b, j: (b, 0, j)),
                   pl.BlockSpec((None, 2, FFN_TILE), lambda b, j: (b, 0, j)),
                   pl.BlockSpec((None, 2, FFN_TILE), lambda b, j: (b, 0, j))],
        out_shape=[jax.ShapeDtypeStruct((bx, t, FFN_DIM), MXU_DTYPE),
                   jax.ShapeDtypeStruct((bx, 2, FFN_DIM), F32),
                   jax.ShapeDtypeStruct((bx, 2, FFN_DIM), F32)],
        scratch_shapes=[pltpu.VMEM((d, 2 * FFN_TILE), MXU_DTYPE), pltpu.VMEM((t + 8, 2 * FFN_TILE), F32)],
        compiler_params=_params("arbitrary", "arbitrary"),
        name="ffn_up_sequence",
    )(h, w_up, w_up, conv_w, conv_w, conv_b, conv_b)
    return g, jnp.concatenate([rows_g, rows_v], axis=-1)


def _ffn_up_step_kernel(a_ref, wg_ref, wv_ref, cwg_ref, cwv_ref, cbg_ref, cbv_ref, p0g_ref, p0v_ref,
                        p1g_ref, p1v_ref, g_ref, ug_ref, uv_ref):
    w = jnp.concatenate([wg_ref[...], wv_ref[...]], axis=-1).astype(MXU_DTYPE)
    u = jnp.dot(a_ref[...], w, preferred_element_type=F32)
    cw = jnp.concatenate([cwg_ref[...], cwv_ref[...]], axis=-1)
    cb = jnp.concatenate([cbg_ref[...], cbv_ref[...]], axis=-1)
    p0 = jnp.concatenate([p0g_ref[...], p0v_ref[...]], axis=-1)
    p1 = jnp.concatenate([p1g_ref[...], p1v_ref[...]], axis=-1)
    z = cb + cw[0:1] * p0
    z = z + cw[1:2] * p1
    z = z + cw[2:3] * u
    g_ref[...] = (_silu(z[:, :FFN_TILE]) * z[:, FFN_TILE:]).astype(g_ref.dtype)
    ug_ref[...] = u[:, :FFN_TILE]
    uv_ref[...] = u[:, FFN_TILE:]


def _ffn_up_step(h, w_up, conv_w, conv_b, prev0, prev1):
    b, d = h.shape
    nt = N_FFN_TILES
    conv_b = conv_b.reshape(1, 2 * FFN_DIM)
    lo = lambda j: (0, j)
    hi = lambda j: (0, nt + j)
    g, ug, uv = pl.pallas_call(
        _ffn_up_step_kernel,
        grid=(nt,),
        in_specs=[pl.BlockSpec((b, d), lambda j: (0, 0)),
                  pl.BlockSpec((d, FFN_TILE), lo), pl.BlockSpec((d, FFN_TILE), hi),
                  pl.BlockSpec((3, FFN_TILE), lo), pl.BlockSpec((3, FFN_TILE), hi),
                  pl.BlockSpec((1, FFN_TILE), lo), pl.BlockSpec((1, FFN_TILE), hi),
                  pl.BlockSpec((b, FFN_TILE), lo), pl.BlockSpec((b, FFN_TILE), hi),
                  pl.BlockSpec((b, FFN_TILE), lo), pl.BlockSpec((b, FFN_TILE), hi)],
        out_specs=[pl.BlockSpec((b, FFN_TILE), lo)] * 3,
        out_shape=[jax.ShapeDtypeStruct((b, FFN_DIM), MXU_DTYPE),
                   jax.ShapeDtypeStruct((b, FFN_DIM), F32),
                   jax.ShapeDtypeStruct((b, FFN_DIM), F32)],
        compiler_params=_params("arbitrary"),
        name="ffn_up_step",
    )(h, w_up, w_up, conv_w, conv_w, conv_b, conv_b, prev0, prev0, prev1, prev1)
    return g, jnp.concatenate([ug, uv], axis=-1)


def _qk_gains(q_gain, k_gain, g):
    return jnp.stack([q_gain[g], k_gain[g], jnp.ones_like(q_gain[g])])[:, None, :]


def _sequence_trunk(x, mods, w, tm):
    bx, t, d = x.shape
    new_kv, new_ret, new_conv = [[] for _ in range(N_GROUPS)], [], []
    for layer in range(DEPTH):
        m = mods[layer]
        h = _norm_mod(x, w["norm_mix"][layer], m, 0, 1, tm)
        if layer % N_MIXERS == 0:
            a = layer // N_MIXERS
            outs, lses = [], []
            for g in range(N_GROUPS):
                dil = DILATIONS[g]
                qkv = _qkv_project(h, w["attn_w_qkv"][a], _qk_gains(w["attn_q_gain"][a], w["attn_k_gain"][a], g),
                                   g, dil, tm)
                o, lse = _window_attention(qkv, _band_bias(w["rel_bias"], g), dil)
                outs.append(o)
                lses.append(lse)
                keep = min(WINDOWS[g], t)
                rows = keep // dil
                kv = qkv[:, 1:3, :, :, t // dil - rows:, :]
                kv = jnp.transpose(kv, (0, 4, 2, 1, 3, 5))
                new_kv[g].append(kv.reshape(bx, keep, 2, HEADS, HEAD_DIM))
            x = _attn_out(outs, lses, w["attn_w_o"][a], x, m, 2, min(tm, 256))
        else:
            r = layer // N_MIXERS
            proj = _matmul(h, w["ret_w_in"][r], tm, 1024)
            cos_rep, sin_signed = _rotation_tables(jnp.arange(t, dtype=jnp.int32))
            y, state = _retention_sequence(proj, cos_rep, sin_signed, w["ret_gn_gain"][r])
            new_ret.append(state)
            x = _resid_matmul(y, w["ret_w_o"][r], x, m, 2, tm, 512)
        h = _norm_mod(x, w["norm_ffn"][layer], m, 3, 4, tm)
        g_act, rows = _ffn_up_sequence(h, w["ffn_w_up"][layer], w["ffn_conv_w"][layer], w["ffn_conv_b"][layer])
        new_conv.append(rows)
        x = _resid_matmul(g_act, w["ffn_w_down"][layer], x, m, 5, tm, 512)
    return x, [jnp.stack(kv) for kv in new_kv], jnp.stack(new_ret), jnp.stack(new_conv)


def _step_trunk(x, mods, caches, ret_state, conv_state, position, w):
    b, d = x.shape
    x = x[None]
    new_kv, new_ret, new_conv = [[] for _ in range(N_GROUPS)], [], []
    for layer in range(DEPTH):
        m = mods[layer]
        h = _norm_mod(x, w["norm_mix"][layer], m, 0, 1, b)
        if layer % N_MIXERS == 0:
            a = layer // N_MIXERS
            outs, lses = [], []
            for g in range(N_GROUPS):
                qkv = _qkv_project(h, w["attn_w_qkv"][a], _qk_gains(w["attn_q_gain"][a], w["attn_k_gain"][a], g),
                                   g, 1, b)
                qkv = jnp.transpose(qkv.reshape(3, HEADS, b, HEAD_DIM), (0, 2, 1, 3))
                o, lse = _step_attention(qkv[0], qkv[1], qkv[2], caches[g], a, _group_bias(w["rel_bias"], g),
                                         DILATIONS[g], b)
                outs.append(jnp.transpose(o, (1, 0, 2))[None])
                lses.append(jnp.transpose(lse, (1, 0, 2))[None])
                new_kv[g].append(_cache_shift(caches[g], a, qkv[1], qkv[2]))
            x = _attn_out(outs, lses, w["attn_w_o"][a], x, m, 2, b)
        else:
            r = layer // N_MIXERS
            proj = _matmul(h, w["ret_w_in"][r], b, 1024)[0]
            q = proj[:, :RET_QK_WIDTH].reshape(b, RET_HEADS, RET_DK)
            k = proj[:, RET_QK_WIDTH:2 * RET_QK_WIDTH].reshape(b, RET_HEADS, RET_DK)
            v = proj[:, 2 * RET_QK_WIDTH:2 * RET_QK_WIDTH + RET_V_WIDTH].reshape(b, RET_HEADS, RET_DV)
            gate = proj[:, 2 * RET_QK_WIDTH + RET_V_WIDTH:].reshape(b, RET_HEADS, RET_DV)
            cos_rep, sin_signed = _rotation_tables(position)
            y, state = _retention_step(q, k, v, gate, ret_state, r, cos_rep, sin_signed, w["ret_gn_gain"][r])
            new_ret.append(state)
            x = _resid_matmul(y.reshape(1, b, RET_V_WIDTH), w["ret_w_o"][r], x, m, 2, b, 512)
        h = _norm_mod(x, w["norm_ffn"][layer], m, 3, 4, b)
        g_act, u_new = _ffn_up_step(h[0], w["ffn_w_up"][layer], w["ffn_conv_w"][layer], w["ffn_conv_b"][layer],
                                    conv_state[layer, :, 0], conv_state[layer, :, 1])
        new_conv.append(jnp.stack([conv_state[layer, :, 1], u_new], axis=1))
        x = _resid_matmul(g_act[None], w["ffn_w_down"][layer], x, m, 5, b, 512)
    return x[0], [jnp.stack(kv) for kv in new_kv], jnp.stack(new_ret), jnp.stack(new_conv)


def kernel(x_prompt, x_sample, cache_attn_kv_w128, cache_attn_kv_w512, cache_attn_kv_w2048, state_ret,
           state_conv, c_prompt, c_sample, rel_bias, w_ada, b_ada, norm_mix, norm_ffn, attn_w_qkv,
           attn_q_gain, attn_k_gain, attn_w_o, ret_w_in, ret_gn_gain, ret_w_o, ffn_w_up, ffn_conv_w,
           ffn_conv_b, ffn_w_down):
    w = dict(rel_bias=rel_bias, norm_mix=norm_mix, norm_ffn=norm_ffn, attn_w_qkv=attn_w_qkv,
             attn_q_gain=attn_q_gain, attn_k_gain=attn_k_gain, attn_w_o=attn_w_o, ret_w_in=ret_w_in,
             ret_gn_gain=ret_gn_gain, ret_w_o=ret_w_o, ffn_w_up=ffn_w_up, ffn_conv_w=ffn_conv_w,
             ffn_conv_b=ffn_conv_b, ffn_w_down=ffn_w_down)
    bp, sp, d = x_prompt.shape
    bs = x_sample.shape[0]
    past_len = cache_attn_kv_w2048.shape[2]

    rows = bp + bs
    pad = (-rows) % 8
    c_all = jnp.concatenate([c_prompt, c_sample, jnp.zeros((pad, d), F32)], axis=0)
    mods = _ada_modulation(c_all, w_ada, b_ada)
    mods_p = mods[:, :bp].reshape(DEPTH, bp, 1, 6 * d)
    mods_s = mods[:, bp:rows].reshape(DEPTH, 1, bs, 6 * d)

    y_p, kv_p, ret_p, conv_p = _sequence_trunk(x_prompt, mods_p, w, 512)
    caches = (cache_attn_kv_w128, cache_attn_kv_w512, cache_attn_kv_w2048)
    position = past_len + jnp.arange(1, dtype=jnp.int32)
    y_s, kv_s, ret_s, conv_s = _step_trunk(x_sample[:, 0], mods_s, caches, state_ret, state_conv, position, w)

    return (y_p, y_s[:, None, :], kv_p[0], kv_p[1], kv_p[2], ret_p, conv_p,
            kv_s[0], kv_s[1], kv_s[2], ret_s, conv_s)
```

```python
import functools
import math

import jax
import jax.numpy as jnp
from jax import lax
from jax.experimental import pallas as pl
from jax.experimental.pallas import tpu as pltpu

D_MODEL = 2048
DEPTH = 2
N_MIXERS = 2

WINDOWS = (128, 512, 2048)
DILATIONS = (1, 4, 16)
N_GROUPS = 3
HEAD_DIM = 128
HEADS = 8
ATTN_WIDTH = HEADS * HEAD_DIM
N_KEYS = 129
KEY_BLOCK = 128
N_BUCKETS = 32
MAX_DISTANCE = 2048
NEG_INF = -1e30

RET_HEADS = 8
RET_DK = 256
RET_DV = 512
RET_QK_WIDTH = RET_HEADS * RET_DK
RET_V_WIDTH = RET_HEADS * RET_DV
RET_CHUNK = 128
RET_HEAD_GROUP = 2
ROT_BASE = 10000.0

FFN_DIM = 5504
FFN_TILE = 128
N_FFN_TILES = FFN_DIM // FFN_TILE
NORM_EPS = 1e-6

F32 = jnp.float32
MXU_DTYPE = jnp.bfloat16
VMEM_LIMIT_BYTES = 56 * 1024 * 1024


def _params(*semantics):
    return pltpu.CompilerParams(dimension_semantics=semantics, vmem_limit_bytes=VMEM_LIMIT_BYTES)


def _silu(x):
    return x * jax.nn.sigmoid(x)


def _ada_kernel(c_ref, w_ref, b_ref, o_ref):
    a = _silu(c_ref[...]).astype(MXU_DTYPE)
    w = w_ref[...].astype(MXU_DTYPE)
    o_ref[...] = jnp.dot(a, w, preferred_element_type=F32) + b_ref[...]


def _ada_modulation(c, w_ada, b_ada):
    rows = c.shape[0]
    tn = 1024
    return pl.pallas_call(
        _ada_kernel,
        grid=(DEPTH, 6 * D_MODEL // tn),
        in_specs=[pl.BlockSpec((rows, D_MODEL), lambda l, j: (0, 0)),
                  pl.BlockSpec((None, D_MODEL, tn), lambda l, j: (l, 0, j)),
                  pl.BlockSpec((None, 1, tn), lambda l, j: (l, 0, j))],
        out_specs=pl.BlockSpec((None, rows, tn), lambda l, j: (l, 0, j)),
        out_shape=jax.ShapeDtypeStruct((DEPTH, rows, 6 * D_MODEL), F32),
        compiler_params=_params("arbitrary", "arbitrary"),
        name="ada_modulation",
    )(c, w_ada, b_ada.reshape(DEPTH, 1, 6 * D_MODEL))


def _norm_mod_kernel(x_ref, g_ref, shift_ref, scale_ref, o_ref):
    x = x_ref[...]
    y = x * lax.rsqrt(jnp.mean(x * x, axis=-1, keepdims=True) + NORM_EPS) * g_ref[...]
    o_ref[...] = (y * (1.0 + scale_ref[...]) + shift_ref[...]).astype(o_ref.dtype)


def _mod_spec(mods, tm, col, n_lead):
    per_row = mods.shape[1] > 1
    rb = tm if per_row else 1

    def index(*ids):
        b, i = ids[n_lead], ids[n_lead + 1]
        return (b, i if per_row else 0, col)

    return pl.BlockSpec((None, rb, D_MODEL), index)


def _norm_mod(x, gain, mods, shift_col, scale_col, tm):
    bx, t, d = x.shape
    return pl.pallas_call(
        _norm_mod_kernel,
        grid=(bx, t // tm),
        in_specs=[pl.BlockSpec((None, tm, d), lambda b, i: (b, i, 0)),
                  pl.BlockSpec((1, d), lambda b, i: (0, 0)),
                  _mod_spec(mods, tm, shift_col, 0),
                  _mod_spec(mods, tm, scale_col, 0)],
        out_specs=pl.BlockSpec((None, tm, d), lambda b, i: (b, i, 0)),
        out_shape=jax.ShapeDtypeStruct((bx, t, d), MXU_DTYPE),
        compiler_params=_params("arbitrary", "arbitrary"),
        name="norm_mod",
    )(x, gain.reshape(1, d), mods, mods)


def _cast_weight_once(w_ref, wbf_ref):
    @pl.when((pl.program_id(1) == 0) & (pl.program_id(2) == 0))
    def _():
        wbf_ref[...] = w_ref[...].astype(wbf_ref.dtype)


def _matmul_kernel(a_ref, w_ref, o_ref, wbf_ref):
    _cast_weight_once(w_ref, wbf_ref)
    o_ref[...] = jnp.dot(a_ref[...], wbf_ref[...], preferred_element_type=F32).astype(o_ref.dtype)


def _matmul(a, w, tm, tn, out_dtype=F32):
    bx, t, k = a.shape
    n = w.shape[1]
    return pl.pallas_call(
        _matmul_kernel,
        grid=(n // tn, bx, t // tm),
        in_specs=[pl.BlockSpec((None, tm, k), lambda j, b, i: (b, i, 0)),
                  pl.BlockSpec((k, tn), lambda j, b, i: (0, j))],
        out_specs=pl.BlockSpec((None, tm, tn), lambda j, b, i: (b, i, j)),
        out_shape=jax.ShapeDtypeStruct((bx, t, n), out_dtype),
        scratch_shapes=[pltpu.VMEM((k, tn), MXU_DTYPE)],
        compiler_params=_params("arbitrary", "arbitrary", "arbitrary"),
        name="matmul",
    )(a, w)


def _resid_matmul_kernel(a_ref, w_ref, x_ref, gate_ref, o_ref, wbf_ref):
    _cast_weight_once(w_ref, wbf_ref)
    acc = jnp.dot(a_ref[...], wbf_ref[...], preferred_element_type=F32)
    o_ref[...] = x_ref[...] + gate_ref[...] * acc


def _resid_matmul(a, w, x, mods, gate_col, tm, tn):
    bx, t, k = a.shape
    n = w.shape[1]
    cols_per_group = D_MODEL // tn
    per_row = mods.shape[1] > 1
    rb = tm if per_row else 1
    gate_spec = pl.BlockSpec(
        (None, rb, tn), lambda j, b, i: (b, i if per_row else 0, gate_col * cols_per_group + j))
    return pl.pallas_call(
        _resid_matmul_kernel,
        grid=(n // tn, bx, t // tm),
        in_specs=[pl.BlockSpec((None, tm, k), lambda j, b, i: (b, i, 0)),
                  pl.BlockSpec((k, tn), lambda j, b, i: (0, j)),
                  pl.BlockSpec((None, tm, tn), lambda j, b, i: (b, i, j)),
                  gate_spec],
        out_specs=pl.BlockSpec((None, tm, tn), lambda j, b, i: (b, i, j)),
        out_shape=jax.ShapeDtypeStruct((bx, t, n), F32),
        scratch_shapes=[pltpu.VMEM((k, tn), MXU_DTYPE)],
        compiler_params=_params("arbitrary", "arbitrary", "arbitrary"),
        name="resid_matmul",
    )(a, w, x, mods)


def _qkv_kernel(a_ref, w_ref, gain_ref, o_ref, wbf_ref, y_ref, *, dilation, tm):
    _cast_weight_once(w_ref, wbf_ref)
    acc = jnp.dot(a_ref[...], wbf_ref[...], preferred_element_type=F32)
    part = pl.program_id(0)
    gain = gain_ref[...]

    @pl.when(part < 2)
    def _():
        for h in range(HEADS):
            xh = acc[:, h * HEAD_DIM:(h + 1) * HEAD_DIM]
            y_ref[h] = xh * lax.rsqrt(jnp.mean(xh * xh, axis=-1, keepdims=True) + NORM_EPS) * gain

    @pl.when(part == 2)
    def _():
        for h in range(HEADS):
            y_ref[h] = acc[:, h * HEAD_DIM:(h + 1) * HEAD_DIM]

    rows = tm // dilation
    for c in range(dilation):
        for h in range(HEADS):
            if dilation == 1:
                o_ref[c, h] = y_ref[h]
            else:
                o_ref[c, h] = y_ref[h, pl.ds(c, rows, stride=dilation), :]


def _qkv_project(h, w_qkv, gains, group, dilation, tm):
    bx, t, d = h.shape
    td = t // dilation
    kern = functools.partial(_qkv_kernel, dilation=dilation, tm=tm)
    return pl.pallas_call(
        kern,
        grid=(3, bx, t // tm),
        in_specs=[pl.BlockSpec((None, tm, d), lambda p, b, i: (b, i, 0)),
                  pl.BlockSpec((d, ATTN_WIDTH), lambda p, b, i: (0, group * 3 + p)),
                  pl.BlockSpec((None, 1, HEAD_DIM), lambda p, b, i: (p, 0, 0))],
        out_specs=pl.BlockSpec((None, None, dilation, HEADS, tm // dilation, HEAD_DIM),
                               lambda p, b, i: (b, p, 0, 0, i, 0)),
        out_shape=jax.ShapeDtypeStruct((bx, 3, dilation, HEADS, td, HEAD_DIM), F32),
        scratch_shapes=[pltpu.VMEM((d, ATTN_WIDTH), MXU_DTYPE), pltpu.VMEM((HEADS, tm, HEAD_DIM), F32)],
        compiler_params=_params("arbitrary", "arbitrary", "arbitrary"),
        name=f"qkv_project_g{group}",
    )(h, w_qkv, gains)


def _t5_causal_bucket(dist):
    max_exact = N_BUCKETS // 2
    d = jnp.maximum(dist, 1).astype(F32)
    large = max_exact + (jnp.log(d / max_exact) / math.log(MAX_DISTANCE / max_exact)
                         * (N_BUCKETS - max_exact)).astype(jnp.int32)
    return jnp.where(dist < max_exact, dist, jnp.minimum(large, N_BUCKETS - 1))


def _bucket_lookup(rel_bias, g, bucket, fill):
    cols = rel_bias[:, g * HEADS:(g + 1) * HEADS].astype(F32)
    expand = (slice(None),) + (None,) * bucket.ndim
    out = jnp.full((HEADS,) + bucket.shape, fill, F32)
    for b in range(N_BUCKETS):
        out = jnp.where(bucket[None] == b, cols[b][expand], out)
    return out


def _group_bias(rel_bias, g):
    dist = DILATIONS[g] * jnp.arange(N_KEYS, dtype=jnp.int32)
    return _bucket_lookup(rel_bias, g, _t5_causal_bucket(dist), 0.0)


def _band_bias(rel_bias, g):
    a = jnp.arange(KEY_BLOCK, dtype=jnp.int32)[:, None]
    c = jnp.arange(2 * KEY_BLOCK, dtype=jnp.int32)[None, :]
    rel = a + KEY_BLOCK - c
    valid = (rel >= 0) & (rel <= KEY_BLOCK)
    bucket = jnp.where(valid, _t5_causal_bucket(DILATIONS[g] * jnp.clip(rel, 0, KEY_BLOCK)), -1)
    return _bucket_lookup(rel_bias, g, bucket, NEG_INF)


def _window_attn_kernel(q_ref, kp_ref, kc_ref, vp_ref, vc_ref, bias_ref, o_ref, lse_ref,
                        *, dilation, heads):
    i = pl.program_id(1)
    c = pl.program_id(3)
    col = lax.broadcasted_iota(jnp.int32, (KEY_BLOCK, 2 * KEY_BLOCK), 1)
    no_prev = (i == 0) & (col < KEY_BLOCK)
    for h in range(heads):
        q = q_ref[h].astype(MXU_DTYPE)
        k = jnp.concatenate([kp_ref[h], kc_ref[h]], axis=0).astype(MXU_DTYPE)
        v = jnp.concatenate([vp_ref[h], vc_ref[h]], axis=0).astype(MXU_DTYPE)
        s = lax.dot_general(q, k, (((1,), (1,)), ((), ())), preferred_element_type=F32)
        s = s * (HEAD_DIM ** -0.5) + bias_ref[h]
        s = jnp.where(no_prev, NEG_INF, s)
        m = jnp.max(s, axis=-1, keepdims=True)
        p = jnp.exp(s - m)
        l = jnp.sum(p, axis=-1, keepdims=True)
        o = jnp.dot(p.astype(MXU_DTYPE), v, preferred_element_type=F32) / l
        lse = jnp.broadcast_to(m + jnp.log(l), (KEY_BLOCK, HEAD_DIM))
        if dilation == 1:
            o_ref[h] = o
            lse_ref[h] = lse
        else:
            o_ref[h, pl.ds(c, KEY_BLOCK, stride=dilation), :] = o
            lse_ref[h, pl.ds(c, KEY_BLOCK, stride=dilation), :] = lse


def _window_attention(qkv, band_bias, dilation):
    bx, _, _, _, td, _ = qkv.shape
    t = td * dilation
    nblk = td // KEY_BLOCK
    hb = 1
    heads = HEADS // hb

    def spec(part, prev):
        def index(b, i, hq, c):
            return (b, part, c, hq, jnp.maximum(i - 1, 0) if prev else i, 0)
        return pl.BlockSpec((None, None, None, heads, KEY_BLOCK, HEAD_DIM), index)

    out_spec = pl.BlockSpec((None, heads, KEY_BLOCK * dilation, HEAD_DIM), lambda b, i, hq, c: (b, hq, i, 0))
    kern = functools.partial(_window_attn_kernel, dilation=dilation, heads=heads)
    return pl.pallas_call(
        kern,
        grid=(bx, nblk, hb, dilation),
        in_specs=[spec(0, False), spec(1, True), spec(1, False), spec(2, True), spec(2, False),
                  pl.BlockSpec((heads, KEY_BLOCK, 2 * KEY_BLOCK), lambda b, i, hq, c: (hq, 0, 0))],
        out_specs=[out_spec, out_spec],
        out_shape=[jax.ShapeDtypeStruct((bx, HEADS, t, HEAD_DIM), F32)] * 2,
        compiler_params=_params("arbitrary", "arbitrary", "arbitrary", "arbitrary"),
        name=f"window_attention_d{dilation}",
    )(qkv, qkv, qkv, qkv, qkv, band_bias)


STEP_ROWS = 8
STEP_SEQS = 8


def _step_attn_kernel(*refs):
    q_ref, kn_ref, vn_ref = refs[:3]
    kv_refs = refs[3:3 + STEP_ROWS]
    bias0_ref, bias_ref, o_ref, lse_ref, m_ref, l_ref, acc_ref = refs[3 + STEP_ROWS:]
    j = pl.program_id(1)
    scale = HEAD_DIM ** -0.5

    def chunk(ci, carry):
        sl = pl.ds(pl.multiple_of(ci * STEP_SEQS, STEP_SEQS), STEP_SEQS)
        q = q_ref[sl]

        @pl.when(j == 0)
        def _():
            s0 = jnp.sum(q * kn_ref[sl], axis=-1, keepdims=True) * scale + bias0_ref[...]
            m_ref[sl] = jnp.broadcast_to(s0, q.shape)
            l_ref[sl] = jnp.ones(q.shape, F32)
            acc_ref[sl] = vn_ref[sl]

        scores = [jnp.sum(q * kv_refs[r][sl, 0], axis=-1, keepdims=True) * scale + bias_ref[r]
                  for r in range(STEP_ROWS)]
        m_old = m_ref[sl]
        m_new = m_old
        for s in scores:
            m_new = jnp.maximum(m_new, s)
        alpha = jnp.exp(m_old - m_new)
        l_new = alpha * l_ref[sl]
        acc = alpha * acc_ref[sl]
        for r in range(STEP_ROWS):
            p = jnp.exp(scores[r] - m_new)
            l_new = l_new + p
            acc = acc + p * kv_refs[r][sl, 1]
        m_ref[sl] = m_new
        l_ref[sl] = l_new
        acc_ref[sl] = acc
        return carry

    n_chunks = q_ref.shape[0] // STEP_SEQS
    lax.fori_loop(0, n_chunks, chunk, 0, unroll=2 if n_chunks % 2 == 0 else 1)

    @pl.when(j == pl.num_programs(1) - 1)
    def _():
        o_ref[...] = acc_ref[...] / l_ref[...]
        lse_ref[...] = m_ref[...] + jnp.log(l_ref[...])


def _step_attention(q, k_new, v_new, cache, layer, gb, dilation, bt):
    b = q.shape[0]
    nk = N_KEYS - 1
    lanes = jnp.broadcast_to(gb.T[:, :, None], (N_KEYS, HEADS, HEAD_DIM))
    bias0 = lanes[0]
    bias_rows = lanes[:0:-1]
    qspec = pl.BlockSpec((bt, HEADS, HEAD_DIM), lambda bi, j: (bi, 0, 0))

    def row_spec(r):
        return pl.BlockSpec((None, bt, None, 2, HEADS, HEAD_DIM),
                            lambda bi, j: (layer, bi, (j * STEP_ROWS + r) * dilation, 0, 0, 0))

    return pl.pallas_call(
        _step_attn_kernel,
        grid=(b // bt, nk // STEP_ROWS),
        in_specs=[qspec, qspec, qspec] + [row_spec(r) for r in range(STEP_ROWS)] + [
            pl.BlockSpec((HEADS, HEAD_DIM), lambda bi, j: (0, 0)),
            pl.BlockSpec((STEP_ROWS, HEADS, HEAD_DIM), lambda bi, j: (j, 0, 0))],
        out_specs=[qspec, qspec],
        out_shape=[jax.ShapeDtypeStruct((b, HEADS, HEAD_DIM), F32)] * 2,
        scratch_shapes=[pltpu.VMEM((bt, HEADS, HEAD_DIM), F32)] * 3,
        compiler_params=_params("arbitrary", "arbitrary"),
        name=f"step_attention_d{dilation}",
    )(q, k_new, v_new, *([cache] * STEP_ROWS), bias0, bias_rows)


def _attn_out_kernel(o0_ref, o1_ref, o2_ref, l0_ref, l1_ref, l2_ref, w_ref, x_ref, gate_ref, out_ref,
                     wbf_ref):
    _cast_weight_once(w_ref, wbf_ref)
    heads = []
    for h in range(HEADS):
        l0, l1, l2 = l0_ref[h], l1_ref[h], l2_ref[h]
        m = jnp.maximum(jnp.maximum(l0, l1), l2)
        e0, e1, e2 = jnp.exp(l0 - m), jnp.exp(l1 - m), jnp.exp(l2 - m)
        tot = e0 + e1 + e2
        merged = (e0 / tot) * o0_ref[h] + (e1 / tot) * o1_ref[h] + (e2 / tot) * o2_ref[h]
        heads.append(merged.astype(MXU_DTYPE))
    acc = jnp.dot(jnp.concatenate(heads, axis=-1), wbf_ref[...], preferred_element_type=F32)
    out_ref[...] = x_ref[...] + gate_ref[...] * acc


def _attn_out(outs, lses, w_o, x, mods, gate_col, tm):
    bx, t, d = x.shape
    per_row = mods.shape[1] > 1
    rb = tm if per_row else 1
    aspec = pl.BlockSpec((None, HEADS, tm, HEAD_DIM), lambda j, b, i: (b, 0, i, 0))
    xspec = pl.BlockSpec((None, tm, d), lambda j, b, i: (b, i, 0))
    return pl.pallas_call(
        _attn_out_kernel,
        grid=(1, bx, t // tm),
        in_specs=[aspec] * 6 + [
            pl.BlockSpec((ATTN_WIDTH, d), lambda j, b, i: (0, 0)),
            xspec,
            pl.BlockSpec((None, rb, d), lambda j, b, i: (b, i if per_row else 0, gate_col))],
        out_specs=xspec,
        out_shape=jax.ShapeDtypeStruct((bx, t, d), F32),
        scratch_shapes=[pltpu.VMEM((ATTN_WIDTH, d), MXU_DTYPE)],
        compiler_params=_params("arbitrary", "arbitrary", "arbitrary"),
        name="attn_out",
    )(*outs, *lses, w_o, x, mods)


KV_ROWS = 2 * HEADS
SHIFT_BUFFERS = 3
SHIFT_CHUNK_BYTES = 8 * 1024 * 1024


def _shift_chunking(b, keep_rows):
    row_bytes = HEAD_DIM * 4
    pieces = 1
    while (keep_rows // pieces) * row_bytes > SHIFT_CHUNK_BYTES or keep_rows % (8 * pieces):
        pieces += 1
    rows = keep_rows // pieces
    seqs = 1
    while seqs * 2 * rows * row_bytes <= SHIFT_CHUNK_BYTES // 2 and b % (seqs * 2) == 0:
        seqs *= 2
    return seqs, rows, pieces


def _cache_shift_kernel(prev_ref, new_ref, out_ref, buf_ref, in_sems, out_sems, new_sem,
                        *, layer, seqs, rows, pieces):
    b, total_rows, _ = out_ref.shape
    n_chunks = (b // seqs) * pieces

    def load(k, slot):
        block, piece = k // pieces, k % pieces
        src = prev_ref.at[pl.ds(layer * b + block * seqs, seqs), pl.ds(KV_ROWS + piece * rows, rows)]
        return pltpu.make_async_copy(src, buf_ref.at[slot], in_sems.at[slot])

    def store(k, slot):
        block, piece = k // pieces, k % pieces
        dst = out_ref.at[pl.ds(block * seqs, seqs), pl.ds(piece * rows, rows)]
        return pltpu.make_async_copy(buf_ref.at[slot], dst, out_sems.at[slot])

    new_copy = pltpu.make_async_copy(new_ref, out_ref.at[:, pl.ds(total_rows - KV_ROWS, KV_ROWS)], new_sem)
    new_copy.start()
    load(0, 0).start()

    def step(k, carry):
        slot = k % SHIFT_BUFFERS
        nxt = k + 1
        nxt_slot = nxt % SHIFT_BUFFERS

        @pl.when(nxt < n_chunks)
        def _():
            @pl.when(nxt >= SHIFT_BUFFERS)
            def _():
                store(nxt - SHIFT_BUFFERS, nxt_slot).wait()
            load(nxt, nxt_slot).start()

        load(k, slot).wait()
        store(k, slot).start()
        return carry

    lax.fori_loop(0, n_chunks, step, 0)
    for k in range(max(0, n_chunks - SHIFT_BUFFERS), n_chunks):
        store(k, k % SHIFT_BUFFERS).wait()
    new_copy.wait()


def _cache_shift(prev, layer, k_new, v_new):
    nl, b, w = prev.shape[:3]
    prev_rows = prev.reshape(nl * b, w * KV_ROWS, HEAD_DIM)
    new_rows = jnp.concatenate([k_new, v_new], axis=1)
    seqs, rows, pieces = _shift_chunking(b, (w - 1) * KV_ROWS)
    out = pl.pallas_call(
        functools.partial(_cache_shift_kernel, layer=layer, seqs=seqs, rows=rows, pieces=pieces),
        in_specs=[pl.BlockSpec(memory_space=pl.ANY),
                  pl.BlockSpec(memory_space=pltpu.VMEM)],
        out_specs=pl.BlockSpec(memory_space=pl.ANY),
        out_shape=jax.ShapeDtypeStruct((b, w * KV_ROWS, HEAD_DIM), prev.dtype),
        scratch_shapes=[pltpu.VMEM((SHIFT_BUFFERS, seqs, rows, HEAD_DIM), prev.dtype),
                        pltpu.SemaphoreType.DMA((SHIFT_BUFFERS,)),
                        pltpu.SemaphoreType.DMA((SHIFT_BUFFERS,)),
                        pltpu.SemaphoreType.DMA(())],
        compiler_params=pltpu.CompilerParams(vmem_limit_bytes=VMEM_LIMIT_BYTES),
        name="cache_shift",
    )(prev_rows, new_rows)
    return out.reshape(b, w, 2, HEADS, HEAD_DIM)


def _rotation_tables(positions):
    half = RET_DK // 2
    inv_freq = 1.0 / (ROT_BASE ** jnp.linspace(0.0, 1.0, half, dtype=F32))
    ang = positions.astype(F32)[:, None] * inv_freq[None, :]
    cos, sin = jnp.cos(ang), jnp.sin(ang)
    cos_rep = jnp.stack([cos, cos], axis=-1).reshape(-1, RET_DK)
    sin_signed = jnp.stack([-sin, sin], axis=-1).reshape(-1, RET_DK)
    return cos_rep, sin_signed


def _rotate_pairs(x, cos_rep, sin_signed):
    lanes = 128
    even = lax.broadcasted_iota(jnp.int32, (x.shape[0], lanes), 1) % 2 == 0
    parts = []
    for s in range(x.shape[1] // lanes):
        xs = x[:, s * lanes:(s + 1) * lanes]
        parts.append(jnp.where(even, pltpu.roll(xs, lanes - 1, 1), pltpu.roll(xs, 1, 1)))
    swapped = jnp.concatenate(parts, axis=-1)
    return x * cos_rep + swapped * sin_signed


def _log_gamma():
    return jnp.log1p(-jnp.exp2(-5.0 - jnp.arange(RET_HEADS, dtype=F32)))


def _group_norm_gate(o, gain, gate):
    mu = jnp.mean(o, axis=-1, keepdims=True)
    var = jnp.mean(jnp.square(o - mu), axis=-1, keepdims=True)
    return _silu(gate) * ((o - mu) * lax.rsqrt(var + NORM_EPS) * gain)


def _retention_kernel(q_ref, k_ref, v_ref, gate_ref, cos_ref, sin_ref, decay_ref, qdec_ref, kdec_ref,
                      cdec_ref, gain_ref, y_ref, s_out_ref, s_ref):
    c = pl.program_id(1)
    nb, hg = s_ref.shape[:2]

    @pl.when(c == 0)
    def _():
        s_ref[...] = jnp.zeros(s_ref.shape, F32)

    cos, sin = cos_ref[...], sin_ref[...]
    for b in range(nb):
        for hh in range(hg):
            ksl = slice(hh * RET_DK, (hh + 1) * RET_DK)
            vsl = slice(hh * RET_DV, (hh + 1) * RET_DV)
            q = _rotate_pairs(q_ref[b, :, ksl], cos, sin)
            k = _rotate_pairs(k_ref[b, :, ksl], cos, sin) * (RET_DK ** -0.5)
            v = v_ref[b, :, vsl].astype(MXU_DTYPE)
            state = s_ref[b, hh]
            scores = lax.dot_general(q.astype(MXU_DTYPE), k.astype(MXU_DTYPE), (((1,), (1,)), ((), ())),
                                     preferred_element_type=F32) * decay_ref[hh]
            o = jnp.dot(scores.astype(MXU_DTYPE), v, preferred_element_type=F32)
            o = o + jnp.dot((q * qdec_ref[hh]).astype(MXU_DTYPE), state.astype(MXU_DTYPE),
                            preferred_element_type=F32)
            kd_t = jnp.transpose(k * kdec_ref[hh]).astype(MXU_DTYPE)
            s_ref[b, hh] = cdec_ref[hh] * state + jnp.dot(kd_t, v, preferred_element_type=F32)
            y_ref[b, :, vsl] = _group_norm_gate(o, gain_ref[hh], gate_ref[b, :, vsl]).astype(y_ref.dtype)

    @pl.when(c == pl.num_programs(1) - 1)
    def _():
        s_out_ref[...] = s_ref[...]


def _retention_sequence(proj, cos_rep, sin_signed, gn_gain):
    bx, t, _ = proj.shape
    cw = RET_CHUNK
    lg = _log_gamma()
    pos = jnp.arange(cw, dtype=F32)
    diff = pos[:, None] - pos[None, :]
    decay = jnp.where(diff >= 0, jnp.exp(diff[None] * lg[:, None, None]), 0.0)
    q_decay = jnp.exp((pos[:, None] + 1.0) * lg[None, :]).T[:, :, None]
    k_decay = jnp.exp((cw - 1.0 - pos)[:, None] * lg[None, :]).T[:, :, None]
    chunk_decay = jnp.exp(cw * lg).reshape(RET_HEADS, 1, 1)
    hg = RET_HEAD_GROUP
    nqk = RET_QK_WIDTH // (hg * RET_DK)
    nv0 = 2 * RET_QK_WIDTH // (hg * RET_DV)
    ng0 = nv0 + RET_HEADS // hg
    return pl.pallas_call(
        _retention_kernel,
        grid=(RET_HEADS // hg, t // cw),
        in_specs=[pl.BlockSpec((bx, cw, hg * RET_DK), lambda h, c: (0, c, h)),
                  pl.BlockSpec((bx, cw, hg * RET_DK), lambda h, c: (0, c, nqk + h)),
                  pl.BlockSpec((bx, cw, hg * RET_DV), lambda h, c: (0, c, nv0 + h)),
                  pl.BlockSpec((bx, cw, hg * RET_DV), lambda h, c: (0, c, ng0 + h)),
                  pl.BlockSpec((cw, RET_DK), lambda h, c: (c, 0)),
                  pl.BlockSpec((cw, RET_DK), lambda h, c: (c, 0)),
                  pl.BlockSpec((hg, cw, cw), lambda h, c: (h, 0, 0)),
                  pl.BlockSpec((hg, cw, 1), lambda h, c: (h, 0, 0)),
                  pl.BlockSpec((hg, cw, 1), lambda h, c: (h, 0, 0)),
                  pl.BlockSpec((hg, 1, 1), lambda h, c: (h, 0, 0)),
                  pl.BlockSpec((hg, 1, RET_DV), lambda h, c: (h, 0, 0))],
        out_specs=[pl.BlockSpec((bx, cw, hg * RET_DV), lambda h, c: (0, c, h)),
                   pl.BlockSpec((bx, hg, RET_DK, RET_DV), lambda h, c: (0, h, 0, 0))],
        out_shape=[jax.ShapeDtypeStruct((bx, t, RET_V_WIDTH), MXU_DTYPE),
                   jax.ShapeDtypeStruct((bx, RET_HEADS, RET_DK, RET_DV), F32)],
        scratch_shapes=[pltpu.VMEM((bx, hg, RET_DK, RET_DV), F32)],
        compiler_params=_params("arbitrary", "arbitrary"),
        name="retention_sequence",
    )(proj, proj, proj, proj, cos_rep, sin_signed, decay, q_decay, k_decay, chunk_decay,
      gn_gain.reshape(RET_HEADS, 1, RET_DV))


def _retention_step_kernel(q_ref, k_ref, v_ref, gate_ref, cos_ref, sin_ref, gamma_ref, gain_ref, s_ref,
                           y_ref, s_out_ref):
    cos, sin = cos_ref[...], sin_ref[...]
    q = _rotate_pairs(q_ref[...], cos, sin)
    k = _rotate_pairs(k_ref[...], cos, sin) * (RET_DK ** -0.5)
    qk = jnp.sum(q * k, axis=-1, keepdims=True)
    q_t = jnp.transpose(q * gamma_ref[...])
    k_t = jnp.transpose(k)
    v = v_ref[...]
    gamma = gamma_ref[...]
    rows = []
    for h in range(RET_HEADS):
        state = s_ref[h]
        vh = v[h:h + 1, :]
        rows.append(qk[h:h + 1, :] * vh + jnp.sum(q_t[:, h:h + 1] * state, axis=0, keepdims=True))
        s_out_ref[h] = gamma[h:h + 1, :] * state + k_t[:, h:h + 1] * vh
    o = jnp.concatenate(rows, axis=0)
    y_ref[...] = _group_norm_gate(o, gain_ref[...], gate_ref[...]).astype(y_ref.dtype)


def _retention_step(q, k, v, gate, state, layer, cos_rep, sin_signed, gn_gain):
    b = q.shape[0]
    gamma = jnp.exp(_log_gamma()).reshape(RET_HEADS, 1)
    qspec = pl.BlockSpec((None, RET_HEADS, RET_DK), lambda i: (i, 0, 0))
    vspec = pl.BlockSpec((None, RET_HEADS, RET_DV), lambda i: (i, 0, 0))
    sspec = pl.BlockSpec((None, RET_HEADS, RET_DK, RET_DV), lambda i: (i, 0, 0, 0))
    sspec_in = pl.BlockSpec((None, None, RET_HEADS, RET_DK, RET_DV), lambda i: (layer, i, 0, 0, 0))
    return pl.pallas_call(
        _retention_step_kernel,
        grid=(b,),
        in_specs=[qspec, qspec, vspec, vspec,
                  pl.BlockSpec((1, RET_DK), lambda i: (0, 0)),
                  pl.BlockSpec((1, RET_DK), lambda i: (0, 0)),
                  pl.BlockSpec((RET_HEADS, 1), lambda i: (0, 0)),
                  pl.BlockSpec((RET_HEADS, RET_DV), lambda i: (0, 0)),
                  sspec_in],
        out_specs=[vspec, sspec],
        out_shape=[jax.ShapeDtypeStruct((b, RET_HEADS, RET_DV), MXU_DTYPE),
                   jax.ShapeDtypeStruct(state.shape[1:], F32)],
        compiler_params=_params("arbitrary"),
        name="retention_step",
    )(q, k, v, gate, cos_rep, sin_signed, gamma, gn_gain.reshape(RET_HEADS, RET_DV), state)


SIDE_POINTS = 2
SIDE_CHUNK_BYTES = 512 * 1024


class _SideCopy:
    def __init__(self, n_seqs, keep_rows, src_seq0, dst_seq0, n_steps):
        assert n_seqs < n_steps
        row_bytes = HEAD_DIM * 4
        pieces = SIDE_POINTS
        while keep_rows % (8 * pieces) or (keep_rows // pieces) * row_bytes > SIDE_CHUNK_BYTES:
            pieces += SIDE_POINTS
        self.rows = keep_rows // pieces
        self.per_point = pieces // SIDE_POINTS
        self.n_seqs, self.src_seq0, self.dst_seq0 = n_seqs, src_seq0, dst_seq0

    def scratch_shapes(self):
        n = SIDE_POINTS * self.per_point
        return [pltpu.VMEM((n, self.rows, HEAD_DIM), F32),
                pltpu.SemaphoreType.DMA((n,)), pltpu.SemaphoreType.DMA((n,))]

    def bind(self, src_ref, dst_ref, buf_ref, load_sems, store_sems):
        self.refs = (src_ref, dst_ref, buf_ref, load_sems, store_sems)

    def _load(self, seq, point, q):
        src_ref, _, buf_ref, load_sems, _ = self.refs
        slot = point * self.per_point + q
        src = src_ref.at[self.src_seq0 + seq, pl.ds(KV_ROWS + slot * self.rows, self.rows)]
        return pltpu.make_async_copy(src, buf_ref.at[slot], load_sems.at[slot])

    def _store(self, seq, point, q):
        _, dst_ref, buf_ref, _, store_sems = self.refs
        slot = point * self.per_point + q
        dst = dst_ref.at[self.dst_seq0 + seq, pl.ds(slot * self.rows, self.rows)]
        return pltpu.make_async_copy(buf_ref.at[slot], dst, store_sems.at[slot])

    def serve(self, step, point):
        prev_point = (point - 1) % SIDE_POINTS
        prev_step = step if point > 0 else step - 1

        @pl.when((prev_step >= 0) & (prev_step < self.n_seqs))
        def _():
            for q in range(self.per_point):
                self._load(prev_step, prev_point, q).wait()
                self._store(prev_step, prev_point, q).start()

        @pl.when((step >= 1) & (step - 1 < self.n_seqs))
        def _():
            for q in range(self.per_point):
                self._store(step - 1, point, q).wait()

        @pl.when(step < self.n_seqs)
        def _():
            for q in range(self.per_point):
                self._load(step, point, q).start()


def _ffn_up_seq_kernel(*refs, t, chunk, parts, side):
    a_ref, wg_ref, wv_ref, cwg_ref, cwv_ref, cbg_ref, cbv_ref = refs[:7]
    if side is not None:
        (prev_ref, new_ref, g_ref, rows_g_ref, rows_v_ref, cache_ref, wbf_ref, u_ref,
         buf_ref, load_sems, store_sems, new_sem) = refs[7:]
        side.bind(prev_ref, cache_ref, buf_ref, load_sems, store_sems)
    else:
        g_ref, rows_g_ref, rows_v_ref, wbf_ref, u_ref = refs[7:]
    part_id = pl.program_id(2)
    step = (pl.program_id(0) * pl.num_programs(1) + pl.program_id(1)) * parts + part_id
    n_steps = pl.num_programs(0) * pl.num_programs(1) * parts

    @pl.when(part_id == 0)
    def _():
        wbf_ref[:, :FFN_TILE] = wg_ref[...].astype(wbf_ref.dtype)
        wbf_ref[:, FFN_TILE:] = wv_ref[...].astype(wbf_ref.dtype)
        u_ref[0:8, :] = jnp.zeros((8, 2 * FFN_TILE), F32)

    cw = jnp.concatenate([cwg_ref[...], cwv_ref[...]], axis=-1)
    cb = jnp.concatenate([cbg_ref[...], cbv_ref[...]], axis=-1)
    part_rows = t // parts
    n_chunks = part_rows // chunk
    stride = max(n_chunks // SIDE_POINTS, 1)

    def rows_of_part(part):
        for s in range(n_chunks):
            r0 = part * part_rows + s * chunk
            o0 = s * chunk
            if side is not None and s % stride == 0 and s // stride < SIDE_POINTS:
                side.serve(step, s // stride)
            u = jnp.dot(a_ref[r0:r0 + chunk, :], wbf_ref[...], preferred_element_type=F32)
            u_ref[8 + r0:8 + r0 + chunk, :] = u
            z = cb + cw[0:1] * u_ref[6 + r0:6 + r0 + chunk, :]
            z = z + cw[1:2] * u_ref[7 + r0:7 + r0 + chunk, :]
            z = z + cw[2:3] * u
            g_ref[o0:o0 + chunk, :] = (_silu(z[:, :FFN_TILE]) * z[:, FFN_TILE:]).astype(g_ref.dtype)

    for part in range(parts):
        pl.when(part_id == part)(functools.partial(rows_of_part, part))

    @pl.when(part_id == parts - 1)
    def _():
        rows_g_ref[...] = u_ref[6 + t:8 + t, :FFN_TILE]
        rows_v_ref[...] = u_ref[6 + t:8 + t, FFN_TILE:]

    if side is not None:
        @pl.when(step == n_steps - 1)
        def _():
            total_rows = cache_ref.shape[1]
            new_copy = pltpu.make_async_copy(new_ref, cache_ref.at[:, pl.ds(total_rows - KV_ROWS, KV_ROWS)], new_sem)
            new_copy.start()
            new_copy.wait()


def _ffn_up_sequence(h, w_up, conv_w, conv_b, shift=None):
    bx, t, d = h.shape
    nt = N_FFN_TILES
    parts = 2 if (shift is not None and t % 1024 == 0) else 1
    chunk = min(t // parts, 512)
    conv_b = conv_b.reshape(1, 2 * FFN_DIM)
    in_specs = [pl.BlockSpec((None, t, d), lambda b, j, p: (b, 0, 0)),
                pl.BlockSpec((d, FFN_TILE), lambda b, j, p: (0, j)),
                pl.BlockSpec((d, FFN_TILE), lambda b, j, p: (0, nt + j)),
                pl.BlockSpec((3, FFN_TILE), lambda b, j, p: (0, j)),
                pl.BlockSpec((3, FFN_TILE), lambda b, j, p: (0, nt + j)),
                pl.BlockSpec((1, FFN_TILE), lambda b, j, p: (0, j)),
                pl.BlockSpec((1, FFN_TILE), lambda b, j, p: (0, nt + j))]
    operands = [h, w_up, w_up, conv_w, conv_w, conv_b, conv_b]
    out_specs = [pl.BlockSpec((None, t // parts, FFN_TILE), lambda b, j, p: (b, p, j)),
                 pl.BlockSpec((None, 2, FFN_TILE), lambda b, j, p: (b, 0, j)),
                 pl.BlockSpec((None, 2, FFN_TILE), lambda b, j, p: (b, 0, j))]
    out_shape = [jax.ShapeDtypeStruct((bx, t, FFN_DIM), MXU_DTYPE),
                 jax.ShapeDtypeStruct((bx, 2, FFN_DIM), F32),
                 jax.ShapeDtypeStruct((bx, 2, FFN_DIM), F32)]
    scratch = [pltpu.VMEM((d, 2 * FFN_TILE), MXU_DTYPE), pltpu.VMEM((t + 8, 2 * FFN_TILE), F32)]
    side = None
    if shift is not None:
        cache_rows, src_seq0, new_rows = shift
        n_seqs, total_rows = new_rows.shape[0], cache_rows.shape[1]
        side = _SideCopy(n_seqs, total_rows - KV_ROWS, src_seq0, 0, bx * nt * parts)
        in_specs[0] = pl.BlockSpec((None, t, d), lambda b, j, p: (b, 0, 0), pipeline_mode=pl.Buffered(1))
        in_specs += [pl.BlockSpec(memory_space=pl.ANY), pl.BlockSpec(memory_space=pltpu.VMEM)]
        operands += [cache_rows, new_rows]
        out_specs.append(pl.BlockSpec(memory_space=pl.ANY))
        out_shape.append(jax.ShapeDtypeStruct((n_seqs, total_rows, HEAD_DIM), cache_rows.dtype))
        scratch += side.scratch_shapes() + [pltpu.SemaphoreType.DMA(())]
    kern = functools.partial(_ffn_up_seq_kernel, t=t, chunk=chunk, parts=parts, side=side)
    outs = pl.pallas_call(
        kern,
        grid=(bx, nt, parts),
        in_specs=in_specs,
        out_specs=out_specs,
        out_shape=out_shape,
        scratch_shapes=scratch,
        compiler_params=_params("arbitrary", "arbitrary", "arbitrary"),
        name="ffn_up_sequence",
    )(*operands)
    rows = jnp.concatenate([outs[1], outs[2]], axis=-1)
    return (outs[0], rows) if shift is None else (outs[0], rows, outs[3])


def _ffn_up_step_kernel(a_ref, wg_ref, wv_ref, cwg_ref, cwv_ref, cbg_ref, cbv_ref, p0g_ref, p0v_ref,
                        p1g_ref, p1v_ref, g_ref, ug_ref, uv_ref):
    w = jnp.concatenate([wg_ref[...], wv_ref[...]], axis=-1).astype(MXU_DTYPE)
    u = jnp.dot(a_ref[...], w, preferred_element_type=F32)
    cw = jnp.concatenate([cwg_ref[...], cwv_ref[...]], axis=-1)
    cb = jnp.concatenate([cbg_ref[...], cbv_ref[...]], axis=-1)
    p0 = jnp.concatenate([p0g_ref[...], p0v_ref[...]], axis=-1)
    p1 = jnp.concatenate([p1g_ref[...], p1v_ref[...]], axis=-1)
    z = cb + cw[0:1] * p0
    z = z + cw[1:2] * p1
    z = z + cw[2:3] * u
    g_ref[...] = (_silu(z[:, :FFN_TILE]) * z[:, FFN_TILE:]).astype(g_ref.dtype)
    ug_ref[...] = u[:, :FFN_TILE]
    uv_ref[...] = u[:, FFN_TILE:]


def _ffn_up_step(h, w_up, conv_w, conv_b, prev0, prev1):
    b, d = h.shape
    nt = N_FFN_TILES
    conv_b = conv_b.reshape(1, 2 * FFN_DIM)
    lo = lambda j: (0, j)
    hi = lambda j: (0, nt + j)
    g, ug, uv = pl.pallas_call(
        _ffn_up_step_kernel,
        grid=(nt,),
        in_specs=[pl.BlockSpec((b, d), lambda j: (0, 0)),
                  pl.BlockSpec((d, FFN_TILE), lo), pl.BlockSpec((d, FFN_TILE), hi),
                  pl.BlockSpec((3, FFN_TILE), lo), pl.BlockSpec((3, FFN_TILE), hi),
                  pl.BlockSpec((1, FFN_TILE), lo), pl.BlockSpec((1, FFN_TILE), hi),
                  pl.BlockSpec((b, FFN_TILE), lo), pl.BlockSpec((b, FFN_TILE), hi),
                  pl.BlockSpec((b, FFN_TILE), lo), pl.BlockSpec((b, FFN_TILE), hi)],
        out_specs=[pl.BlockSpec((b, FFN_TILE), lo)] * 3,
        out_shape=[jax.ShapeDtypeStruct((b, FFN_DIM), MXU_DTYPE),
                   jax.ShapeDtypeStruct((b, FFN_DIM), F32),
                   jax.ShapeDtypeStruct((b, FFN_DIM), F32)],
        compiler_params=_params("arbitrary"),
        name="ffn_up_step",
    )(h, w_up, w_up, conv_w, conv_w, conv_b, conv_b, prev0, prev0, prev1, prev1)
    return g, jnp.concatenate([ug, uv], axis=-1)


def _qk_gains(q_gain, k_gain, g):
    return jnp.stack([q_gain[g], k_gain[g], jnp.ones_like(q_gain[g])])[:, None, :]


def _sequence_trunk(x, mods, w, tm, carried=None):
    bx, t, d = x.shape
    new_kv, new_ret, new_conv, updated = [[] for _ in range(N_GROUPS)], [], [], []
    for layer in range(DEPTH):
        m = mods[layer]
        h = _norm_mod(x, w["norm_mix"][layer], m, 0, 1, tm)
        if layer % N_MIXERS == 0:
            a = layer // N_MIXERS
            outs, lses = [], []
            for g in range(N_GROUPS):
                dil = DILATIONS[g]
                qkv = _qkv_project(h, w["attn_w_qkv"][a], _qk_gains(w["attn_q_gain"][a], w["attn_k_gain"][a], g),
                                   g, dil, tm)
                o, lse = _window_attention(qkv, _band_bias(w["rel_bias"], g), dil)
                outs.append(o)
                lses.append(lse)
                keep = min(WINDOWS[g], t)
                rows = keep // dil
                kv = qkv[:, 1:3, :, :, t // dil - rows:, :]
                kv = jnp.transpose(kv, (0, 4, 2, 1, 3, 5))
                new_kv[g].append(kv.reshape(bx, keep, 2, HEADS, HEAD_DIM))
            x = _attn_out(outs, lses, w["attn_w_o"][a], x, m, 2, min(tm, 256))
        else:
            r = layer // N_MIXERS
            proj = _matmul(h, w["ret_w_in"][r], tm, 1024)
            cos_rep, sin_signed = _rotation_tables(jnp.arange(t, dtype=jnp.int32))
            y, state = _retention_sequence(proj, cos_rep, sin_signed, w["ret_gn_gain"][r])
            new_ret.append(state)
            x = _resid_matmul(y, w["ret_w_o"][r], x, m, 2, tm, 512)
        h = _norm_mod(x, w["norm_ffn"][layer], m, 3, 4, tm)
        ffn_w = (w["ffn_w_up"][layer], w["ffn_conv_w"][layer], w["ffn_conv_b"][layer])
        if carried is None or carried[layer] is None:
            g_act, rows = _ffn_up_sequence(h, *ffn_w)
        else:
            cache, a, k_new, v_new = carried[layer]
            nl, n_seqs, width = cache.shape[:3]
            cache_rows = cache.reshape(nl * n_seqs, width * KV_ROWS, HEAD_DIM)
            new_rows = jnp.concatenate([k_new, v_new], axis=1)
            g_act, rows, upd = _ffn_up_sequence(h, *ffn_w, shift=(cache_rows, a * n_seqs, new_rows))
            updated.append(upd.reshape(n_seqs, width, 2, HEADS, HEAD_DIM))
        new_conv.append(rows)
        x = _resid_matmul(g_act, w["ffn_w_down"][layer], x, m, 5, tm, 512)
    return x, [jnp.stack(kv) for kv in new_kv], jnp.stack(new_ret), jnp.stack(new_conv), updated


def _step_qkv(x, m, w, layer):
    b = x.shape[1]
    a = layer // N_MIXERS
    h = _norm_mod(x, w["norm_mix"][layer], m, 0, 1, b)
    out = []
    for g in range(N_GROUPS):
        qkv = _qkv_project(h, w["attn_w_qkv"][a], _qk_gains(w["attn_q_gain"][a], w["attn_k_gain"][a], g), g, 1, b)
        out.append(jnp.transpose(qkv.reshape(3, HEADS, b, HEAD_DIM), (0, 2, 1, 3)))
    return out


def _step_trunk(x, mods, caches, ret_state, conv_state, position, w, qkv_first=None, updated=None):
    b, d = x.shape
    x = x[None]
    new_kv, new_ret, new_conv = [[] for _ in range(N_GROUPS)], [], []
    for layer in range(DEPTH):
        m = mods[layer]
        if layer % N_MIXERS == 0:
            a = layer // N_MIXERS
            qkvs = qkv_first if (layer == 0 and qkv_first is not None) else _step_qkv(x, m, w, layer)
            outs, lses = [], []
            for g in range(N_GROUPS):
                qkv = qkvs[g]
                o, lse = _step_attention(qkv[0], qkv[1], qkv[2], caches[g], a, _group_bias(w["rel_bias"], g),
                                         DILATIONS[g], b)
                outs.append(jnp.transpose(o, (1, 0, 2))[None])
                lses.append(jnp.transpose(lse, (1, 0, 2))[None])
                if layer == 0 and updated is not None and g in updated:
                    new_kv[g].append(updated[g])
                else:
                    new_kv[g].append(_cache_shift(caches[g], a, qkv[1], qkv[2]))
            x = _attn_out(outs, lses, w["attn_w_o"][a], x, m, 2, b)
        else:
            r = layer // N_MIXERS
            h = _norm_mod(x, w["norm_mix"][layer], m, 0, 1, b)
            proj = _matmul(h, w["ret_w_in"][r], b, 1024)[0]
            q = proj[:, :RET_QK_WIDTH].reshape(b, RET_HEADS, RET_DK)
            k = proj[:, RET_QK_WIDTH:2 * RET_QK_WIDTH].reshape(b, RET_HEADS, RET_DK)
            v = proj[:, 2 * RET_QK_WIDTH:2 * RET_QK_WIDTH + RET_V_WIDTH].reshape(b, RET_HEADS, RET_DV)
            gate = proj[:, 2 * RET_QK_WIDTH + RET_V_WIDTH:].reshape(b, RET_HEADS, RET_DV)
            cos_rep, sin_signed = _rotation_tables(position)
            y, state = _retention_step(q, k, v, gate, ret_state, r, cos_rep, sin_signed, w["ret_gn_gain"][r])
            new_ret.append(state)
            x = _resid_matmul(y.reshape(1, b, RET_V_WIDTH), w["ret_w_o"][r], x, m, 2, b, 512)
        h = _norm_mod(x, w["norm_ffn"][layer], m, 3, 4, b)
        g_act, u_new = _ffn_up_step(h[0], w["ffn_w_up"][layer], w["ffn_conv_w"][layer], w["ffn_conv_b"][layer],
                                    conv_state[layer, :, 0], conv_state[layer, :, 1])
        new_conv.append(jnp.stack([conv_state[layer, :, 1], u_new], axis=1))
        x = _resid_matmul(g_act[None], w["ffn_w_down"][layer], x, m, 5, b, 512)
    return x[0], [jnp.stack(kv) for kv in new_kv], jnp.stack(new_ret), jnp.stack(new_conv)


def kernel(x_prompt, x_sample, cache_attn_kv_w128, cache_attn_kv_w512, cache_attn_kv_w2048, state_ret,
           state_conv, c_prompt, c_sample, rel_bias, w_ada, b_ada, norm_mix, norm_ffn, attn_w_qkv,
           attn_q_gain, attn_k_gain, attn_w_o, ret_w_in, ret_gn_gain, ret_w_o, ffn_w_up, ffn_conv_w,
           ffn_conv_b, ffn_w_down):
    w = dict(rel_bias=rel_bias, norm_mix=norm_mix, norm_ffn=norm_ffn, attn_w_qkv=attn_w_qkv,
             attn_q_gain=attn_q_gain, attn_k_gain=attn_k_gain, attn_w_o=attn_w_o, ret_w_in=ret_w_in,
             ret_gn_gain=ret_gn_gain, ret_w_o=ret_w_o, ffn_w_up=ffn_w_up, ffn_conv_w=ffn_conv_w,
             ffn_conv_b=ffn_conv_b, ffn_w_down=ffn_w_down)
    bp, sp, d = x_prompt.shape
    bs = x_sample.shape[0]
    past_len = cache_attn_kv_w2048.shape[2]

    rows = bp + bs
    pad = (-rows) % 8
    c_all = jnp.concatenate([c_prompt, c_sample, jnp.zeros((pad, d), F32)], axis=0)
    mods = _ada_modulation(c_all, w_ada, b_ada)
    mods_p = mods[:, :bp].reshape(DEPTH, bp, 1, 6 * d)
    mods_s = mods[:, bp:rows].reshape(DEPTH, 1, bs, 6 * d)

    caches = (cache_attn_kv_w128, cache_attn_kv_w512, cache_attn_kv_w2048)
    x_s = x_sample[:, 0]
    qkv_first = _step_qkv(x_s[None], mods_s[0], w, 0)
    hosted = (N_GROUPS - 1, N_GROUPS - 2)
    carried = [(caches[g], 0, qkv_first[g][1], qkv_first[g][2]) for g in hosted]
    y_p, kv_p, ret_p, conv_p, updated = _sequence_trunk(x_prompt, mods_p, w, 512, carried)
    position = past_len + jnp.arange(1, dtype=jnp.int32)
    y_s, kv_s, ret_s, conv_s = _step_trunk(x_s, mods_s, caches, state_ret, state_conv, position, w,
                                           qkv_first, dict(zip(hosted, updated)))

    return (y_p, y_s[:, None, :], kv_p[0], kv_p[1], kv_p[2], ret_p, conv_p,
            kv_s[0], kv_s[1], kv_s[2], ret_s, conv_s)
```

```python
import functools
import math

import jax
import jax.numpy as jnp
from jax import lax
from jax.experimental import pallas as pl
from jax.experimental.pallas import tpu as pltpu

D_MODEL = 2048
DEPTH = 2
N_MIXERS = 2

WINDOWS = (128, 512, 2048)
DILATIONS = (1, 4, 16)
N_GROUPS = 3
HEAD_DIM = 128
HEADS = 8
ATTN_WIDTH = HEADS * HEAD_DIM
N_KEYS = 129
KEY_BLOCK = 128
N_BUCKETS = 32
MAX_DISTANCE = 2048
NEG_INF = -1e30

RET_HEADS = 8
RET_DK = 256
RET_DV = 512
RET_QK_WIDTH = RET_HEADS * RET_DK
RET_V_WIDTH = RET_HEADS * RET_DV
RET_CHUNK = 128
RET_HEAD_GROUP = 2
ROT_BASE = 10000.0

FFN_DIM = 5504
FFN_TILE = 128
N_FFN_TILES = FFN_DIM // FFN_TILE
NORM_EPS = 1e-6

F32 = jnp.float32
MXU_DTYPE = jnp.bfloat16
VMEM_LIMIT_BYTES = 56 * 1024 * 1024


def _params(*semantics):
    return pltpu.CompilerParams(dimension_semantics=semantics, vmem_limit_bytes=VMEM_LIMIT_BYTES)


def _silu(x):
    return x * jax.nn.sigmoid(x)


def _ada_kernel(c_ref, w_ref, b_ref, o_ref):
    a = _silu(c_ref[...]).astype(MXU_DTYPE)
    w = w_ref[...].astype(MXU_DTYPE)
    o_ref[...] = jnp.dot(a, w, preferred_element_type=F32) + b_ref[...]


def _ada_modulation(c, w_ada, b_ada):
    rows = c.shape[0]
    tn = 1024
    return pl.pallas_call(
        _ada_kernel,
        grid=(DEPTH, 6 * D_MODEL // tn),
        in_specs=[pl.BlockSpec((rows, D_MODEL), lambda l, j: (0, 0)),
                  pl.BlockSpec((None, D_MODEL, tn), lambda l, j: (l, 0, j)),
                  pl.BlockSpec((None, 1, tn), lambda l, j: (l, 0, j))],
        out_specs=pl.BlockSpec((None, rows, tn), lambda l, j: (l, 0, j)),
        out_shape=jax.ShapeDtypeStruct((DEPTH, rows, 6 * D_MODEL), F32),
        compiler_params=_params("arbitrary", "arbitrary"),
        name="ada_modulation",
    )(c, w_ada, b_ada.reshape(DEPTH, 1, 6 * D_MODEL))


def _norm_mod_kernel(x_ref, g_ref, shift_ref, scale_ref, o_ref):
    x = x_ref[...]
    y = x * lax.rsqrt(jnp.mean(x * x, axis=-1, keepdims=True) + NORM_EPS) * g_ref[...]
    o_ref[...] = (y * (1.0 + scale_ref[...]) + shift_ref[...]).astype(o_ref.dtype)


def _mod_spec(mods, tm, col, n_lead):
    per_row = mods.shape[1] > 1
    rb = tm if per_row else 1

    def index(*ids):
        b, i = ids[n_lead], ids[n_lead + 1]
        return (b, i if per_row else 0, col)

    return pl.BlockSpec((None, rb, D_MODEL), index)


def _norm_mod(x, gain, mods, shift_col, scale_col, tm):
    bx, t, d = x.shape
    return pl.pallas_call(
        _norm_mod_kernel,
        grid=(bx, t // tm),
        in_specs=[pl.BlockSpec((None, tm, d), lambda b, i: (b, i, 0)),
                  pl.BlockSpec((1, d), lambda b, i: (0, 0)),
                  _mod_spec(mods, tm, shift_col, 0),
                  _mod_spec(mods, tm, scale_col, 0)],
        out_specs=pl.BlockSpec((None, tm, d), lambda b, i: (b, i, 0)),
        out_shape=jax.ShapeDtypeStruct((bx, t, d), MXU_DTYPE),
        compiler_params=_params("arbitrary", "arbitrary"),
        name="norm_mod",
    )(x, gain.reshape(1, d), mods, mods)


def _cast_weight_once(w_ref, wbf_ref):
    @pl.when((pl.program_id(1) == 0) & (pl.program_id(2) == 0))
    def _():
        wbf_ref[...] = w_ref[...].astype(wbf_ref.dtype)


def _matmul_kernel(a_ref, w_ref, o_ref, wbf_ref):
    _cast_weight_once(w_ref, wbf_ref)
    o_ref[...] = jnp.dot(a_ref[...], wbf_ref[...], preferred_element_type=F32).astype(o_ref.dtype)


def _matmul(a, w, tm, tn, out_dtype=F32):
    bx, t, k = a.shape
    n = w.shape[1]
    return pl.pallas_call(
        _matmul_kernel,
        grid=(n // tn, bx, t // tm),
        in_specs=[pl.BlockSpec((None, tm, k), lambda j, b, i: (b, i, 0)),
                  pl.BlockSpec((k, tn), lambda j, b, i: (0, j))],
        out_specs=pl.BlockSpec((None, tm, tn), lambda j, b, i: (b, i, j)),
        out_shape=jax.ShapeDtypeStruct((bx, t, n), out_dtype),
        scratch_shapes=[pltpu.VMEM((k, tn), MXU_DTYPE)],
        compiler_params=_params("arbitrary", "arbitrary", "arbitrary"),
        name="matmul",
    )(a, w)


def _resid_matmul_kernel(a_ref, w_ref, x_ref, gate_ref, o_ref, wbf_ref):
    _cast_weight_once(w_ref, wbf_ref)
    acc = jnp.dot(a_ref[...], wbf_ref[...], preferred_element_type=F32)
    o_ref[...] = x_ref[...] + gate_ref[...] * acc


def _resid_matmul(a, w, x, mods, gate_col, tm, tn):
    bx, t, k = a.shape
    n = w.shape[1]
    cols_per_group = D_MODEL // tn
    per_row = mods.shape[1] > 1
    rb = tm if per_row else 1
    gate_spec = pl.BlockSpec(
        (None, rb, tn), lambda j, b, i: (b, i if per_row else 0, gate_col * cols_per_group + j))
    return pl.pallas_call(
        _resid_matmul_kernel,
        grid=(n // tn, bx, t // tm),
        in_specs=[pl.BlockSpec((None, tm, k), lambda j, b, i: (b, i, 0)),
                  pl.BlockSpec((k, tn), lambda j, b, i: (0, j)),
                  pl.BlockSpec((None, tm, tn), lambda j, b, i: (b, i, j)),
                  gate_spec],
        out_specs=pl.BlockSpec((None, tm, tn), lambda j, b, i: (b, i, j)),
        out_shape=jax.ShapeDtypeStruct((bx, t, n), F32),
        scratch_shapes=[pltpu.VMEM((k, tn), MXU_DTYPE)],
        compiler_params=_params("arbitrary", "arbitrary", "arbitrary"),
        name="resid_matmul",
    )(a, w, x, mods)


def _qkv_kernel(a_ref, w_ref, gain_ref, o_ref, wbf_ref, y_ref, *, dilation, tm):
    _cast_weight_once(w_ref, wbf_ref)
    acc = jnp.dot(a_ref[...], wbf_ref[...], preferred_element_type=F32)
    part = pl.program_id(0)
    gain = gain_ref[...]

    @pl.when(part < 2)
    def _():
        for h in range(HEADS):
            xh = acc[:, h * HEAD_DIM:(h + 1) * HEAD_DIM]
            y_ref[h] = xh * lax.rsqrt(jnp.mean(xh * xh, axis=-1, keepdims=True) + NORM_EPS) * gain

    @pl.when(part == 2)
    def _():
        for h in range(HEADS):
            y_ref[h] = acc[:, h * HEAD_DIM:(h + 1) * HEAD_DIM]

    rows = tm // dilation
    for c in range(dilation):
        for h in range(HEADS):
            if dilation == 1:
                o_ref[c, h] = y_ref[h]
            else:
                o_ref[c, h] = y_ref[h, pl.ds(c, rows, stride=dilation), :]


def _qkv_project(h, w_qkv, gains, group, dilation, tm):
    bx, t, d = h.shape
    td = t // dilation
    kern = functools.partial(_qkv_kernel, dilation=dilation, tm=tm)
    return pl.pallas_call(
        kern,
        grid=(3, bx, t // tm),
        in_specs=[pl.BlockSpec((None, tm, d), lambda p, b, i: (b, i, 0)),
                  pl.BlockSpec((d, ATTN_WIDTH), lambda p, b, i: (0, group * 3 + p)),
                  pl.BlockSpec((None, 1, HEAD_DIM), lambda p, b, i: (p, 0, 0))],
        out_specs=pl.BlockSpec((None, None, dilation, HEADS, tm // dilation, HEAD_DIM),
                               lambda p, b, i: (b, p, 0, 0, i, 0)),
        out_shape=jax.ShapeDtypeStruct((bx, 3, dilation, HEADS, td, HEAD_DIM), F32),
        scratch_shapes=[pltpu.VMEM((d, ATTN_WIDTH), MXU_DTYPE), pltpu.VMEM((HEADS, tm, HEAD_DIM), F32)],
        compiler_params=_params("arbitrary", "arbitrary", "arbitrary"),
        name=f"qkv_project_g{group}",
    )(h, w_qkv, gains)


def _t5_causal_bucket(dist):
    max_exact = N_BUCKETS // 2
    d = jnp.maximum(dist, 1).astype(F32)
    large = max_exact + (jnp.log(d / max_exact) / math.log(MAX_DISTANCE / max_exact)
                         * (N_BUCKETS - max_exact)).astype(jnp.int32)
    return jnp.where(dist < max_exact, dist, jnp.minimum(large, N_BUCKETS - 1))


def _bucket_lookup(rel_bias, g, bucket, fill):
    cols = rel_bias[:, g * HEADS:(g + 1) * HEADS].astype(F32)
    expand = (slice(None),) + (None,) * bucket.ndim
    out = jnp.full((HEADS,) + bucket.shape, fill, F32)
    for b in range(N_BUCKETS):
        out = jnp.where(bucket[None] == b, cols[b][expand], out)
    return out


def _group_bias(rel_bias, g):
    dist = DILATIONS[g] * jnp.arange(N_KEYS, dtype=jnp.int32)
    return _bucket_lookup(rel_bias, g, _t5_causal_bucket(dist), 0.0)


def _band_bias(rel_bias, g):
    a = jnp.arange(KEY_BLOCK, dtype=jnp.int32)[:, None]
    c = jnp.arange(2 * KEY_BLOCK, dtype=jnp.int32)[None, :]
    rel = a + KEY_BLOCK - c
    valid = (rel >= 0) & (rel <= KEY_BLOCK)
    bucket = jnp.where(valid, _t5_causal_bucket(DILATIONS[g] * jnp.clip(rel, 0, KEY_BLOCK)), -1)
    return _bucket_lookup(rel_bias, g, bucket, NEG_INF)


def _window_attn_kernel(q_ref, kp_ref, kc_ref, vp_ref, vc_ref, bias_ref, o_ref, lse_ref,
                        *, dilation, heads):
    i = pl.program_id(1)
    c = pl.program_id(3)
    col = lax.broadcasted_iota(jnp.int32, (KEY_BLOCK, 2 * KEY_BLOCK), 1)
    no_prev = (i == 0) & (col < KEY_BLOCK)
    for h in range(heads):
        q = q_ref[h].astype(MXU_DTYPE)
        k = jnp.concatenate([kp_ref[h], kc_ref[h]], axis=0).astype(MXU_DTYPE)
        v = jnp.concatenate([vp_ref[h], vc_ref[h]], axis=0).astype(MXU_DTYPE)
        s = lax.dot_general(q, k, (((1,), (1,)), ((), ())), preferred_element_type=F32)
        s = s * (HEAD_DIM ** -0.5) + bias_ref[h]
        s = jnp.where(no_prev, NEG_INF, s)
        m = jnp.max(s, axis=-1, keepdims=True)
        p = jnp.exp(s - m)
        l = jnp.sum(p, axis=-1, keepdims=True)
        o = jnp.dot(p.astype(MXU_DTYPE), v, preferred_element_type=F32) / l
        lse = jnp.broadcast_to(m + jnp.log(l), (KEY_BLOCK, HEAD_DIM))
        if dilation == 1:
            o_ref[h] = o
            lse_ref[h] = lse
        else:
            o_ref[h, pl.ds(c, KEY_BLOCK, stride=dilation), :] = o
            lse_ref[h, pl.ds(c, KEY_BLOCK, stride=dilation), :] = lse


def _window_attention(qkv, band_bias, dilation):
    bx, _, _, _, td, _ = qkv.shape
    t = td * dilation
    nblk = td // KEY_BLOCK
    hb = 1
    heads = HEADS // hb

    def spec(part, prev):
        def index(b, i, hq, c):
            return (b, part, c, hq, jnp.maximum(i - 1, 0) if prev else i, 0)
        return pl.BlockSpec((None, None, None, heads, KEY_BLOCK, HEAD_DIM), index)

    out_spec = pl.BlockSpec((None, heads, KEY_BLOCK * dilation, HEAD_DIM), lambda b, i, hq, c: (b, hq, i, 0))
    kern = functools.partial(_window_attn_kernel, dilation=dilation, heads=heads)
    return pl.pallas_call(
        kern,
        grid=(bx, nblk, hb, dilation),
        in_specs=[spec(0, False), spec(1, True), spec(1, False), spec(2, True), spec(2, False),
                  pl.BlockSpec((heads, KEY_BLOCK, 2 * KEY_BLOCK), lambda b, i, hq, c: (hq, 0, 0))],
        out_specs=[out_spec, out_spec],
        out_shape=[jax.ShapeDtypeStruct((bx, HEADS, t, HEAD_DIM), F32)] * 2,
        compiler_params=_params("arbitrary", "arbitrary", "arbitrary", "arbitrary"),
        name=f"window_attention_d{dilation}",
    )(qkv, qkv, qkv, qkv, qkv, band_bias)


STEP_ROWS = 8
STEP_SEQS = 8


def _step_attn_kernel(*refs):
    q_ref, kn_ref, vn_ref = refs[:3]
    kv_refs = refs[3:3 + STEP_ROWS]
    bias0_ref, bias_ref, o_ref, lse_ref, m_ref, l_ref, acc_ref = refs[3 + STEP_ROWS:]
    j = pl.program_id(1)
    scale = HEAD_DIM ** -0.5

    def chunk(ci, carry):
        sl = pl.ds(pl.multiple_of(ci * STEP_SEQS, STEP_SEQS), STEP_SEQS)
        q = q_ref[sl]

        @pl.when(j == 0)
        def _():
            s0 = jnp.sum(q * kn_ref[sl], axis=-1, keepdims=True) * scale + bias0_ref[...]
            m_ref[sl] = jnp.broadcast_to(s0, q.shape)
            l_ref[sl] = jnp.ones(q.shape, F32)
            acc_ref[sl] = vn_ref[sl]

        scores = [jnp.sum(q * kv_refs[r][sl, 0], axis=-1, keepdims=True) * scale + bias_ref[r]
                  for r in range(STEP_ROWS)]
        m_old = m_ref[sl]
        m_new = m_old
        for s in scores:
            m_new = jnp.maximum(m_new, s)
        alpha = jnp.exp(m_old - m_new)
        l_new = alpha * l_ref[sl]
        acc = alpha * acc_ref[sl]
        for r in range(STEP_ROWS):
            p = jnp.exp(scores[r] - m_new)
            l_new = l_new + p
            acc = acc + p * kv_refs[r][sl, 1]
        m_ref[sl] = m_new
        l_ref[sl] = l_new
        acc_ref[sl] = acc
        return carry

    n_chunks = q_ref.shape[0] // STEP_SEQS
    lax.fori_loop(0, n_chunks, chunk, 0, unroll=2 if n_chunks % 2 == 0 else 1)

    @pl.when(j == pl.num_programs(1) - 1)
    def _():
        o_ref[...] = acc_ref[...] / l_ref[...]
        lse_ref[...] = m_ref[...] + jnp.log(l_ref[...])


def _step_attention(q, k_new, v_new, cache, layer, gb, dilation, bt):
    b = q.shape[0]
    nk = N_KEYS - 1
    lanes = jnp.broadcast_to(gb.T[:, :, None], (N_KEYS, HEADS, HEAD_DIM))
    bias0 = lanes[0]
    bias_rows = lanes[:0:-1]
    qspec = pl.BlockSpec((bt, HEADS, HEAD_DIM), lambda bi, j: (bi, 0, 0))

    def row_spec(r):
        return pl.BlockSpec((None, bt, None, 2, HEADS, HEAD_DIM),
                            lambda bi, j: (layer, bi, (j * STEP_ROWS + r) * dilation, 0, 0, 0))

    return pl.pallas_call(
        _step_attn_kernel,
        grid=(b // bt, nk // STEP_ROWS),
        in_specs=[qspec, qspec, qspec] + [row_spec(r) for r in range(STEP_ROWS)] + [
            pl.BlockSpec((HEADS, HEAD_DIM), lambda bi, j: (0, 0)),
            pl.BlockSpec((STEP_ROWS, HEADS, HEAD_DIM), lambda bi, j: (j, 0, 0))],
        out_specs=[qspec, qspec],
        out_shape=[jax.ShapeDtypeStruct((b, HEADS, HEAD_DIM), F32)] * 2,
        scratch_shapes=[pltpu.VMEM((bt, HEADS, HEAD_DIM), F32)] * 3,
        compiler_params=_params("arbitrary", "arbitrary"),
        name=f"step_attention_d{dilation}",
    )(q, k_new, v_new, *([cache] * STEP_ROWS), bias0, bias_rows)


def _attn_out_kernel(o0_ref, o1_ref, o2_ref, l0_ref, l1_ref, l2_ref, w_ref, x_ref, gate_ref, out_ref,
                     wbf_ref):
    _cast_weight_once(w_ref, wbf_ref)
    heads = []
    for h in range(HEADS):
        l0, l1, l2 = l0_ref[h], l1_ref[h], l2_ref[h]
        m = jnp.maximum(jnp.maximum(l0, l1), l2)
        e0, e1, e2 = jnp.exp(l0 - m), jnp.exp(l1 - m), jnp.exp(l2 - m)
        tot = e0 + e1 + e2
        merged = (e0 / tot) * o0_ref[h] + (e1 / tot) * o1_ref[h] + (e2 / tot) * o2_ref[h]
        heads.append(merged.astype(MXU_DTYPE))
    acc = jnp.dot(jnp.concatenate(heads, axis=-1), wbf_ref[...], preferred_element_type=F32)
    out_ref[...] = x_ref[...] + gate_ref[...] * acc


def _attn_out(outs, lses, w_o, x, mods, gate_col, tm):
    bx, t, d = x.shape
    per_row = mods.shape[1] > 1
    rb = tm if per_row else 1
    aspec = pl.BlockSpec((None, HEADS, tm, HEAD_DIM), lambda j, b, i: (b, 0, i, 0))
    xspec = pl.BlockSpec((None, tm, d), lambda j, b, i: (b, i, 0))
    return pl.pallas_call(
        _attn_out_kernel,
        grid=(1, bx, t // tm),
        in_specs=[aspec] * 6 + [
            pl.BlockSpec((ATTN_WIDTH, d), lambda j, b, i: (0, 0)),
            xspec,
            pl.BlockSpec((None, rb, d), lambda j, b, i: (b, i if per_row else 0, gate_col))],
        out_specs=xspec,
        out_shape=jax.ShapeDtypeStruct((bx, t, d), F32),
        scratch_shapes=[pltpu.VMEM((ATTN_WIDTH, d), MXU_DTYPE)],
        compiler_params=_params("arbitrary", "arbitrary", "arbitrary"),
        name="attn_out",
    )(*outs, *lses, w_o, x, mods)


KV_ROWS = 2 * HEADS
SHIFT_BUFFERS = 3
SHIFT_CHUNK_BYTES = 8 * 1024 * 1024


def _shift_chunking(b, keep_rows):
    row_bytes = HEAD_DIM * 4
    pieces = 1
    while (keep_rows // pieces) * row_bytes > SHIFT_CHUNK_BYTES or keep_rows % (8 * pieces):
        pieces += 1
    rows = keep_rows // pieces
    seqs = 1
    while seqs * 2 * rows * row_bytes <= SHIFT_CHUNK_BYTES // 2 and b % (seqs * 2) == 0:
        seqs *= 2
    return seqs, rows, pieces


def _cache_shift_kernel(prev_ref, new_ref, out_ref, buf_ref, in_sems, out_sems, new_sem,
                        *, layer, seqs, rows, pieces):
    b, total_rows, _ = out_ref.shape
    n_chunks = (b // seqs) * pieces

    def load(k, slot):
        block, piece = k // pieces, k % pieces
        src = prev_ref.at[pl.ds(layer * b + block * seqs, seqs), pl.ds(KV_ROWS + piece * rows, rows)]
        return pltpu.make_async_copy(src, buf_ref.at[slot], in_sems.at[slot])

    def store(k, slot):
        block, piece = k // pieces, k % pieces
        dst = out_ref.at[pl.ds(block * seqs, seqs), pl.ds(piece * rows, rows)]
        return pltpu.make_async_copy(buf_ref.at[slot], dst, out_sems.at[slot])

    new_copy = pltpu.make_async_copy(new_ref, out_ref.at[:, pl.ds(total_rows - KV_ROWS, KV_ROWS)], new_sem)
    new_copy.start()
    load(0, 0).start()

    def step(k, carry):
        slot = k % SHIFT_BUFFERS
        nxt = k + 1
        nxt_slot = nxt % SHIFT_BUFFERS

        @pl.when(nxt < n_chunks)
        def _():
            @pl.when(nxt >= SHIFT_BUFFERS)
            def _():
                store(nxt - SHIFT_BUFFERS, nxt_slot).wait()
            load(nxt, nxt_slot).start()

        load(k, slot).wait()
        store(k, slot).start()
        return carry

    lax.fori_loop(0, n_chunks, step, 0)
    for k in range(max(0, n_chunks - SHIFT_BUFFERS), n_chunks):
        store(k, k % SHIFT_BUFFERS).wait()
    new_copy.wait()


def _cache_shift(prev, layer, k_new, v_new):
    nl, b, w = prev.shape[:3]
    prev_rows = prev.reshape(nl * b, w * KV_ROWS, HEAD_DIM)
    new_rows = jnp.concatenate([k_new, v_new], axis=1)
    seqs, rows, pieces = _shift_chunking(b, (w - 1) * KV_ROWS)
    out = pl.pallas_call(
        functools.partial(_cache_shift_kernel, layer=layer, seqs=seqs, rows=rows, pieces=pieces),
        in_specs=[pl.BlockSpec(memory_space=pl.ANY),
                  pl.BlockSpec(memory_space=pltpu.VMEM)],
        out_specs=pl.BlockSpec(memory_space=pl.ANY),
        out_shape=jax.ShapeDtypeStruct((b, w * KV_ROWS, HEAD_DIM), prev.dtype),
        scratch_shapes=[pltpu.VMEM((SHIFT_BUFFERS, seqs, rows, HEAD_DIM), prev.dtype),
                        pltpu.SemaphoreType.DMA((SHIFT_BUFFERS,)),
                        pltpu.SemaphoreType.DMA((SHIFT_BUFFERS,)),
                        pltpu.SemaphoreType.DMA(())],
        compiler_params=pltpu.CompilerParams(vmem_limit_bytes=VMEM_LIMIT_BYTES),
        name="cache_shift",
    )(prev_rows, new_rows)
    return out.reshape(b, w, 2, HEADS, HEAD_DIM)


def _rotation_tables(positions):
    half = RET_DK // 2
    inv_freq = 1.0 / (ROT_BASE ** jnp.linspace(0.0, 1.0, half, dtype=F32))
    ang = positions.astype(F32)[:, None] * inv_freq[None, :]
    cos, sin = jnp.cos(ang), jnp.sin(ang)
    cos_rep = jnp.stack([cos, cos], axis=-1).reshape(-1, RET_DK)
    sin_signed = jnp.stack([-sin, sin], axis=-1).reshape(-1, RET_DK)
    return cos_rep, sin_signed


def _rotate_pairs(x, cos_rep, sin_signed):
    lanes = 128
    even = lax.broadcasted_iota(jnp.int32, (x.shape[0], lanes), 1) % 2 == 0
    parts = []
    for s in range(x.shape[1] // lanes):
        xs = x[:, s * lanes:(s + 1) * lanes]
        parts.append(jnp.where(even, pltpu.roll(xs, lanes - 1, 1), pltpu.roll(xs, 1, 1)))
    swapped = jnp.concatenate(parts, axis=-1)
    return x * cos_rep + swapped * sin_signed


def _log_gamma():
    return jnp.log1p(-jnp.exp2(-5.0 - jnp.arange(RET_HEADS, dtype=F32)))


def _group_norm_gate(o, gain, gate):
    mu = jnp.mean(o, axis=-1, keepdims=True)
    var = jnp.mean(jnp.square(o - mu), axis=-1, keepdims=True)
    return _silu(gate) * ((o - mu) * lax.rsqrt(var + NORM_EPS) * gain)


def _retention_kernel(q_ref, k_ref, v_ref, gate_ref, cos_ref, sin_ref, decay_ref, qdec_ref, kdec_ref,
                      cdec_ref, gain_ref, y_ref, s_out_ref, s_ref):
    c = pl.program_id(1)
    nb, hg = s_ref.shape[:2]

    @pl.when(c == 0)
    def _():
        s_ref[...] = jnp.zeros(s_ref.shape, F32)

    cos, sin = cos_ref[...], sin_ref[...]
    for b in range(nb):
        for hh in range(hg):
            ksl = slice(hh * RET_DK, (hh + 1) * RET_DK)
            vsl = slice(hh * RET_DV, (hh + 1) * RET_DV)
            q = _rotate_pairs(q_ref[b, :, ksl], cos, sin)
            k = _rotate_pairs(k_ref[b, :, ksl], cos, sin) * (RET_DK ** -0.5)
            v = v_ref[b, :, vsl].astype(MXU_DTYPE)
            state = s_ref[b, hh]
            scores = lax.dot_general(q.astype(MXU_DTYPE), k.astype(MXU_DTYPE), (((1,), (1,)), ((), ())),
                                     preferred_element_type=F32) * decay_ref[hh]
            o = jnp.dot(scores.astype(MXU_DTYPE), v, preferred_element_type=F32)
            o = o + jnp.dot((q * qdec_ref[hh]).astype(MXU_DTYPE), state.astype(MXU_DTYPE),
                            preferred_element_type=F32)
            kd_t = jnp.transpose(k * kdec_ref[hh]).astype(MXU_DTYPE)
            s_ref[b, hh] = cdec_ref[hh] * state + jnp.dot(kd_t, v, preferred_element_type=F32)
            y_ref[b, :, vsl] = _group_norm_gate(o, gain_ref[hh], gate_ref[b, :, vsl]).astype(y_ref.dtype)

    @pl.when(c == pl.num_programs(1) - 1)
    def _():
        s_out_ref[...] = s_ref[...]


def _retention_sequence(proj, cos_rep, sin_signed, gn_gain):
    bx, t, _ = proj.shape
    cw = RET_CHUNK
    lg = _log_gamma()
    pos = jnp.arange(cw, dtype=F32)
    diff = pos[:, None] - pos[None, :]
    decay = jnp.where(diff >= 0, jnp.exp(diff[None] * lg[:, None, None]), 0.0)
    q_decay = jnp.exp((pos[:, None] + 1.0) * lg[None, :]).T[:, :, None]
    k_decay = jnp.exp((cw - 1.0 - pos)[:, None] * lg[None, :]).T[:, :, None]
    chunk_decay = jnp.exp(cw * lg).reshape(RET_HEADS, 1, 1)
    hg = RET_HEAD_GROUP
    nqk = RET_QK_WIDTH // (hg * RET_DK)
    nv0 = 2 * RET_QK_WIDTH // (hg * RET_DV)
    ng0 = nv0 + RET_HEADS // hg
    return pl.pallas_call(
        _retention_kernel,
        grid=(RET_HEADS // hg, t // cw),
        in_specs=[pl.BlockSpec((bx, cw, hg * RET_DK), lambda h, c: (0, c, h)),
                  pl.BlockSpec((bx, cw, hg * RET_DK), lambda h, c: (0, c, nqk + h)),
                  pl.BlockSpec((bx, cw, hg * RET_DV), lambda h, c: (0, c, nv0 + h)),
                  pl.BlockSpec((bx, cw, hg * RET_DV), lambda h, c: (0, c, ng0 + h)),
                  pl.BlockSpec((cw, RET_DK), lambda h, c: (c, 0)),
                  pl.BlockSpec((cw, RET_DK), lambda h, c: (c, 0)),
                  pl.BlockSpec((hg, cw, cw), lambda h, c: (h, 0, 0)),
                  pl.BlockSpec((hg, cw, 1), lambda h, c: (h, 0, 0)),
                  pl.BlockSpec((hg, cw, 1), lambda h, c: (h, 0, 0)),
                  pl.BlockSpec((hg, 1, 1), lambda h, c: (h, 0, 0)),
                  pl.BlockSpec((hg, 1, RET_DV), lambda h, c: (h, 0, 0))],
        out_specs=[pl.BlockSpec((bx, cw, hg * RET_DV), lambda h, c: (0, c, h)),
                   pl.BlockSpec((bx, hg, RET_DK, RET_DV), lambda h, c: (0, h, 0, 0))],
        out_shape=[jax.ShapeDtypeStruct((bx, t, RET_V_WIDTH), MXU_DTYPE),
                   jax.ShapeDtypeStruct((bx, RET_HEADS, RET_DK, RET_DV), F32)],
        scratch_shapes=[pltpu.VMEM((bx, hg, RET_DK, RET_DV), F32)],
        compiler_params=_params("arbitrary", "arbitrary"),
        name="retention_sequence",
    )(proj, proj, proj, proj, cos_rep, sin_signed, decay, q_decay, k_decay, chunk_decay,
      gn_gain.reshape(RET_HEADS, 1, RET_DV))


def _retention_step_kernel(q_ref, k_ref, v_ref, gate_ref, cos_ref, sin_ref, gamma_ref, gain_ref, s_ref,
                           y_ref, s_out_ref):
    cos, sin = cos_ref[...], sin_ref[...]
    q = _rotate_pairs(q_ref[...], cos, sin)
    k = _rotate_pairs(k_ref[...], cos, sin) * (RET_DK ** -0.5)
    qk = jnp.sum(q * k, axis=-1, keepdims=True)
    q_t = jnp.transpose(q * gamma_ref[...])
    k_t = jnp.transpose(k)
    v = v_ref[...]
    gamma = gamma_ref[...]
    rows = []
    for h in range(RET_HEADS):
        state = s_ref[h]
        vh = v[h:h + 1, :]
        rows.append(qk[h:h + 1, :] * vh + jnp.sum(q_t[:, h:h + 1] * state, axis=0, keepdims=True))
        s_out_ref[h] = gamma[h:h + 1, :] * state + k_t[:, h:h + 1] * vh
    o = jnp.concatenate(rows, axis=0)
    y_ref[...] = _group_norm_gate(o, gain_ref[...], gate_ref[...]).astype(y_ref.dtype)


def _retention_step(q, k, v, gate, state, layer, cos_rep, sin_signed, gn_gain):
    b = q.shape[0]
    gamma = jnp.exp(_log_gamma()).reshape(RET_HEADS, 1)
    qspec = pl.BlockSpec((None, RET_HEADS, RET_DK), lambda i: (i, 0, 0))
    vspec = pl.BlockSpec((None, RET_HEADS, RET_DV), lambda i: (i, 0, 0))
    sspec = pl.BlockSpec((None, RET_HEADS, RET_DK, RET_DV), lambda i: (i, 0, 0, 0))
    sspec_in = pl.BlockSpec((None, None, RET_HEADS, RET_DK, RET_DV), lambda i: (layer, i, 0, 0, 0))
    return pl.pallas_call(
        _retention_step_kernel,
        grid=(b,),
        in_specs=[qspec, qspec, vspec, vspec,
                  pl.BlockSpec((1, RET_DK), lambda i: (0, 0)),
                  pl.BlockSpec((1, RET_DK), lambda i: (0, 0)),
                  pl.BlockSpec((RET_HEADS, 1), lambda i: (0, 0)),
                  pl.BlockSpec((RET_HEADS, RET_DV), lambda i: (0, 0)),
                  sspec_in],
        out_specs=[vspec, sspec],
        out_shape=[jax.ShapeDtypeStruct((b, RET_HEADS, RET_DV), MXU_DTYPE),
                   jax.ShapeDtypeStruct(state.shape[1:], F32)],
        compiler_params=_params("arbitrary"),
        name="retention_step",
    )(q, k, v, gate, cos_rep, sin_signed, gamma, gn_gain.reshape(RET_HEADS, RET_DV), state)


SIDE_POINTS = 2
SIDE_CHUNK_BYTES = 512 * 1024
SIDE_DEEP_BYTES = 10 * 1024 * 1024


class _SideCopy:
    def __init__(self, n_seqs, keep_rows, src_seq0, n_steps, deep):
        assert n_seqs + 1 < n_steps
        row_bytes = HEAD_DIM * 4
        pieces = SIDE_POINTS
        while keep_rows % (8 * pieces) or (keep_rows // pieces) * row_bytes > SIDE_CHUNK_BYTES:
            pieces += SIDE_POINTS
        self.rows = keep_rows // pieces
        self.per_point = pieces // SIDE_POINTS
        self.n_seqs, self.src_seq0, self.deep = n_seqs, src_seq0, deep

    def scratch_shapes(self):
        n = SIDE_POINTS * self.per_point * (2 if self.deep else 1)
        return [pltpu.VMEM((n, self.rows, HEAD_DIM), F32),
                pltpu.SemaphoreType.DMA((n,)), pltpu.SemaphoreType.DMA((n,)), pltpu.SemaphoreType.DMA(())]

    def bind(self, src_ref, new_ref, dst_ref, buf_ref, load_sems, store_sems, new_sem):
        self.refs = (src_ref, dst_ref, buf_ref, load_sems, store_sems)
        self.new = (new_ref, new_sem)

    def _slot(self, seq, piece):
        return piece + (seq % 2) * SIDE_POINTS * self.per_point if self.deep else piece

    def _load(self, seq, point, q):
        src_ref, _, buf_ref, load_sems, _ = self.refs
        piece = point * self.per_point + q
        slot = self._slot(seq, piece)
        src = src_ref.at[self.src_seq0 + seq, pl.ds(KV_ROWS + piece * self.rows, self.rows)]
        return pltpu.make_async_copy(src, buf_ref.at[slot], load_sems.at[slot])

    def _store(self, seq, point, q):
        _, dst_ref, buf_ref, _, store_sems = self.refs
        piece = point * self.per_point + q
        slot = self._slot(seq, piece)
        dst = dst_ref.at[seq, pl.ds(piece * self.rows, self.rows)]
        return pltpu.make_async_copy(buf_ref.at[slot], dst, store_sems.at[slot])

    def serve(self, step, point):
        if self.deep:
            out_step, out_point = step - 1, point
        else:
            out_step, out_point = (step if point > 0 else step - 1), (point - 1) % SIDE_POINTS
        retire_step = out_step - 1 if self.deep else step - 1

        @pl.when((out_step >= 0) & (out_step < self.n_seqs))
        def _():
            for q in range(self.per_point):
                self._load(out_step, out_point, q).wait()
                self._store(out_step, out_point, q).start()

        @pl.when((retire_step >= 0) & (retire_step < self.n_seqs))
        def _():
            for q in range(self.per_point):
                self._store(retire_step, point, q).wait()

        @pl.when(step < self.n_seqs)
        def _():
            for q in range(self.per_point):
                self._load(step, point, q).start()

    def append_new_rows(self):
        new_ref, new_sem = self.new
        dst_ref = self.refs[1]
        total_rows = dst_ref.shape[1]
        copy = pltpu.make_async_copy(new_ref, dst_ref.at[:, pl.ds(total_rows - KV_ROWS, KV_ROWS)], new_sem)
        copy.start()
        copy.wait()


def _ffn_up_seq_kernel(*refs, t, chunk, parts, sides):
    a_ref, wg_ref, wv_ref, cwg_ref, cwv_ref, cbg_ref, cbv_ref = refs[:7]
    ns = len(sides)
    side_in = refs[7:7 + 2 * ns]
    g_ref, rows_g_ref, rows_v_ref = refs[7 + 2 * ns:10 + 2 * ns]
    side_out = refs[10 + 2 * ns:10 + 3 * ns]
    wbf_ref, u_ref = refs[10 + 3 * ns:12 + 3 * ns]
    side_scratch = refs[12 + 3 * ns:]
    for i, side in enumerate(sides):
        side.bind(side_in[2 * i], side_in[2 * i + 1], side_out[i], *side_scratch[4 * i:4 * i + 4])
    part_id = pl.program_id(2)
    step = (pl.program_id(0) * pl.num_programs(1) + pl.program_id(1)) * parts + part_id
    n_steps = pl.num_programs(0) * pl.num_programs(1) * parts

    @pl.when(part_id == 0)
    def _():
        wbf_ref[:, :FFN_TILE] = wg_ref[...].astype(wbf_ref.dtype)
        wbf_ref[:, FFN_TILE:] = wv_ref[...].astype(wbf_ref.dtype)
        u_ref[0:8, :] = jnp.zeros((8, 2 * FFN_TILE), F32)

    cw = jnp.concatenate([cwg_ref[...], cwv_ref[...]], axis=-1)
    cb = jnp.concatenate([cbg_ref[...], cbv_ref[...]], axis=-1)
    part_rows = t // parts
    n_chunks = part_rows // chunk
    stride = max(n_chunks // SIDE_POINTS, 1)

    def rows_of_part(part):
        for s in range(n_chunks):
            r0 = part * part_rows + s * chunk
            o0 = s * chunk
            if s % stride == 0 and s // stride < SIDE_POINTS:
                for side in sides:
                    side.serve(step, s // stride)
            u = jnp.dot(a_ref[r0:r0 + chunk, :], wbf_ref[...], preferred_element_type=F32)
            u_ref[8 + r0:8 + r0 + chunk, :] = u
            z = cb + cw[0:1] * u_ref[6 + r0:6 + r0 + chunk, :]
            z = z + cw[1:2] * u_ref[7 + r0:7 + r0 + chunk, :]
            z = z + cw[2:3] * u
            g_ref[o0:o0 + chunk, :] = (_silu(z[:, :FFN_TILE]) * z[:, FFN_TILE:]).astype(g_ref.dtype)

    for part in range(parts):
        pl.when(part_id == part)(functools.partial(rows_of_part, part))

    @pl.when(part_id == parts - 1)
    def _():
        rows_g_ref[...] = u_ref[6 + t:8 + t, :FFN_TILE]
        rows_v_ref[...] = u_ref[6 + t:8 + t, FFN_TILE:]

    if sides:
        @pl.when(step == n_steps - 1)
        def _():
            for side in sides:
                side.append_new_rows()


def _ffn_up_sequence(h, w_up, conv_w, conv_b, shift=None):
    bx, t, d = h.shape
    nt = N_FFN_TILES
    parts = 2 if (shift is not None and t % 1024 == 0) else 1
    chunk = min(t // parts, 512)
    conv_b = conv_b.reshape(1, 2 * FFN_DIM)
    in_specs = [pl.BlockSpec((None, t, d), lambda b, j, p: (b, 0, 0)),
                pl.BlockSpec((d, FFN_TILE), lambda b, j, p: (0, j)),
                pl.BlockSpec((d, FFN_TILE), lambda b, j, p: (0, nt + j)),
                pl.BlockSpec((3, FFN_TILE), lambda b, j, p: (0, j)),
                pl.BlockSpec((3, FFN_TILE), lambda b, j, p: (0, nt + j)),
                pl.BlockSpec((1, FFN_TILE), lambda b, j, p: (0, j)),
                pl.BlockSpec((1, FFN_TILE), lambda b, j, p: (0, nt + j))]
    operands = [h, w_up, w_up, conv_w, conv_w, conv_b, conv_b]
    out_specs = [pl.BlockSpec((None, t // parts, FFN_TILE), lambda b, j, p: (b, p, j)),
                 pl.BlockSpec((None, 2, FFN_TILE), lambda b, j, p: (b, 0, j)),
                 pl.BlockSpec((None, 2, FFN_TILE), lambda b, j, p: (b, 0, j))]
    out_shape = [jax.ShapeDtypeStruct((bx, t, FFN_DIM), MXU_DTYPE),
                 jax.ShapeDtypeStruct((bx, 2, FFN_DIM), F32),
                 jax.ShapeDtypeStruct((bx, 2, FFN_DIM), F32)]
    scratch = [pltpu.VMEM((d, 2 * FFN_TILE), MXU_DTYPE), pltpu.VMEM((t + 8, 2 * FFN_TILE), F32)]
    sides = []
    if shift is not None:
        in_specs[0] = pl.BlockSpec((None, t, d), lambda b, j, p: (b, 0, 0), pipeline_mode=pl.Buffered(1))
    for cache_rows, src_seq0, new_rows in (shift or ()):
        n_seqs, total_rows = new_rows.shape[0], cache_rows.shape[1]
        deep = 2 * (total_rows - KV_ROWS) * HEAD_DIM * 4 <= SIDE_DEEP_BYTES
        side = _SideCopy(n_seqs, total_rows - KV_ROWS, src_seq0, bx * nt * parts, deep)
        sides.append(side)
        in_specs += [pl.BlockSpec(memory_space=pl.ANY), pl.BlockSpec(memory_space=pltpu.VMEM)]
        operands += [cache_rows, new_rows]
        out_specs.append(pl.BlockSpec(memory_space=pl.ANY))
        out_shape.append(jax.ShapeDtypeStruct((n_seqs, total_rows, HEAD_DIM), cache_rows.dtype))
    for side in sides:
        scratch += side.scratch_shapes()
    kern = functools.partial(_ffn_up_seq_kernel, t=t, chunk=chunk, parts=parts, sides=tuple(sides))
    outs = pl.pallas_call(
        kern,
        grid=(bx, nt, parts),
        in_specs=in_specs,
        out_specs=out_specs,
        out_shape=out_shape,
        scratch_shapes=scratch,
        compiler_params=_params("arbitrary", "arbitrary", "arbitrary"),
        name="ffn_up_sequence",
    )(*operands)
    rows = jnp.concatenate([outs[1], outs[2]], axis=-1)
    return (outs[0], rows) if shift is None else (outs[0], rows, list(outs[3:]))


def _ffn_up_step_kernel(a_ref, wg_ref, wv_ref, cwg_ref, cwv_ref, cbg_ref, cbv_ref, p0g_ref, p0v_ref,
                        p1g_ref, p1v_ref, g_ref, ug_ref, uv_ref):
    w = jnp.concatenate([wg_ref[...], wv_ref[...]], axis=-1).astype(MXU_DTYPE)
    u = jnp.dot(a_ref[...], w, preferred_element_type=F32)
    cw = jnp.concatenate([cwg_ref[...], cwv_ref[...]], axis=-1)
    cb = jnp.concatenate([cbg_ref[...], cbv_ref[...]], axis=-1)
    p0 = jnp.concatenate([p0g_ref[...], p0v_ref[...]], axis=-1)
    p1 = jnp.concatenate([p1g_ref[...], p1v_ref[...]], axis=-1)
    z = cb + cw[0:1] * p0
    z = z + cw[1:2] * p1
    z = z + cw[2:3] * u
    g_ref[...] = (_silu(z[:, :FFN_TILE]) * z[:, FFN_TILE:]).astype(g_ref.dtype)
    ug_ref[...] = u[:, :FFN_TILE]
    uv_ref[...] = u[:, FFN_TILE:]


def _ffn_up_step(h, w_up, conv_w, conv_b, prev0, prev1):
    b, d = h.shape
    nt = N_FFN_TILES
    conv_b = conv_b.reshape(1, 2 * FFN_DIM)
    lo = lambda j: (0, j)
    hi = lambda j: (0, nt + j)
    g, ug, uv = pl.pallas_call(
        _ffn_up_step_kernel,
        grid=(nt,),
        in_specs=[pl.BlockSpec((b, d), lambda j: (0, 0)),
                  pl.BlockSpec((d, FFN_TILE), lo), pl.BlockSpec((d, FFN_TILE), hi),
                  pl.BlockSpec((3, FFN_TILE), lo), pl.BlockSpec((3, FFN_TILE), hi),
                  pl.BlockSpec((1, FFN_TILE), lo), pl.BlockSpec((1, FFN_TILE), hi),
                  pl.BlockSpec((b, FFN_TILE), lo), pl.BlockSpec((b, FFN_TILE), hi),
                  pl.BlockSpec((b, FFN_TILE), lo), pl.BlockSpec((b, FFN_TILE), hi)],
        out_specs=[pl.BlockSpec((b, FFN_TILE), lo)] * 3,
        out_shape=[jax.ShapeDtypeStruct((b, FFN_DIM), MXU_DTYPE),
                   jax.ShapeDtypeStruct((b, FFN_DIM), F32),
                   jax.ShapeDtypeStruct((b, FFN_DIM), F32)],
        compiler_params=_params("arbitrary"),
        name="ffn_up_step",
    )(h, w_up, w_up, conv_w, conv_w, conv_b, conv_b, prev0, prev0, prev1, prev1)
    return g, jnp.concatenate([ug, uv], axis=-1)


def _qk_gains(q_gain, k_gain, g):
    return jnp.stack([q_gain[g], k_gain[g], jnp.ones_like(q_gain[g])])[:, None, :]


def _sequence_trunk(x, mods, w, tm, carried=None):
    bx, t, d = x.shape
    new_kv, new_ret, new_conv, updated = [[] for _ in range(N_GROUPS)], [], [], []
    for layer in range(DEPTH):
        m = mods[layer]
        h = _norm_mod(x, w["norm_mix"][layer], m, 0, 1, tm)
        if layer % N_MIXERS == 0:
            a = layer // N_MIXERS
            outs, lses = [], []
            for g in range(N_GROUPS):
                dil = DILATIONS[g]
                qkv = _qkv_project(h, w["attn_w_qkv"][a], _qk_gains(w["attn_q_gain"][a], w["attn_k_gain"][a], g),
                                   g, dil, tm)
                o, lse = _window_attention(qkv, _band_bias(w["rel_bias"], g), dil)
                outs.append(o)
                lses.append(lse)
                keep = min(WINDOWS[g], t)
                rows = keep // dil
                kv = qkv[:, 1:3, :, :, t // dil - rows:, :]
                kv = jnp.transpose(kv, (0, 4, 2, 1, 3, 5))
                new_kv[g].append(kv.reshape(bx, keep, 2, HEADS, HEAD_DIM))
            x = _attn_out(outs, lses, w["attn_w_o"][a], x, m, 2, min(tm, 256))
        else:
            r = layer // N_MIXERS
            proj = _matmul(h, w["ret_w_in"][r], tm, 1024)
            cos_rep, sin_signed = _rotation_tables(jnp.arange(t, dtype=jnp.int32))
            y, state = _retention_sequence(proj, cos_rep, sin_signed, w["ret_gn_gain"][r])
            new_ret.append(state)
            x = _resid_matmul(y, w["ret_w_o"][r], x, m, 2, tm, 512)
        h = _norm_mod(x, w["norm_ffn"][layer], m, 3, 4, tm)
        ffn_w = (w["ffn_w_up"][layer], w["ffn_conv_w"][layer], w["ffn_conv_b"][layer])
        if carried is None or not carried[layer]:
            g_act, rows = _ffn_up_sequence(h, *ffn_w)
        else:
            shifts = []
            for cache, a, k_new, v_new in carried[layer]:
                nl, n_seqs, width = cache.shape[:3]
                cache_rows = cache.reshape(nl * n_seqs, width * KV_ROWS, HEAD_DIM)
                new_rows = jnp.concatenate([k_new, v_new], axis=1)
                shifts.append((cache_rows, a * n_seqs, new_rows))
            g_act, rows, upds = _ffn_up_sequence(h, *ffn_w, shift=shifts)
            updated += [u.reshape(c[0].shape[1:]) for u, c in zip(upds, carried[layer])]
        new_conv.append(rows)
        x = _resid_matmul(g_act, w["ffn_w_down"][layer], x, m, 5, tm, 512)
    return x, [jnp.stack(kv) for kv in new_kv], jnp.stack(new_ret), jnp.stack(new_conv), updated


def _step_qkv(x, m, w, layer):
    b = x.shape[1]
    a = layer // N_MIXERS
    h = _norm_mod(x, w["norm_mix"][layer], m, 0, 1, b)
    out = []
    for g in range(N_GROUPS):
        qkv = _qkv_project(h, w["attn_w_qkv"][a], _qk_gains(w["attn_q_gain"][a], w["attn_k_gain"][a], g), g, 1, b)
        out.append(jnp.transpose(qkv.reshape(3, HEADS, b, HEAD_DIM), (0, 2, 1, 3)))
    return out


def _step_trunk(x, mods, caches, ret_state, conv_state, position, w, qkv_first=None, updated=None):
    b, d = x.shape
    x = x[None]
    new_kv, new_ret, new_conv = [[] for _ in range(N_GROUPS)], [], []
    for layer in range(DEPTH):
        m = mods[layer]
        if layer % N_MIXERS == 0:
            a = layer // N_MIXERS
            qkvs = qkv_first if (layer == 0 and qkv_first is not None) else _step_qkv(x, m, w, layer)
            outs, lses = [], []
            for g in range(N_GROUPS):
                qkv = qkvs[g]
                o, lse = _step_attention(qkv[0], qkv[1], qkv[2], caches[g], a, _group_bias(w["rel_bias"], g),
                                         DILATIONS[g], b)
                outs.append(jnp.transpose(o, (1, 0, 2))[None])
                lses.append(jnp.transpose(lse, (1, 0, 2))[None])
                if layer == 0 and updated is not None and g in updated:
                    new_kv[g].append(updated[g])
                else:
                    new_kv[g].append(_cache_shift(caches[g], a, qkv[1], qkv[2]))
            x = _attn_out(outs, lses, w["attn_w_o"][a], x, m, 2, b)
        else:
            r = layer // N_MIXERS
            h = _norm_mod(x, w["norm_mix"][layer], m, 0, 1, b)
            proj = _matmul(h, w["ret_w_in"][r], b, 1024)[0]
            q = proj[:, :RET_QK_WIDTH].reshape(b, RET_HEADS, RET_DK)
            k = proj[:, RET_QK_WIDTH:2 * RET_QK_WIDTH].reshape(b, RET_HEADS, RET_DK)
            v = proj[:, 2 * RET_QK_WIDTH:2 * RET_QK_WIDTH + RET_V_WIDTH].reshape(b, RET_HEADS, RET_DV)
            gate = proj[:, 2 * RET_QK_WIDTH + RET_V_WIDTH:].reshape(b, RET_HEADS, RET_DV)
            cos_rep, sin_signed = _rotation_tables(position)
            y, state = _retention_step(q, k, v, gate, ret_state, r, cos_rep, sin_signed, w["ret_gn_gain"][r])
            new_ret.append(state)
            x = _resid_matmul(y.reshape(1, b, RET_V_WIDTH), w["ret_w_o"][r], x, m, 2, b, 512)
        h = _norm_mod(x, w["norm_ffn"][layer], m, 3, 4, b)
        g_act, u_new = _ffn_up_step(h[0], w["ffn_w_up"][layer], w["ffn_conv_w"][layer], w["ffn_conv_b"][layer],
                                    conv_state[layer, :, 0], conv_state[layer, :, 1])
        new_conv.append(jnp.stack([conv_state[layer, :, 1], u_new], axis=1))
        x = _resid_matmul(g_act[None], w["ffn_w_down"][layer], x, m, 5, b, 512)
    return x[0], [jnp.stack(kv) for kv in new_kv], jnp.stack(new_ret), jnp.stack(new_conv)


def kernel(x_prompt, x_sample, cache_attn_kv_w128, cache_attn_kv_w512, cache_attn_kv_w2048, state_ret,
           state_conv, c_prompt, c_sample, rel_bias, w_ada, b_ada, norm_mix, norm_ffn, attn_w_qkv,
           attn_q_gain, attn_k_gain, attn_w_o, ret_w_in, ret_gn_gain, ret_w_o, ffn_w_up, ffn_conv_w,
           ffn_conv_b, ffn_w_down):
    w = dict(rel_bias=rel_bias, norm_mix=norm_mix, norm_ffn=norm_ffn, attn_w_qkv=attn_w_qkv,
             attn_q_gain=attn_q_gain, attn_k_gain=attn_k_gain, attn_w_o=attn_w_o, ret_w_in=ret_w_in,
             ret_gn_gain=ret_gn_gain, ret_w_o=ret_w_o, ffn_w_up=ffn_w_up, ffn_conv_w=ffn_conv_w,
             ffn_conv_b=ffn_conv_b, ffn_w_down=ffn_w_down)
    bp, sp, d = x_prompt.shape
    bs = x_sample.shape[0]
    past_len = cache_attn_kv_w2048.shape[2]

    rows = bp + bs
    pad = (-rows) % 8
    c_all = jnp.concatenate([c_prompt, c_sample, jnp.zeros((pad, d), F32)], axis=0)
    mods = _ada_modulation(c_all, w_ada, b_ada)
    mods_p = mods[:, :bp].reshape(DEPTH, bp, 1, 6 * d)
    mods_s = mods[:, bp:rows].reshape(DEPTH, 1, bs, 6 * d)

    caches = (cache_attn_kv_w128, cache_attn_kv_w512, cache_attn_kv_w2048)
    x_s = x_sample[:, 0]
    qkv_first = _step_qkv(x_s[None], mods_s[0], w, 0)
    hosted = [[N_GROUPS - 1], list(range(N_GROUPS - 2, -1, -1))] + [[]] * (DEPTH - 2)
    carried = [[(caches[g], 0, qkv_first[g][1], qkv_first[g][2]) for g in groups] for groups in hosted]
    y_p, kv_p, ret_p, conv_p, updated = _sequence_trunk(x_prompt, mods_p, w, 512, carried)
    position = past_len + jnp.arange(1, dtype=jnp.int32)
    order = [g for groups in hosted for g in groups]
    y_s, kv_s, ret_s, conv_s = _step_trunk(x_s, mods_s, caches, state_ret, state_conv, position, w,
                                           qkv_first, dict(zip(order, updated)))

    return (y_p, y_s[:, None, :], kv_p[0], kv_p[1], kv_p[2], ret_p, conv_p,
            kv_s[0], kv_s[1], kv_s[2], ret_s, conv_s)
```

```python
import functools
import math

import jax
import jax.numpy as jnp
from jax import lax
from jax.experimental import pallas as pl
from jax.experimental.pallas import tpu as pltpu

D_MODEL = 2048
DEPTH = 2
N_MIXERS = 2

WINDOWS = (128, 512, 2048)
DILATIONS = (1, 4, 16)
N_GROUPS = 3
HEAD_DIM = 128
HEADS = 8
ATTN_WIDTH = HEADS * HEAD_DIM
N_KEYS = 129
KEY_BLOCK = 128
N_BUCKETS = 32
MAX_DISTANCE = 2048
NEG_INF = -1e30

RET_HEADS = 8
RET_DK = 256
RET_DV = 512
RET_QK_WIDTH = RET_HEADS * RET_DK
RET_V_WIDTH = RET_HEADS * RET_DV
RET_CHUNK = 128
RET_HEAD_GROUP = 2
ROT_BASE = 10000.0

FFN_DIM = 5504
FFN_TILE = 128
N_FFN_TILES = FFN_DIM // FFN_TILE
NORM_EPS = 1e-6

F32 = jnp.float32
MXU_DTYPE = jnp.bfloat16
VMEM_LIMIT_BYTES = 56 * 1024 * 1024


def _params(*semantics):
    return pltpu.CompilerParams(dimension_semantics=semantics, vmem_limit_bytes=VMEM_LIMIT_BYTES)


def _silu(x):
    return x * jax.nn.sigmoid(x)


def _ada_kernel(c_ref, w_ref, b_ref, o_ref):
    a = _silu(c_ref[...]).astype(MXU_DTYPE)
    w = w_ref[...].astype(MXU_DTYPE)
    o_ref[...] = jnp.dot(a, w, preferred_element_type=F32) + b_ref[...]


def _ada_modulation(c, w_ada, b_ada):
    rows = c.shape[0]
    tn = 1024
    return pl.pallas_call(
        _ada_kernel,
        grid=(DEPTH, 6 * D_MODEL // tn),
        in_specs=[pl.BlockSpec((rows, D_MODEL), lambda l, j: (0, 0)),
                  pl.BlockSpec((None, D_MODEL, tn), lambda l, j: (l, 0, j)),
                  pl.BlockSpec((None, 1, tn), lambda l, j: (l, 0, j))],
        out_specs=pl.BlockSpec((None, rows, tn), lambda l, j: (l, 0, j)),
        out_shape=jax.ShapeDtypeStruct((DEPTH, rows, 6 * D_MODEL), F32),
        compiler_params=_params("arbitrary", "arbitrary"),
        name="ada_modulation",
    )(c, w_ada, b_ada.reshape(DEPTH, 1, 6 * D_MODEL))


def _norm_mod_kernel(x_ref, g_ref, shift_ref, scale_ref, o_ref):
    x = x_ref[...]
    y = x * lax.rsqrt(jnp.mean(x * x, axis=-1, keepdims=True) + NORM_EPS) * g_ref[...]
    o_ref[...] = (y * (1.0 + scale_ref[...]) + shift_ref[...]).astype(o_ref.dtype)


def _mod_spec(mods, tm, col, n_lead):
    per_row = mods.shape[1] > 1
    rb = tm if per_row else 1

    def index(*ids):
        b, i = ids[n_lead], ids[n_lead + 1]
        return (b, i if per_row else 0, col)

    return pl.BlockSpec((None, rb, D_MODEL), index)


def _norm_mod(x, gain, mods, shift_col, scale_col, tm):
    bx, t, d = x.shape
    return pl.pallas_call(
        _norm_mod_kernel,
        grid=(bx, t // tm),
        in_specs=[pl.BlockSpec((None, tm, d), lambda b, i: (b, i, 0)),
                  pl.BlockSpec((1, d), lambda b, i: (0, 0)),
                  _mod_spec(mods, tm, shift_col, 0),
                  _mod_spec(mods, tm, scale_col, 0)],
        out_specs=pl.BlockSpec((None, tm, d), lambda b, i: (b, i, 0)),
        out_shape=jax.ShapeDtypeStruct((bx, t, d), MXU_DTYPE),
        compiler_params=_params("arbitrary", "arbitrary"),
        name="norm_mod",
    )(x, gain.reshape(1, d), mods, mods)


def _cast_weight_once(w_ref, wbf_ref):
    @pl.when((pl.program_id(1) == 0) & (pl.program_id(2) == 0))
    def _():
        wbf_ref[...] = w_ref[...].astype(wbf_ref.dtype)


def _matmul_kernel(a_ref, w_ref, o_ref, wbf_ref):
    _cast_weight_once(w_ref, wbf_ref)
    o_ref[...] = jnp.dot(a_ref[...], wbf_ref[...], preferred_element_type=F32).astype(o_ref.dtype)


def _matmul(a, w, tm, tn, out_dtype=F32):
    bx, t, k = a.shape
    n = w.shape[1]
    return pl.pallas_call(
        _matmul_kernel,
        grid=(n // tn, bx, t // tm),
        in_specs=[pl.BlockSpec((None, tm, k), lambda j, b, i: (b, i, 0)),
                  pl.BlockSpec((k, tn), lambda j, b, i: (0, j))],
        out_specs=pl.BlockSpec((None, tm, tn), lambda j, b, i: (b, i, j)),
        out_shape=jax.ShapeDtypeStruct((bx, t, n), out_dtype),
        scratch_shapes=[pltpu.VMEM((k, tn), MXU_DTYPE)],
        compiler_params=_params("arbitrary", "arbitrary", "arbitrary"),
        name="matmul",
    )(a, w)


def _resid_matmul_kernel(a_ref, w_ref, x_ref, gate_ref, o_ref, wbf_ref):
    _cast_weight_once(w_ref, wbf_ref)
    acc = jnp.dot(a_ref[...], wbf_ref[...], preferred_element_type=F32)
    o_ref[...] = x_ref[...] + gate_ref[...] * acc


def _resid_matmul(a, w, x, mods, gate_col, tm, tn):
    bx, t, k = a.shape
    n = w.shape[1]
    cols_per_group = D_MODEL // tn
    per_row = mods.shape[1] > 1
    rb = tm if per_row else 1
    gate_spec = pl.BlockSpec(
        (None, rb, tn), lambda j, b, i: (b, i if per_row else 0, gate_col * cols_per_group + j))
    return pl.pallas_call(
        _resid_matmul_kernel,
        grid=(n // tn, bx, t // tm),
        in_specs=[pl.BlockSpec((None, tm, k), lambda j, b, i: (b, i, 0)),
                  pl.BlockSpec((k, tn), lambda j, b, i: (0, j)),
                  pl.BlockSpec((None, tm, tn), lambda j, b, i: (b, i, j)),
                  gate_spec],
        out_specs=pl.BlockSpec((None, tm, tn), lambda j, b, i: (b, i, j)),
        out_shape=jax.ShapeDtypeStruct((bx, t, n), F32),
        scratch_shapes=[pltpu.VMEM((k, tn), MXU_DTYPE)],
        compiler_params=_params("arbitrary", "arbitrary", "arbitrary"),
        name="resid_matmul",
    )(a, w, x, mods)


def _qkv_kernel(a_ref, w_ref, gain_ref, o_ref, wbf_ref, y_ref, *, dilation, tm):
    _cast_weight_once(w_ref, wbf_ref)
    normed = pl.program_id(0) < 2
    gain = gain_ref[...]
    halves = 2 if tm % (16 * dilation) == 0 else 1
    hm = tm // halves
    rows = hm // dilation
    for half in range(halves):
        acc = jnp.dot(a_ref[half * hm:(half + 1) * hm, :], wbf_ref[...], preferred_element_type=F32)
        for h in range(HEADS):
            xh = acc[:, h * HEAD_DIM:(h + 1) * HEAD_DIM]
            yh = xh * lax.rsqrt(jnp.mean(xh * xh, axis=-1, keepdims=True) + NORM_EPS) * gain
            yh = jnp.where(normed, yh, xh)
            if dilation == 1:
                o_ref[0, h, half * hm:(half + 1) * hm, :] = yh
            else:
                y_ref[h, half * hm:(half + 1) * hm, :] = yh
        if dilation > 1:
            for c in range(dilation):
                for h in range(HEADS):
                    o_ref[c, h, half * rows:(half + 1) * rows, :] = (
                        y_ref[h, pl.ds(half * hm + c, rows, stride=dilation), :])


def _qkv_project(h, w_qkv, gains, group, dilation, tm):
    bx, t, d = h.shape
    td = t // dilation
    kern = functools.partial(_qkv_kernel, dilation=dilation, tm=tm)
    return pl.pallas_call(
        kern,
        grid=(3, bx, t // tm),
        in_specs=[pl.BlockSpec((None, tm, d), lambda p, b, i: (b, i, 0)),
                  pl.BlockSpec((d, ATTN_WIDTH), lambda p, b, i: (0, group * 3 + p)),
                  pl.BlockSpec((None, 1, HEAD_DIM), lambda p, b, i: (p, 0, 0))],
        out_specs=pl.BlockSpec((None, None, dilation, HEADS, tm // dilation, HEAD_DIM),
                               lambda p, b, i: (b, p, 0, 0, i, 0)),
        out_shape=jax.ShapeDtypeStruct((bx, 3, dilation, HEADS, td, HEAD_DIM), F32),
        scratch_shapes=[pltpu.VMEM((d, ATTN_WIDTH), MXU_DTYPE), pltpu.VMEM((HEADS, tm, HEAD_DIM), F32)],
        compiler_params=_params("arbitrary", "arbitrary", "arbitrary"),
        name=f"qkv_project_g{group}",
    )(h, w_qkv, gains)


def _t5_causal_bucket(dist):
    max_exact = N_BUCKETS // 2
    d = jnp.maximum(dist, 1).astype(F32)
    large = max_exact + (jnp.log(d / max_exact) / math.log(MAX_DISTANCE / max_exact)
                         * (N_BUCKETS - max_exact)).astype(jnp.int32)
    return jnp.where(dist < max_exact, dist, jnp.minimum(large, N_BUCKETS - 1))


def _bucket_lookup(rel_bias, g, bucket, fill):
    cols = rel_bias[:, g * HEADS:(g + 1) * HEADS].astype(F32)
    expand = (slice(None),) + (None,) * bucket.ndim
    out = jnp.full((HEADS,) + bucket.shape, fill, F32)
    for b in range(N_BUCKETS):
        out = jnp.where(bucket[None] == b, cols[b][expand], out)
    return out


def _group_bias(rel_bias, g):
    dist = DILATIONS[g] * jnp.arange(N_KEYS, dtype=jnp.int32)
    return _bucket_lookup(rel_bias, g, _t5_causal_bucket(dist), 0.0)


def _band_bias(rel_bias, g):
    a = jnp.arange(KEY_BLOCK, dtype=jnp.int32)[:, None]
    c = jnp.arange(2 * KEY_BLOCK, dtype=jnp.int32)[None, :]
    rel = a + KEY_BLOCK - c
    valid = (rel >= 0) & (rel <= KEY_BLOCK)
    bucket = jnp.where(valid, _t5_causal_bucket(DILATIONS[g] * jnp.clip(rel, 0, KEY_BLOCK)), -1)
    return _bucket_lookup(rel_bias, g, bucket, NEG_INF)


def _window_attn_kernel(q_ref, kp_ref, kc_ref, vp_ref, vc_ref, bias_ref, o_ref, lse_ref,
                        *, dilation, heads):
    i = pl.program_id(1)
    c = pl.program_id(3)
    col = lax.broadcasted_iota(jnp.int32, (KEY_BLOCK, 2 * KEY_BLOCK), 1)
    no_prev = (i == 0) & (col < KEY_BLOCK)
    for h in range(heads):
        q = q_ref[h].astype(MXU_DTYPE)
        k = jnp.concatenate([kp_ref[h], kc_ref[h]], axis=0).astype(MXU_DTYPE)
        v = jnp.concatenate([vp_ref[h], vc_ref[h]], axis=0).astype(MXU_DTYPE)
        s = lax.dot_general(q, k, (((1,), (1,)), ((), ())), preferred_element_type=F32)
        s = s * (HEAD_DIM ** -0.5) + bias_ref[h]
        s = jnp.where(no_prev, NEG_INF, s)
        m = jnp.max(s, axis=-1, keepdims=True)
        p = jnp.exp(s - m)
        l = jnp.sum(p, axis=-1, keepdims=True)
        o = jnp.dot(p.astype(MXU_DTYPE), v, preferred_element_type=F32) / l
        lse = jnp.broadcast_to(m + jnp.log(l), (KEY_BLOCK, HEAD_DIM))
        if dilation == 1:
            o_ref[h] = o
            lse_ref[h] = lse
        else:
            o_ref[h, pl.ds(c, KEY_BLOCK, stride=dilation), :] = o
            lse_ref[h, pl.ds(c, KEY_BLOCK, stride=dilation), :] = lse


def _window_attention(qkv, band_bias, dilation):
    bx, _, _, _, td, _ = qkv.shape
    t = td * dilation
    nblk = td // KEY_BLOCK
    hb = 1
    heads = HEADS // hb

    def spec(part, prev):
        def index(b, i, hq, c):
            return (b, part, c, hq, jnp.maximum(i - 1, 0) if prev else i, 0)
        return pl.BlockSpec((None, None, None, heads, KEY_BLOCK, HEAD_DIM), index)

    out_spec = pl.BlockSpec((None, heads, KEY_BLOCK * dilation, HEAD_DIM), lambda b, i, hq, c: (b, hq, i, 0))
    kern = functools.partial(_window_attn_kernel, dilation=dilation, heads=heads)
    return pl.pallas_call(
        kern,
        grid=(bx, nblk, hb, dilation),
        in_specs=[spec(0, False), spec(1, True), spec(1, False), spec(2, True), spec(2, False),
                  pl.BlockSpec((heads, KEY_BLOCK, 2 * KEY_BLOCK), lambda b, i, hq, c: (hq, 0, 0))],
        out_specs=[out_spec, out_spec],
        out_shape=[jax.ShapeDtypeStruct((bx, HEADS, t, HEAD_DIM), F32)] * 2,
        compiler_params=_params("arbitrary", "arbitrary", "arbitrary", "arbitrary"),
        name=f"window_attention_d{dilation}",
    )(qkv, qkv, qkv, qkv, qkv, band_bias)


STEP_ROWS = 8
STEP_SEQS = 8


def _step_attn_kernel(*refs):
    q_ref, kn_ref, vn_ref = refs[:3]
    kv_refs = refs[3:3 + STEP_ROWS]
    bias0_ref, bias_ref, o_ref, lse_ref, m_ref, l_ref, acc_ref = refs[3 + STEP_ROWS:]
    j = pl.program_id(1)
    scale = HEAD_DIM ** -0.5

    def chunk(ci, carry):
        sl = pl.ds(pl.multiple_of(ci * STEP_SEQS, STEP_SEQS), STEP_SEQS)
        q = q_ref[sl]

        @pl.when(j == 0)
        def _():
            s0 = jnp.sum(q * kn_ref[sl], axis=-1, keepdims=True) * scale + bias0_ref[...]
            m_ref[sl] = jnp.broadcast_to(s0, q.shape)
            l_ref[sl] = jnp.ones(q.shape, F32)
            acc_ref[sl] = vn_ref[sl]

        scores = [jnp.sum(q * kv_refs[r][sl, 0], axis=-1, keepdims=True) * scale + bias_ref[r]
                  for r in range(STEP_ROWS)]
        m_old = m_ref[sl]
        m_new = m_old
        for s in scores:
            m_new = jnp.maximum(m_new, s)
        alpha = jnp.exp(m_old - m_new)
        l_new = alpha * l_ref[sl]
        acc = alpha * acc_ref[sl]
        for r in range(STEP_ROWS):
            p = jnp.exp(scores[r] - m_new)
            l_new = l_new + p
            acc = acc + p * kv_refs[r][sl, 1]
        m_ref[sl] = m_new
        l_ref[sl] = l_new
        acc_ref[sl] = acc
        return carry

    n_chunks = q_ref.shape[0] // STEP_SEQS
    lax.fori_loop(0, n_chunks, chunk, 0, unroll=2 if n_chunks % 2 == 0 else 1)

    @pl.when(j == pl.num_programs(1) - 1)
    def _():
        o_ref[...] = acc_ref[...] / l_ref[...]
        lse_ref[...] = m_ref[...] + jnp.log(l_ref[...])


def _step_attention(q, k_new, v_new, cache, layer, gb, dilation, bt):
    b = q.shape[0]
    nk = N_KEYS - 1
    lanes = jnp.broadcast_to(gb.T[:, :, None], (N_KEYS, HEADS, HEAD_DIM))
    bias0 = lanes[0]
    bias_rows = lanes[:0:-1]
    qspec = pl.BlockSpec((bt, HEADS, HEAD_DIM), lambda bi, j: (bi, 0, 0))

    def row_spec(r):
        return pl.BlockSpec((None, bt, None, 2, HEADS, HEAD_DIM),
                            lambda bi, j: (layer, bi, (j * STEP_ROWS + r) * dilation, 0, 0, 0))

    return pl.pallas_call(
        _step_attn_kernel,
        grid=(b // bt, nk // STEP_ROWS),
        in_specs=[qspec, qspec, qspec] + [row_spec(r) for r in range(STEP_ROWS)] + [
            pl.BlockSpec((HEADS, HEAD_DIM), lambda bi, j: (0, 0)),
            pl.BlockSpec((STEP_ROWS, HEADS, HEAD_DIM), lambda bi, j: (j, 0, 0))],
        out_specs=[qspec, qspec],
        out_shape=[jax.ShapeDtypeStruct((b, HEADS, HEAD_DIM), F32)] * 2,
        scratch_shapes=[pltpu.VMEM((bt, HEADS, HEAD_DIM), F32)] * 3,
        compiler_params=_params("arbitrary", "arbitrary"),
        name=f"step_attention_d{dilation}",
    )(q, k_new, v_new, *([cache] * STEP_ROWS), bias0, bias_rows)


def _attn_out_kernel(o0_ref, o1_ref, o2_ref, l0_ref, l1_ref, l2_ref, w_ref, x_ref, gate_ref, out_ref,
                     wbf_ref):
    _cast_weight_once(w_ref, wbf_ref)
    heads = []
    for h in range(HEADS):
        l0, l1, l2 = l0_ref[h], l1_ref[h], l2_ref[h]
        m = jnp.maximum(jnp.maximum(l0, l1), l2)
        e0, e1, e2 = jnp.exp(l0 - m), jnp.exp(l1 - m), jnp.exp(l2 - m)
        tot = e0 + e1 + e2
        merged = (e0 / tot) * o0_ref[h] + (e1 / tot) * o1_ref[h] + (e2 / tot) * o2_ref[h]
        heads.append(merged.astype(MXU_DTYPE))
    acc = jnp.dot(jnp.concatenate(heads, axis=-1), wbf_ref[...], preferred_element_type=F32)
    out_ref[...] = x_ref[...] + gate_ref[...] * acc


def _attn_out(outs, lses, w_o, x, mods, gate_col, tm):
    bx, t, d = x.shape
    per_row = mods.shape[1] > 1
    rb = tm if per_row else 1
    aspec = pl.BlockSpec((None, HEADS, tm, HEAD_DIM), lambda j, b, i: (b, 0, i, 0))
    xspec = pl.BlockSpec((None, tm, d), lambda j, b, i: (b, i, 0))
    return pl.pallas_call(
        _attn_out_kernel,
        grid=(1, bx, t // tm),
        in_specs=[aspec] * 6 + [
            pl.BlockSpec((ATTN_WIDTH, d), lambda j, b, i: (0, 0)),
            xspec,
            pl.BlockSpec((None, rb, d), lambda j, b, i: (b, i if per_row else 0, gate_col))],
        out_specs=xspec,
        out_shape=jax.ShapeDtypeStruct((bx, t, d), F32),
        scratch_shapes=[pltpu.VMEM((ATTN_WIDTH, d), MXU_DTYPE)],
        compiler_params=_params("arbitrary", "arbitrary", "arbitrary"),
        name="attn_out",
    )(*outs, *lses, w_o, x, mods)


KV_ROWS = 2 * HEADS
SHIFT_BUFFERS = 3
SHIFT_CHUNK_BYTES = 8 * 1024 * 1024


def _shift_chunking(b, keep_rows):
    row_bytes = HEAD_DIM * 4
    pieces = 1
    while (keep_rows // pieces) * row_bytes > SHIFT_CHUNK_BYTES or keep_rows % (8 * pieces):
        pieces += 1
    rows = keep_rows // pieces
    seqs = 1
    while seqs * 2 * rows * row_bytes <= SHIFT_CHUNK_BYTES // 2 and b % (seqs * 2) == 0:
        seqs *= 2
    return seqs, rows, pieces


def _cache_shift_kernel(prev_ref, new_ref, out_ref, buf_ref, in_sems, out_sems, new_sem,
                        *, layer, seqs, rows, pieces):
    b, total_rows, _ = out_ref.shape
    n_chunks = (b // seqs) * pieces

    def load(k, slot):
        block, piece = k // pieces, k % pieces
        src = prev_ref.at[pl.ds(layer * b + block * seqs, seqs), pl.ds(KV_ROWS + piece * rows, rows)]
        return pltpu.make_async_copy(src, buf_ref.at[slot], in_sems.at[slot])

    def store(k, slot):
        block, piece = k // pieces, k % pieces
        dst = out_ref.at[pl.ds(block * seqs, seqs), pl.ds(piece * rows, rows)]
        return pltpu.make_async_copy(buf_ref.at[slot], dst, out_sems.at[slot])

    new_copy = pltpu.make_async_copy(new_ref, out_ref.at[:, pl.ds(total_rows - KV_ROWS, KV_ROWS)], new_sem)
    new_copy.start()
    load(0, 0).start()

    def step(k, carry):
        slot = k % SHIFT_BUFFERS
        nxt = k + 1
        nxt_slot = nxt % SHIFT_BUFFERS

        @pl.when(nxt < n_chunks)
        def _():
            @pl.when(nxt >= SHIFT_BUFFERS)
            def _():
                store(nxt - SHIFT_BUFFERS, nxt_slot).wait()
            load(nxt, nxt_slot).start()

        load(k, slot).wait()
        store(k, slot).start()
        return carry

    lax.fori_loop(0, n_chunks, step, 0)
    for k in range(max(0, n_chunks - SHIFT_BUFFERS), n_chunks):
        store(k, k % SHIFT_BUFFERS).wait()
    new_copy.wait()


def _cache_shift(prev, layer, k_new, v_new):
    nl, b, w = prev.shape[:3]
    prev_rows = prev.reshape(nl * b, w * KV_ROWS, HEAD_DIM)
    new_rows = jnp.concatenate([k_new, v_new], axis=1)
    seqs, rows, pieces = _shift_chunking(b, (w - 1) * KV_ROWS)
    out = pl.pallas_call(
        functools.partial(_cache_shift_kernel, layer=layer, seqs=seqs, rows=rows, pieces=pieces),
        in_specs=[pl.BlockSpec(memory_space=pl.ANY),
                  pl.BlockSpec(memory_space=pltpu.VMEM)],
        out_specs=pl.BlockSpec(memory_space=pl.ANY),
        out_shape=jax.ShapeDtypeStruct((b, w * KV_ROWS, HEAD_DIM), prev.dtype),
        scratch_shapes=[pltpu.VMEM((SHIFT_BUFFERS, seqs, rows, HEAD_DIM), prev.dtype),
                        pltpu.SemaphoreType.DMA((SHIFT_BUFFERS,)),
                        pltpu.SemaphoreType.DMA((SHIFT_BUFFERS,)),
                        pltpu.SemaphoreType.DMA(())],
        compiler_params=pltpu.CompilerParams(vmem_limit_bytes=VMEM_LIMIT_BYTES),
        name="cache_shift",
    )(prev_rows, new_rows)
    return out.reshape(b, w, 2, HEADS, HEAD_DIM)


def _rotation_tables(positions):
    half = RET_DK // 2
    inv_freq = 1.0 / (ROT_BASE ** jnp.linspace(0.0, 1.0, half, dtype=F32))
    ang = positions.astype(F32)[:, None] * inv_freq[None, :]
    cos, sin = jnp.cos(ang), jnp.sin(ang)
    cos_rep = jnp.stack([cos, cos], axis=-1).reshape(-1, RET_DK)
    sin_signed = jnp.stack([-sin, sin], axis=-1).reshape(-1, RET_DK)
    return cos_rep, sin_signed


def _rotate_pairs(x, cos_rep, sin_signed):
    lanes = 128
    even = lax.broadcasted_iota(jnp.int32, (x.shape[0], lanes), 1) % 2 == 0
    parts = []
    for s in range(x.shape[1] // lanes):
        xs = x[:, s * lanes:(s + 1) * lanes]
        parts.append(jnp.where(even, pltpu.roll(xs, lanes - 1, 1), pltpu.roll(xs, 1, 1)))
    swapped = jnp.concatenate(parts, axis=-1)
    return x * cos_rep + swapped * sin_signed


def _log_gamma():
    return jnp.log1p(-jnp.exp2(-5.0 - jnp.arange(RET_HEADS, dtype=F32)))


def _group_norm_gate(o, gain, gate):
    mu = jnp.mean(o, axis=-1, keepdims=True)
    var = jnp.mean(jnp.square(o - mu), axis=-1, keepdims=True)
    return _silu(gate) * ((o - mu) * lax.rsqrt(var + NORM_EPS) * gain)


def _retention_kernel(q_ref, k_ref, v_ref, gate_ref, cos_ref, sin_ref, decay_ref, qdec_ref, kdec_ref,
                      cdec_ref, gain_ref, y_ref, s_out_ref, s_ref):
    c = pl.program_id(1)
    nb, hg = s_ref.shape[:2]

    @pl.when(c == 0)
    def _():
        s_ref[...] = jnp.zeros(s_ref.shape, F32)

    cos, sin = cos_ref[...], sin_ref[...]
    for b in range(nb):
        for hh in range(hg):
            ksl = slice(hh * RET_DK, (hh + 1) * RET_DK)
            vsl = slice(hh * RET_DV, (hh + 1) * RET_DV)
            q = _rotate_pairs(q_ref[b, :, ksl], cos, sin)
            k = _rotate_pairs(k_ref[b, :, ksl], cos, sin) * (RET_DK ** -0.5)
            v = v_ref[b, :, vsl].astype(MXU_DTYPE)
            state = s_ref[b, hh]
            scores = lax.dot_general(q.astype(MXU_DTYPE), k.astype(MXU_DTYPE), (((1,), (1,)), ((), ())),
                                     preferred_element_type=F32) * decay_ref[hh]
            o = jnp.dot(scores.astype(MXU_DTYPE), v, preferred_element_type=F32)
            o = o + jnp.dot((q * qdec_ref[hh]).astype(MXU_DTYPE), state.astype(MXU_DTYPE),
                            preferred_element_type=F32)
            kd_t = jnp.transpose(k * kdec_ref[hh]).astype(MXU_DTYPE)
            s_ref[b, hh] = cdec_ref[hh] * state + jnp.dot(kd_t, v, preferred_element_type=F32)
            y_ref[b, :, vsl] = _group_norm_gate(o, gain_ref[hh], gate_ref[b, :, vsl]).astype(y_ref.dtype)

    @pl.when(c == pl.num_programs(1) - 1)
    def _():
        s_out_ref[...] = s_ref[...]


def _retention_sequence(proj, cos_rep, sin_signed, gn_gain):
    bx, t, _ = proj.shape
    cw = RET_CHUNK
    lg = _log_gamma()
    pos = jnp.arange(cw, dtype=F32)
    diff = pos[:, None] - pos[None, :]
    decay = jnp.where(diff >= 0, jnp.exp(diff[None] * lg[:, None, None]), 0.0)
    q_decay = jnp.exp((pos[:, None] + 1.0) * lg[None, :]).T[:, :, None]
    k_decay = jnp.exp((cw - 1.0 - pos)[:, None] * lg[None, :]).T[:, :, None]
    chunk_decay = jnp.exp(cw * lg).reshape(RET_HEADS, 1, 1)
    hg = RET_HEAD_GROUP
    nqk = RET_QK_WIDTH // (hg * RET_DK)
    nv0 = 2 * RET_QK_WIDTH // (hg * RET_DV)
    ng0 = nv0 + RET_HEADS // hg
    return pl.pallas_call(
        _retention_kernel,
        grid=(RET_HEADS // hg, t // cw),
        in_specs=[pl.BlockSpec((bx, cw, hg * RET_DK), lambda h, c: (0, c, h)),
                  pl.BlockSpec((bx, cw, hg * RET_DK), lambda h, c: (0, c, nqk + h)),
                  pl.BlockSpec((bx, cw, hg * RET_DV), lambda h, c: (0, c, nv0 + h)),
                  pl.BlockSpec((bx, cw, hg * RET_DV), lambda h, c: (0, c, ng0 + h)),
                  pl.BlockSpec((cw, RET_DK), lambda h, c: (c, 0)),
                  pl.BlockSpec((cw, RET_DK), lambda h, c: (c, 0)),
                  pl.BlockSpec((hg, cw, cw), lambda h, c: (h, 0, 0)),
                  pl.BlockSpec((hg, cw, 1), lambda h, c: (h, 0, 0)),
                  pl.BlockSpec((hg, cw, 1), lambda h, c: (h, 0, 0)),
                  pl.BlockSpec((hg, 1, 1), lambda h, c: (h, 0, 0)),
                  pl.BlockSpec((hg, 1, RET_DV), lambda h, c: (h, 0, 0))],
        out_specs=[pl.BlockSpec((bx, cw, hg * RET_DV), lambda h, c: (0, c, h)),
                   pl.BlockSpec((bx, hg, RET_DK, RET_DV), lambda h, c: (0, h, 0, 0))],
        out_shape=[jax.ShapeDtypeStruct((bx, t, RET_V_WIDTH), MXU_DTYPE),
                   jax.ShapeDtypeStruct((bx, RET_HEADS, RET_DK, RET_DV), F32)],
        scratch_shapes=[pltpu.VMEM((bx, hg, RET_DK, RET_DV), F32)],
        compiler_params=_params("arbitrary", "arbitrary"),
        name="retention_sequence",
    )(proj, proj, proj, proj, cos_rep, sin_signed, decay, q_decay, k_decay, chunk_decay,
      gn_gain.reshape(RET_HEADS, 1, RET_DV))


def _retention_step_kernel(q_ref, k_ref, v_ref, gate_ref, cos_ref, sin_ref, gamma_ref, gain_ref, s_ref,
                           y_ref, s_out_ref):
    cos, sin = cos_ref[...], sin_ref[...]
    q = _rotate_pairs(q_ref[...], cos, sin)
    k = _rotate_pairs(k_ref[...], cos, sin) * (RET_DK ** -0.5)
    qk = jnp.sum(q * k, axis=-1, keepdims=True)
    q_t = jnp.transpose(q * gamma_ref[...])
    k_t = jnp.transpose(k)
    v = v_ref[...]
    gamma = gamma_ref[...]
    rows = []
    for h in range(RET_HEADS):
        state = s_ref[h]
        vh = v[h:h + 1, :]
        rows.append(qk[h:h + 1, :] * vh + jnp.sum(q_t[:, h:h + 1] * state, axis=0, keepdims=True))
        s_out_ref[h] = gamma[h:h + 1, :] * state + k_t[:, h:h + 1] * vh
    o = jnp.concatenate(rows, axis=0)
    y_ref[...] = _group_norm_gate(o, gain_ref[...], gate_ref[...]).astype(y_ref.dtype)


def _retention_step(q, k, v, gate, state, layer, cos_rep, sin_signed, gn_gain):
    b = q.shape[0]
    gamma = jnp.exp(_log_gamma()).reshape(RET_HEADS, 1)
    qspec = pl.BlockSpec((None, RET_HEADS, RET_DK), lambda i: (i, 0, 0))
    vspec = pl.BlockSpec((None, RET_HEADS, RET_DV), lambda i: (i, 0, 0))
    sspec = pl.BlockSpec((None, RET_HEADS, RET_DK, RET_DV), lambda i: (i, 0, 0, 0))
    sspec_in = pl.BlockSpec((None, None, RET_HEADS, RET_DK, RET_DV), lambda i: (layer, i, 0, 0, 0))
    return pl.pallas_call(
        _retention_step_kernel,
        grid=(b,),
        in_specs=[qspec, qspec, vspec, vspec,
                  pl.BlockSpec((1, RET_DK), lambda i: (0, 0)),
                  pl.BlockSpec((1, RET_DK), lambda i: (0, 0)),
                  pl.BlockSpec((RET_HEADS, 1), lambda i: (0, 0)),
                  pl.BlockSpec((RET_HEADS, RET_DV), lambda i: (0, 0)),
                  sspec_in],
        out_specs=[vspec, sspec],
        out_shape=[jax.ShapeDtypeStruct((b, RET_HEADS, RET_DV), MXU_DTYPE),
                   jax.ShapeDtypeStruct(state.shape[1:], F32)],
        compiler_params=_params("arbitrary"),
        name="retention_step",
    )(q, k, v, gate, cos_rep, sin_signed, gamma, gn_gain.reshape(RET_HEADS, RET_DV), state)


SIDE_POINTS = 2
SIDE_CHUNK_BYTES = 9 * 1024 * 1024
SIDE_DEEP_BYTES = 10 * 1024 * 1024


class _SideCopy:
    def __init__(self, n_seqs, keep_rows, src_seq0, n_steps, deep):
        assert n_seqs + 1 < n_steps
        row_bytes = HEAD_DIM * 4
        pieces = SIDE_POINTS
        while keep_rows % (8 * pieces) or (keep_rows // pieces) * row_bytes > SIDE_CHUNK_BYTES:
            pieces += SIDE_POINTS
        self.rows = keep_rows // pieces
        self.per_point = pieces // SIDE_POINTS
        self.n_seqs, self.src_seq0, self.deep = n_seqs, src_seq0, deep

    def scratch_shapes(self):
        n = SIDE_POINTS * self.per_point * (2 if self.deep else 1)
        return [pltpu.VMEM((n, self.rows, HEAD_DIM), F32),
                pltpu.SemaphoreType.DMA((n,)), pltpu.SemaphoreType.DMA((n,)), pltpu.SemaphoreType.DMA(())]

    def bind(self, src_ref, new_ref, dst_ref, buf_ref, load_sems, store_sems, new_sem):
        self.refs = (src_ref, dst_ref, buf_ref, load_sems, store_sems)
        self.new = (new_ref, new_sem)

    def _slot(self, seq, piece):
        return piece + (seq % 2) * SIDE_POINTS * self.per_point if self.deep else piece

    def _load(self, seq, point, q):
        src_ref, _, buf_ref, load_sems, _ = self.refs
        piece = point * self.per_point + q
        slot = self._slot(seq, piece)
        src = src_ref.at[self.src_seq0 + seq, pl.ds(KV_ROWS + piece * self.rows, self.rows)]
        return pltpu.make_async_copy(src, buf_ref.at[slot], load_sems.at[slot])

    def _store(self, seq, point, q):
        _, dst_ref, buf_ref, _, store_sems = self.refs
        piece = point * self.per_point + q
        slot = self._slot(seq, piece)
        dst = dst_ref.at[seq, pl.ds(piece * self.rows, self.rows)]
        return pltpu.make_async_copy(buf_ref.at[slot], dst, store_sems.at[slot])

    def serve(self, step, point):
        if self.deep:
            out_step, out_point = step - 1, point
        else:
            out_step, out_point = (step if point > 0 else step - 1), (point - 1) % SIDE_POINTS
        retire_step = out_step - 1 if self.deep else step - 1

        @pl.when((out_step >= 0) & (out_step < self.n_seqs))
        def _():
            for q in range(self.per_point):
                self._load(out_step, out_point, q).wait()
                self._store(out_step, out_point, q).start()

        @pl.when((retire_step >= 0) & (retire_step < self.n_seqs))
        def _():
            for q in range(self.per_point):
                self._store(retire_step, point, q).wait()

        @pl.when(step < self.n_seqs)
        def _():
            for q in range(self.per_point):
                self._load(step, point, q).start()

    def append_new_rows(self):
        new_ref, new_sem = self.new
        dst_ref = self.refs[1]
        total_rows = dst_ref.shape[1]
        copy = pltpu.make_async_copy(new_ref, dst_ref.at[:, pl.ds(total_rows - KV_ROWS, KV_ROWS)], new_sem)
        copy.start()
        copy.wait()


def _ffn_up_seq_kernel(*refs, t, chunk, parts, sides):
    a_ref, wg_ref, wv_ref, cwg_ref, cwv_ref, cbg_ref, cbv_ref = refs[:7]
    ns = len(sides)
    side_in = refs[7:7 + 2 * ns]
    g_ref, rows_g_ref, rows_v_ref = refs[7 + 2 * ns:10 + 2 * ns]
    side_out = refs[10 + 2 * ns:10 + 3 * ns]
    wbf_ref, u_ref = refs[10 + 3 * ns:12 + 3 * ns]
    side_scratch = refs[12 + 3 * ns:]
    for i, side in enumerate(sides):
        side.bind(side_in[2 * i], side_in[2 * i + 1], side_out[i], *side_scratch[4 * i:4 * i + 4])
    part_id = pl.program_id(2)
    step = (pl.program_id(0) * pl.num_programs(1) + pl.program_id(1)) * parts + part_id
    n_steps = pl.num_programs(0) * pl.num_programs(1) * parts

    @pl.when(part_id == 0)
    def _():
        wbf_ref[:, :FFN_TILE] = wg_ref[...].astype(wbf_ref.dtype)
        wbf_ref[:, FFN_TILE:] = wv_ref[...].astype(wbf_ref.dtype)
        u_ref[0:8, :] = jnp.zeros((8, 2 * FFN_TILE), F32)

    cw = jnp.concatenate([cwg_ref[...], cwv_ref[...]], axis=-1)
    cb = jnp.concatenate([cbg_ref[...], cbv_ref[...]], axis=-1)
    part_rows = t // parts
    n_chunks = part_rows // chunk
    stride = max(n_chunks // SIDE_POINTS, 1)

    def rows_of_part(part):
        for s in range(n_chunks):
            r0 = part * part_rows + s * chunk
            o0 = s * chunk
            if s % stride == 0 and s // stride < SIDE_POINTS:
                for side in sides:
                    side.serve(step, s // stride)
            u = jnp.dot(a_ref[r0:r0 + chunk, :], wbf_ref[...], preferred_element_type=F32)
            u_ref[8 + r0:8 + r0 + chunk, :] = u
            z = cb + cw[0:1] * u_ref[6 + r0:6 + r0 + chunk, :]
            z = z + cw[1:2] * u_ref[7 + r0:7 + r0 + chunk, :]
            z = z + cw[2:3] * u
            g_ref[o0:o0 + chunk, :] = (_silu(z[:, :FFN_TILE]) * z[:, FFN_TILE:]).astype(g_ref.dtype)

    for part in range(parts):
        pl.when(part_id == part)(functools.partial(rows_of_part, part))

    @pl.when(part_id == parts - 1)
    def _():
        rows_g_ref[...] = u_ref[6 + t:8 + t, :FFN_TILE]
        rows_v_ref[...] = u_ref[6 + t:8 + t, FFN_TILE:]

    if sides:
        @pl.when(step == n_steps - 1)
        def _():
            for side in sides:
                side.append_new_rows()


def _ffn_up_sequence(h, w_up, conv_w, conv_b, shift=None):
    bx, t, d = h.shape
    nt = N_FFN_TILES
    parts = 2 if (shift is not None and t % 1024 == 0) else 1
    chunk = min(t // parts, 512)
    conv_b = conv_b.reshape(1, 2 * FFN_DIM)
    in_specs = [pl.BlockSpec((None, t, d), lambda b, j, p: (b, 0, 0)),
                pl.BlockSpec((d, FFN_TILE), lambda b, j, p: (0, j)),
                pl.BlockSpec((d, FFN_TILE), lambda b, j, p: (0, nt + j)),
                pl.BlockSpec((3, FFN_TILE), lambda b, j, p: (0, j)),
                pl.BlockSpec((3, FFN_TILE), lambda b, j, p: (0, nt + j)),
                pl.BlockSpec((1, FFN_TILE), lambda b, j, p: (0, j)),
                pl.BlockSpec((1, FFN_TILE), lambda b, j, p: (0, nt + j))]
    operands = [h, w_up, w_up, conv_w, conv_w, conv_b, conv_b]
    out_specs = [pl.BlockSpec((None, t // parts, FFN_TILE), lambda b, j, p: (b, p, j)),
                 pl.BlockSpec((None, 2, FFN_TILE), lambda b, j, p: (b, 0, j)),
                 pl.BlockSpec((None, 2, FFN_TILE), lambda b, j, p: (b, 0, j))]
    out_shape = [jax.ShapeDtypeStruct((bx, t, FFN_DIM), MXU_DTYPE),
                 jax.ShapeDtypeStruct((bx, 2, FFN_DIM), F32),
                 jax.ShapeDtypeStruct((bx, 2, FFN_DIM), F32)]
    scratch = [pltpu.VMEM((d, 2 * FFN_TILE), MXU_DTYPE), pltpu.VMEM((t + 8, 2 * FFN_TILE), F32)]
    sides = []
    if shift is not None:
        in_specs[0] = pl.BlockSpec((None, t, d), lambda b, j, p: (b, 0, 0), pipeline_mode=pl.Buffered(1))
    for cache_rows, src_seq0, new_rows in (shift or ()):
        n_seqs, total_rows = new_rows.shape[0], cache_rows.shape[1]
        deep = 2 * (total_rows - KV_ROWS) * HEAD_DIM * 4 <= SIDE_DEEP_BYTES
        side = _SideCopy(n_seqs, total_rows - KV_ROWS, src_seq0, bx * nt * parts, deep)
        sides.append(side)
        in_specs += [pl.BlockSpec(memory_space=pl.ANY), pl.BlockSpec(memory_space=pltpu.VMEM)]
        operands += [cache_rows, new_rows]
        out_specs.append(pl.BlockSpec(memory_space=pl.ANY))
        out_shape.append(jax.ShapeDtypeStruct((n_seqs, total_rows, HEAD_DIM), cache_rows.dtype))
    for side in sides:
        scratch += side.scratch_shapes()
    kern = functools.partial(_ffn_up_seq_kernel, t=t, chunk=chunk, parts=parts, sides=tuple(sides))
    outs = pl.pallas_call(
        kern,
        grid=(bx, nt, parts),
        in_specs=in_specs,
        out_specs=out_specs,
        out_shape=out_shape,
        scratch_shapes=scratch,
        compiler_params=_params("arbitrary", "arbitrary", "arbitrary"),
        name="ffn_up_sequence",
    )(*operands)
    rows = jnp.concatenate([outs[1], outs[2]], axis=-1)
    return (outs[0], rows) if shift is None else (outs[0], rows, list(outs[3:]))


def _ffn_up_step_kernel(a_ref, wg_ref, wv_ref, cwg_ref, cwv_ref, cbg_ref, cbv_ref, p0g_ref, p0v_ref,
                        p1g_ref, p1v_ref, g_ref, ug_ref, uv_ref):
    w = jnp.concatenate([wg_ref[...], wv_ref[...]], axis=-1).astype(MXU_DTYPE)
    u = jnp.dot(a_ref[...], w, preferred_element_type=F32)
    cw = jnp.concatenate([cwg_ref[...], cwv_ref[...]], axis=-1)
    cb = jnp.concatenate([cbg_ref[...], cbv_ref[...]], axis=-1)
    p0 = jnp.concatenate([p0g_ref[...], p0v_ref[...]], axis=-1)
    p1 = jnp.concatenate([p1g_ref[...], p1v_ref[...]], axis=-1)
    z = cb + cw[0:1] * p0
    z = z + cw[1:2] * p1
    z = z + cw[2:3] * u
    g_ref[...] = (_silu(z[:, :FFN_TILE]) * z[:, FFN_TILE:]).astype(g_ref.dtype)
    ug_ref[...] = u[:, :FFN_TILE]
    uv_ref[...] = u[:, FFN_TILE:]


def _ffn_up_step(h, w_up, conv_w, conv_b, prev0, prev1):
    b, d = h.shape
    nt = N_FFN_TILES
    conv_b = conv_b.reshape(1, 2 * FFN_DIM)
    lo = lambda j: (0, j)
    hi = lambda j: (0, nt + j)
    g, ug, uv = pl.pallas_call(
        _ffn_up_step_kernel,
        grid=(nt,),
        in_specs=[pl.BlockSpec((b, d), lambda j: (0, 0)),
                  pl.BlockSpec((d, FFN_TILE), lo), pl.BlockSpec((d, FFN_TILE), hi),
                  pl.BlockSpec((3, FFN_TILE), lo), pl.BlockSpec((3, FFN_TILE), hi),
                  pl.BlockSpec((1, FFN_TILE), lo), pl.BlockSpec((1, FFN_TILE), hi),
                  pl.BlockSpec((b, FFN_TILE), lo), pl.BlockSpec((b, FFN_TILE), hi),
                  pl.BlockSpec((b, FFN_TILE), lo), pl.BlockSpec((b, FFN_TILE), hi)],
        out_specs=[pl.BlockSpec((b, FFN_TILE), lo)] * 3,
        out_shape=[jax.ShapeDtypeStruct((b, FFN_DIM), MXU_DTYPE),
                   jax.ShapeDtypeStruct((b, FFN_DIM), F32),
                   jax.ShapeDtypeStruct((b, FFN_DIM), F32)],
        compiler_params=_params("arbitrary"),
        name="ffn_up_step",
    )(h, w_up, w_up, conv_w, conv_w, conv_b, conv_b, prev0, prev0, prev1, prev1)
    return g, jnp.concatenate([ug, uv], axis=-1)


def _qk_gains(q_gain, k_gain, g):
    return jnp.stack([q_gain[g], k_gain[g], jnp.ones_like(q_gain[g])])[:, None, :]


def _sequence_trunk(x, mods, w, tm, carried=None):
    bx, t, d = x.shape
    new_kv, new_ret, new_conv, updated = [[] for _ in range(N_GROUPS)], [], [], []
    for layer in range(DEPTH):
        m = mods[layer]
        h = _norm_mod(x, w["norm_mix"][layer], m, 0, 1, tm)
        if layer % N_MIXERS == 0:
            a = layer // N_MIXERS
            outs, lses = [], []
            for g in range(N_GROUPS):
                dil = DILATIONS[g]
                qkv = _qkv_project(h, w["attn_w_qkv"][a], _qk_gains(w["attn_q_gain"][a], w["attn_k_gain"][a], g),
                                   g, dil, tm)
                o, lse = _window_attention(qkv, _band_bias(w["rel_bias"], g), dil)
                outs.append(o)
                lses.append(lse)
                keep = min(WINDOWS[g], t)
                rows = keep // dil
                kv = qkv[:, 1:3, :, :, t // dil - rows:, :]
                kv = jnp.transpose(kv, (0, 4, 2, 1, 3, 5))
                new_kv[g].append(kv.reshape(bx, keep, 2, HEADS, HEAD_DIM))
            x = _attn_out(outs, lses, w["attn_w_o"][a], x, m, 2, min(tm, 256))
        else:
            r = layer // N_MIXERS
            proj = _matmul(h, w["ret_w_in"][r], min(2 * tm, t), 1024)
            cos_rep, sin_signed = _rotation_tables(jnp.arange(t, dtype=jnp.int32))
            y, state = _retention_sequence(proj, cos_rep, sin_signed, w["ret_gn_gain"][r])
            new_ret.append(state)
            x = _resid_matmul(y, w["ret_w_o"][r], x, m, 2, tm, 512)
        h = _norm_mod(x, w["norm_ffn"][layer], m, 3, 4, tm)
        ffn_w = (w["ffn_w_up"][layer], w["ffn_conv_w"][layer], w["ffn_conv_b"][layer])
        if carried is None or not carried[layer]:
            g_act, rows = _ffn_up_sequence(h, *ffn_w)
        else:
            shifts = []
            for cache, a, k_new, v_new in carried[layer]:
                nl, n_seqs, width = cache.shape[:3]
                cache_rows = cache.reshape(nl * n_seqs, width * KV_ROWS, HEAD_DIM)
                new_rows = jnp.concatenate([k_new, v_new], axis=1)
                shifts.append((cache_rows, a * n_seqs, new_rows))
            g_act, rows, upds = _ffn_up_sequence(h, *ffn_w, shift=shifts)
            updated += [u.reshape(c[0].shape[1:]) for u, c in zip(upds, carried[layer])]
        new_conv.append(rows)
        x = _resid_matmul(g_act, w["ffn_w_down"][layer], x, m, 5, tm, 512)
    return x, [jnp.stack(kv) for kv in new_kv], jnp.stack(new_ret), jnp.stack(new_conv), updated


def _step_qkv(x, m, w, layer):
    b = x.shape[1]
    a = layer // N_MIXERS
    h = _norm_mod(x, w["norm_mix"][layer], m, 0, 1, b)
    out = []
    for g in range(N_GROUPS):
        qkv = _qkv_project(h, w["attn_w_qkv"][a], _qk_gains(w["attn_q_gain"][a], w["attn_k_gain"][a], g), g, 1, b)
        out.append(jnp.transpose(qkv.reshape(3, HEADS, b, HEAD_DIM), (0, 2, 1, 3)))
    return out


def _step_trunk(x, mods, caches, ret_state, conv_state, position, w, qkv_first=None, updated=None):
    b, d = x.shape
    x = x[None]
    new_kv, new_ret, new_conv = [[] for _ in range(N_GROUPS)], [], []
    for layer in range(DEPTH):
        m = mods[layer]
        if layer % N_MIXERS == 0:
            a = layer // N_MIXERS
            qkvs = qkv_first if (layer == 0 and qkv_first is not None) else _step_qkv(x, m, w, layer)
            outs, lses = [], []
            for g in range(N_GROUPS):
                qkv = qkvs[g]
                o, lse = _step_attention(qkv[0], qkv[1], qkv[2], caches[g], a, _group_bias(w["rel_bias"], g),
                                         DILATIONS[g], b)
                outs.append(jnp.transpose(o, (1, 0, 2))[None])
                lses.append(jnp.transpose(lse, (1, 0, 2))[None])
                if layer == 0 and updated is not None and g in updated:
                    new_kv[g].append(updated[g])
                else:
                    new_kv[g].append(_cache_shift(caches[g], a, qkv[1], qkv[2]))
            x = _attn_out(outs, lses, w["attn_w_o"][a], x, m, 2, b)
        else:
            r = layer // N_MIXERS
            h = _norm_mod(x, w["norm_mix"][layer], m, 0, 1, b)
            proj = _matmul(h, w["ret_w_in"][r], b, 1024)[0]
            q = proj[:, :RET_QK_WIDTH].reshape(b, RET_HEADS, RET_DK)
            k = proj[:, RET_QK_WIDTH:2 * RET_QK_WIDTH].reshape(b, RET_HEADS, RET_DK)
            v = proj[:, 2 * RET_QK_WIDTH:2 * RET_QK_WIDTH + RET_V_WIDTH].reshape(b, RET_HEADS, RET_DV)
            gate = proj[:, 2 * RET_QK_WIDTH + RET_V_WIDTH:].reshape(b, RET_HEADS, RET_DV)
            cos_rep, sin_signed = _rotation_tables(position)
            y, state = _retention_step(q, k, v, gate, ret_state, r, cos_rep, sin_signed, w["ret_gn_gain"][r])
            new_ret.append(state)
            x = _resid_matmul(y.reshape(1, b, RET_V_WIDTH), w["ret_w_o"][r], x, m, 2, b, 512)
        h = _norm_mod(x, w["norm_ffn"][layer], m, 3, 4, b)
        g_act, u_new = _ffn_up_step(h[0], w["ffn_w_up"][layer], w["ffn_conv_w"][layer], w["ffn_conv_b"][layer],
                                    conv_state[layer, :, 0], conv_state[layer, :, 1])
        new_conv.append(jnp.stack([conv_state[layer, :, 1], u_new], axis=1))
        x = _resid_matmul(g_act[None], w["ffn_w_down"][layer], x, m, 5, b, 512)
    return x[0], [jnp.stack(kv) for kv in new_kv], jnp.stack(new_ret), jnp.stack(new_conv)


def kernel(x_prompt, x_sample, cache_attn_kv_w128, cache_attn_kv_w512, cache_attn_kv_w2048, state_ret,
           state_conv, c_prompt, c_sample, rel_bias, w_ada, b_ada, norm_mix, norm_ffn, attn_w_qkv,
           attn_q_gain, attn_k_gain, attn_w_o, ret_w_in, ret_gn_gain, ret_w_o, ffn_w_up, ffn_conv_w,
           ffn_conv_b, ffn_w_down):
    w = dict(rel_bias=rel_bias, norm_mix=norm_mix, norm_ffn=norm_ffn, attn_w_qkv=attn_w_qkv,
             attn_q_gain=attn_q_gain, attn_k_gain=attn_k_gain, attn_w_o=attn_w_o, ret_w_in=ret_w_in,
             ret_gn_gain=ret_gn_gain, ret_w_o=ret_w_o, ffn_w_up=ffn_w_up, ffn_conv_w=ffn_conv_w,
             ffn_conv_b=ffn_conv_b, ffn_w_down=ffn_w_down)
    bp, sp, d = x_prompt.shape
    bs = x_sample.shape[0]
    past_len = cache_attn_kv_w2048.shape[2]

    rows = bp + bs
    pad = (-rows) % 8
    c_all = jnp.concatenate([c_prompt, c_sample, jnp.zeros((pad, d), F32)], axis=0)
    mods = _ada_modulation(c_all, w_ada, b_ada)
    mods_p = mods[:, :bp].reshape(DEPTH, bp, 1, 6 * d)
    mods_s = mods[:, bp:rows].reshape(DEPTH, 1, bs, 6 * d)

    caches = (cache_attn_kv_w128, cache_attn_kv_w512, cache_attn_kv_w2048)
    x_s = x_sample[:, 0]
    qkv_first = _step_qkv(x_s[None], mods_s[0], w, 0)
    hosted = [[N_GROUPS - 1], list(range(N_GROUPS - 2, -1, -1))] + [[]] * (DEPTH - 2)
    carried = [[(caches[g], 0, qkv_first[g][1], qkv_first[g][2]) for g in groups] for groups in hosted]
    y_p, kv_p, ret_p, conv_p, updated = _sequence_trunk(x_prompt, mods_p, w, 512, carried)
    position = past_len + jnp.arange(1, dtype=jnp.int32)
    order = [g for groups in hosted for g in groups]
    y_s, kv_s, ret_s, conv_s = _step_trunk(x_s, mods_s, caches, state_ret, state_conv, position, w,
                                           qkv_first, dict(zip(order, updated)))

    return (y_p, y_s[:, None, :], kv_p[0], kv_p[1], kv_p[2], ret_p, conv_p,
            kv_s[0], kv_s[1], kv_s[2], ret_s, conv_s)
```

```python
import functools
import math

import jax
import jax.numpy as jnp
from jax import lax
from jax.experimental import pallas as pl
from jax.experimental.pallas import tpu as pltpu

D_MODEL = 2048
DEPTH = 2
N_MIXERS = 2

WINDOWS = (128, 512, 2048)
DILATIONS = (1, 4, 16)
N_GROUPS = 3
HEAD_DIM = 128
HEADS = 8
ATTN_WIDTH = HEADS * HEAD_DIM
N_KEYS = 129
KEY_BLOCK = 128
N_BUCKETS = 32
MAX_DISTANCE = 2048
NEG_INF = -1e30

RET_HEADS = 8
RET_DK = 256
RET_DV = 512
RET_QK_WIDTH = RET_HEADS * RET_DK
RET_V_WIDTH = RET_HEADS * RET_DV
RET_CHUNK = 128
RET_HEAD_GROUP = 2
ROT_BASE = 10000.0

FFN_DIM = 5504
FFN_TILE = 128
N_FFN_TILES = FFN_DIM // FFN_TILE
NORM_EPS = 1e-6

F32 = jnp.float32
MXU_DTYPE = jnp.bfloat16
VMEM_LIMIT_BYTES = 56 * 1024 * 1024


def _params(*semantics):
    return pltpu.CompilerParams(dimension_semantics=semantics, vmem_limit_bytes=VMEM_LIMIT_BYTES)


def _silu(x):
    return x * jax.nn.sigmoid(x)


def _ada_kernel(c_ref, w_ref, b_ref, o_ref):
    a = _silu(c_ref[...]).astype(MXU_DTYPE)
    w = w_ref[...].astype(MXU_DTYPE)
    o_ref[...] = jnp.dot(a, w, preferred_element_type=F32) + b_ref[...]


def _ada_modulation(c, w_ada, b_ada):
    rows = c.shape[0]
    tn = 1024
    return pl.pallas_call(
        _ada_kernel,
        grid=(DEPTH, 6 * D_MODEL // tn),
        in_specs=[pl.BlockSpec((rows, D_MODEL), lambda l, j: (0, 0)),
                  pl.BlockSpec((None, D_MODEL, tn), lambda l, j: (l, 0, j)),
                  pl.BlockSpec((None, 1, tn), lambda l, j: (l, 0, j))],
        out_specs=pl.BlockSpec((None, rows, tn), lambda l, j: (l, 0, j)),
        out_shape=jax.ShapeDtypeStruct((DEPTH, rows, 6 * D_MODEL), F32),
        compiler_params=_params("arbitrary", "arbitrary"),
        name="ada_modulation",
    )(c, w_ada, b_ada.reshape(DEPTH, 1, 6 * D_MODEL))


def _norm_mod_kernel(x_ref, g_ref, shift_ref, scale_ref, o_ref):
    x = x_ref[...]
    y = x * lax.rsqrt(jnp.mean(x * x, axis=-1, keepdims=True) + NORM_EPS) * g_ref[...]
    o_ref[...] = (y * (1.0 + scale_ref[...]) + shift_ref[...]).astype(o_ref.dtype)


def _mod_spec(mods, tm, col, n_lead):
    per_row = mods.shape[1] > 1
    rb = tm if per_row else 1

    def index(*ids):
        b, i = ids[n_lead], ids[n_lead + 1]
        return (b, i if per_row else 0, col)

    return pl.BlockSpec((None, rb, D_MODEL), index)


def _norm_mod(x, gain, mods, shift_col, scale_col, tm):
    bx, t, d = x.shape
    return pl.pallas_call(
        _norm_mod_kernel,
        grid=(bx, t // tm),
        in_specs=[pl.BlockSpec((None, tm, d), lambda b, i: (b, i, 0)),
                  pl.BlockSpec((1, d), lambda b, i: (0, 0)),
                  _mod_spec(mods, tm, shift_col, 0),
                  _mod_spec(mods, tm, scale_col, 0)],
        out_specs=pl.BlockSpec((None, tm, d), lambda b, i: (b, i, 0)),
        out_shape=jax.ShapeDtypeStruct((bx, t, d), MXU_DTYPE),
        compiler_params=_params("arbitrary", "arbitrary"),
        name="norm_mod",
    )(x, gain.reshape(1, d), mods, mods)


def _cast_weight_once(w_ref, wbf_ref):
    @pl.when((pl.program_id(1) == 0) & (pl.program_id(2) == 0))
    def _():
        wbf_ref[...] = w_ref[...].astype(wbf_ref.dtype)


def _matmul_kernel(a_ref, w_ref, o_ref, wbf_ref):
    _cast_weight_once(w_ref, wbf_ref)
    o_ref[...] = jnp.dot(a_ref[...], wbf_ref[...], preferred_element_type=F32).astype(o_ref.dtype)


def _matmul(a, w, tm, tn, out_dtype=F32):
    bx, t, k = a.shape
    n = w.shape[1]
    return pl.pallas_call(
        _matmul_kernel,
        grid=(n // tn, bx, t // tm),
        in_specs=[pl.BlockSpec((None, tm, k), lambda j, b, i: (b, i, 0)),
                  pl.BlockSpec((k, tn), lambda j, b, i: (0, j))],
        out_specs=pl.BlockSpec((None, tm, tn), lambda j, b, i: (b, i, j)),
        out_shape=jax.ShapeDtypeStruct((bx, t, n), out_dtype),
        scratch_shapes=[pltpu.VMEM((k, tn), MXU_DTYPE)],
        compiler_params=_params("arbitrary", "arbitrary", "arbitrary"),
        name="matmul",
    )(a, w)


def _resid_matmul_kernel(a_ref, w_ref, x_ref, gate_ref, o_ref, wbf_ref):
    _cast_weight_once(w_ref, wbf_ref)
    acc = jnp.dot(a_ref[...], wbf_ref[...], preferred_element_type=F32)
    o_ref[...] = x_ref[...] + gate_ref[...] * acc


def _resid_matmul(a, w, x, mods, gate_col, tm, tn):
    bx, t, k = a.shape
    n = w.shape[1]
    cols_per_group = D_MODEL // tn
    per_row = mods.shape[1] > 1
    rb = tm if per_row else 1
    gate_spec = pl.BlockSpec(
        (None, rb, tn), lambda j, b, i: (b, i if per_row else 0, gate_col * cols_per_group + j))
    return pl.pallas_call(
        _resid_matmul_kernel,
        grid=(n // tn, bx, t // tm),
        in_specs=[pl.BlockSpec((None, tm, k), lambda j, b, i: (b, i, 0)),
                  pl.BlockSpec((k, tn), lambda j, b, i: (0, j)),
                  pl.BlockSpec((None, tm, tn), lambda j, b, i: (b, i, j)),
                  gate_spec],
        out_specs=pl.BlockSpec((None, tm, tn), lambda j, b, i: (b, i, j)),
        out_shape=jax.ShapeDtypeStruct((bx, t, n), F32),
        scratch_shapes=[pltpu.VMEM((k, tn), MXU_DTYPE)],
        compiler_params=_params("arbitrary", "arbitrary", "arbitrary"),
        name="resid_matmul",
    )(a, w, x, mods)


def _qkv_kernel(a_ref, w_ref, gain_ref, o_ref, wbf_ref, y_ref, *, dilation, tm):
    _cast_weight_once(w_ref, wbf_ref)
    normed = pl.program_id(0) < 2
    gain = gain_ref[...]
    halves = 2 if tm % (16 * dilation) == 0 else 1
    hm = tm // halves
    rows = hm // dilation
    for half in range(halves):
        acc = jnp.dot(a_ref[half * hm:(half + 1) * hm, :], wbf_ref[...], preferred_element_type=F32)
        for h in range(HEADS):
            xh = acc[:, h * HEAD_DIM:(h + 1) * HEAD_DIM]
            yh = xh * lax.rsqrt(jnp.mean(xh * xh, axis=-1, keepdims=True) + NORM_EPS) * gain
            yh = jnp.where(normed, yh, xh)
            if dilation == 1:
                o_ref[0, h, half * hm:(half + 1) * hm, :] = yh
            else:
                y_ref[h, half * hm:(half + 1) * hm, :] = yh
        if dilation > 1:
            for c in range(dilation):
                for h in range(HEADS):
                    o_ref[c, h, half * rows:(half + 1) * rows, :] = (
                        y_ref[h, pl.ds(half * hm + c, rows, stride=dilation), :])


def _qkv_project(h, w_qkv, gains, group, dilation, tm):
    bx, t, d = h.shape
    td = t // dilation
    kern = functools.partial(_qkv_kernel, dilation=dilation, tm=tm)
    return pl.pallas_call(
        kern,
        grid=(3, bx, t // tm),
        in_specs=[pl.BlockSpec((None, tm, d), lambda p, b, i: (b, i, 0)),
                  pl.BlockSpec((d, ATTN_WIDTH), lambda p, b, i: (0, group * 3 + p)),
                  pl.BlockSpec((None, 1, HEAD_DIM), lambda p, b, i: (p, 0, 0))],
        out_specs=pl.BlockSpec((None, None, dilation, HEADS, tm // dilation, HEAD_DIM),
                               lambda p, b, i: (b, p, 0, 0, i, 0)),
        out_shape=jax.ShapeDtypeStruct((bx, 3, dilation, HEADS, td, HEAD_DIM), F32),
        scratch_shapes=[pltpu.VMEM((d, ATTN_WIDTH), MXU_DTYPE), pltpu.VMEM((HEADS, tm, HEAD_DIM), F32)],
        compiler_params=_params("arbitrary", "arbitrary", "arbitrary"),
        name=f"qkv_project_g{group}",
    )(h, w_qkv, gains)


def _t5_causal_bucket(dist):
    max_exact = N_BUCKETS // 2
    d = jnp.maximum(dist, 1).astype(F32)
    large = max_exact + (jnp.log(d / max_exact) / math.log(MAX_DISTANCE / max_exact)
                         * (N_BUCKETS - max_exact)).astype(jnp.int32)
    return jnp.where(dist < max_exact, dist, jnp.minimum(large, N_BUCKETS - 1))


def _bucket_lookup(rel_bias, g, bucket, fill):
    cols = rel_bias[:, g * HEADS:(g + 1) * HEADS].astype(F32)
    expand = (slice(None),) + (None,) * bucket.ndim
    out = jnp.full((HEADS,) + bucket.shape, fill, F32)
    for b in range(N_BUCKETS):
        out = jnp.where(bucket[None] == b, cols[b][expand], out)
    return out


def _group_bias(rel_bias, g):
    dist = DILATIONS[g] * jnp.arange(N_KEYS, dtype=jnp.int32)
    return _bucket_lookup(rel_bias, g, _t5_causal_bucket(dist), 0.0)


def _band_bias(rel_bias, g):
    a = jnp.arange(KEY_BLOCK, dtype=jnp.int32)[:, None]
    c = jnp.arange(2 * KEY_BLOCK, dtype=jnp.int32)[None, :]
    rel = a + KEY_BLOCK - c
    valid = (rel >= 0) & (rel <= KEY_BLOCK)
    bucket = jnp.where(valid, _t5_causal_bucket(DILATIONS[g] * jnp.clip(rel, 0, KEY_BLOCK)), -1)
    return _bucket_lookup(rel_bias, g, bucket, NEG_INF)


def _window_attn_kernel(q_ref, kp_ref, kc_ref, vp_ref, vc_ref, bias_ref, o_ref, lse_ref,
                        *, dilation, heads):
    i = pl.program_id(1)
    c = pl.program_id(3)
    col = lax.broadcasted_iota(jnp.int32, (KEY_BLOCK, 2 * KEY_BLOCK), 1)
    no_prev = (i == 0) & (col < KEY_BLOCK)
    for h in range(heads):
        q = q_ref[h].astype(MXU_DTYPE)
        k = jnp.concatenate([kp_ref[h], kc_ref[h]], axis=0).astype(MXU_DTYPE)
        v = jnp.concatenate([vp_ref[h], vc_ref[h]], axis=0).astype(MXU_DTYPE)
        s = lax.dot_general(q, k, (((1,), (1,)), ((), ())), preferred_element_type=F32)
        s = s * (HEAD_DIM ** -0.5) + bias_ref[h]
        s = jnp.where(no_prev, NEG_INF, s)
        m = jnp.max(s, axis=-1, keepdims=True)
        p = jnp.exp(s - m)
        l = jnp.sum(p, axis=-1, keepdims=True)
        o = jnp.dot(p.astype(MXU_DTYPE), v, preferred_element_type=F32) / l
        lse = jnp.broadcast_to(m + jnp.log(l), (KEY_BLOCK, HEAD_DIM))
        if dilation == 1:
            o_ref[h] = o
            lse_ref[h] = lse
        else:
            o_ref[h, pl.ds(c, KEY_BLOCK, stride=dilation), :] = o
            lse_ref[h, pl.ds(c, KEY_BLOCK, stride=dilation), :] = lse


def _window_attention(qkv, band_bias, dilation):
    bx, _, _, _, td, _ = qkv.shape
    t = td * dilation
    nblk = td // KEY_BLOCK
    hb = 1
    heads = HEADS // hb

    def spec(part, prev):
        def index(b, i, hq, c):
            return (b, part, c, hq, jnp.maximum(i - 1, 0) if prev else i, 0)
        return pl.BlockSpec((None, None, None, heads, KEY_BLOCK, HEAD_DIM), index)

    out_spec = pl.BlockSpec((None, heads, KEY_BLOCK * dilation, HEAD_DIM), lambda b, i, hq, c: (b, hq, i, 0))
    kern = functools.partial(_window_attn_kernel, dilation=dilation, heads=heads)
    return pl.pallas_call(
        kern,
        grid=(bx, nblk, hb, dilation),
        in_specs=[spec(0, False), spec(1, True), spec(1, False), spec(2, True), spec(2, False),
                  pl.BlockSpec((heads, KEY_BLOCK, 2 * KEY_BLOCK), lambda b, i, hq, c: (hq, 0, 0))],
        out_specs=[out_spec, out_spec],
        out_shape=[jax.ShapeDtypeStruct((bx, HEADS, t, HEAD_DIM), F32)] * 2,
        compiler_params=_params("arbitrary", "arbitrary", "arbitrary", "arbitrary"),
        name=f"window_attention_d{dilation}",
    )(qkv, qkv, qkv, qkv, qkv, band_bias)


STEP_ROWS = 8
STEP_SEQS = 8


def _step_attn_kernel(*refs):
    q_ref, kn_ref, vn_ref = refs[:3]
    kv_refs = refs[3:3 + STEP_ROWS]
    bias0_ref, bias_ref, o_ref, lse_ref, m_ref, l_ref, acc_ref = refs[3 + STEP_ROWS:]
    j = pl.program_id(1)
    scale = HEAD_DIM ** -0.5

    def chunk(ci, carry):
        sl = pl.ds(pl.multiple_of(ci * STEP_SEQS, STEP_SEQS), STEP_SEQS)
        q = q_ref[sl]

        @pl.when(j == 0)
        def _():
            s0 = jnp.sum(q * kn_ref[sl], axis=-1, keepdims=True) * scale + bias0_ref[...]
            m_ref[sl] = jnp.broadcast_to(s0, q.shape)
            l_ref[sl] = jnp.ones(q.shape, F32)
            acc_ref[sl] = vn_ref[sl]

        scores = [jnp.sum(q * kv_refs[r][sl, 0], axis=-1, keepdims=True) * scale + bias_ref[r]
                  for r in range(STEP_ROWS)]
        m_old = m_ref[sl]
        m_new = m_old
        for s in scores:
            m_new = jnp.maximum(m_new, s)
        alpha = jnp.exp(m_old - m_new)
        l_new = alpha * l_ref[sl]
        acc = alpha * acc_ref[sl]
        for r in range(STEP_ROWS):
            p = jnp.exp(scores[r] - m_new)
            l_new = l_new + p
            acc = acc + p * kv_refs[r][sl, 1]
        m_ref[sl] = m_new
        l_ref[sl] = l_new
        acc_ref[sl] = acc
        return carry

    n_chunks = q_ref.shape[0] // STEP_SEQS
    lax.fori_loop(0, n_chunks, chunk, 0, unroll=2 if n_chunks % 2 == 0 else 1)

    @pl.when(j == pl.num_programs(1) - 1)
    def _():
        o_ref[...] = acc_ref[...] / l_ref[...]
        lse_ref[...] = m_ref[...] + jnp.log(l_ref[...])


def _step_attention(q, k_new, v_new, cache, layer, gb, dilation, bt):
    b = q.shape[0]
    nk = N_KEYS - 1
    lanes = jnp.broadcast_to(gb.T[:, :, None], (N_KEYS, HEADS, HEAD_DIM))
    bias0 = lanes[0]
    bias_rows = lanes[:0:-1]
    qspec = pl.BlockSpec((bt, HEADS, HEAD_DIM), lambda bi, j: (bi, 0, 0))

    def row_spec(r):
        return pl.BlockSpec((None, bt, None, 2, HEADS, HEAD_DIM),
                            lambda bi, j: (layer, bi, (j * STEP_ROWS + r) * dilation, 0, 0, 0))

    return pl.pallas_call(
        _step_attn_kernel,
        grid=(b // bt, nk // STEP_ROWS),
        in_specs=[qspec, qspec, qspec] + [row_spec(r) for r in range(STEP_ROWS)] + [
            pl.BlockSpec((HEADS, HEAD_DIM), lambda bi, j: (0, 0)),
            pl.BlockSpec((STEP_ROWS, HEADS, HEAD_DIM), lambda bi, j: (j, 0, 0))],
        out_specs=[qspec, qspec],
        out_shape=[jax.ShapeDtypeStruct((b, HEADS, HEAD_DIM), F32)] * 2,
        scratch_shapes=[pltpu.VMEM((bt, HEADS, HEAD_DIM), F32)] * 3,
        compiler_params=_params("arbitrary", "arbitrary"),
        name=f"step_attention_d{dilation}",
    )(q, k_new, v_new, *([cache] * STEP_ROWS), bias0, bias_rows)


def _attn_out_kernel(o0_ref, o1_ref, o2_ref, l0_ref, l1_ref, l2_ref, w_ref, x_ref, gate_ref, out_ref,
                     wbf_ref):
    _cast_weight_once(w_ref, wbf_ref)
    heads = []
    for h in range(HEADS):
        l0, l1, l2 = l0_ref[h], l1_ref[h], l2_ref[h]
        m = jnp.maximum(jnp.maximum(l0, l1), l2)
        e0, e1, e2 = jnp.exp(l0 - m), jnp.exp(l1 - m), jnp.exp(l2 - m)
        tot = e0 + e1 + e2
        merged = (e0 / tot) * o0_ref[h] + (e1 / tot) * o1_ref[h] + (e2 / tot) * o2_ref[h]
        heads.append(merged.astype(MXU_DTYPE))
    acc = jnp.dot(jnp.concatenate(heads, axis=-1), wbf_ref[...], preferred_element_type=F32)
    out_ref[...] = x_ref[...] + gate_ref[...] * acc


def _attn_out(outs, lses, w_o, x, mods, gate_col, tm):
    bx, t, d = x.shape
    per_row = mods.shape[1] > 1
    rb = tm if per_row else 1
    aspec = pl.BlockSpec((None, HEADS, tm, HEAD_DIM), lambda j, b, i: (b, 0, i, 0))
    xspec = pl.BlockSpec((None, tm, d), lambda j, b, i: (b, i, 0))
    return pl.pallas_call(
        _attn_out_kernel,
        grid=(1, bx, t // tm),
        in_specs=[aspec] * 6 + [
            pl.BlockSpec((ATTN_WIDTH, d), lambda j, b, i: (0, 0)),
            xspec,
            pl.BlockSpec((None, rb, d), lambda j, b, i: (b, i if per_row else 0, gate_col))],
        out_specs=xspec,
        out_shape=jax.ShapeDtypeStruct((bx, t, d), F32),
        scratch_shapes=[pltpu.VMEM((ATTN_WIDTH, d), MXU_DTYPE)],
        compiler_params=_params("arbitrary", "arbitrary", "arbitrary"),
        name="attn_out",
    )(*outs, *lses, w_o, x, mods)


KV_ROWS = 2 * HEADS
SHIFT_BUFFERS = 3
SHIFT_CHUNK_BYTES = 8 * 1024 * 1024


def _shift_chunking(b, keep_rows):
    row_bytes = HEAD_DIM * 4
    pieces = 1
    while (keep_rows // pieces) * row_bytes > SHIFT_CHUNK_BYTES or keep_rows % (8 * pieces):
        pieces += 1
    rows = keep_rows // pieces
    seqs = 1
    while seqs * 2 * rows * row_bytes <= SHIFT_CHUNK_BYTES // 2 and b % (seqs * 2) == 0:
        seqs *= 2
    return seqs, rows, pieces


def _cache_shift_kernel(prev_ref, new_ref, out_ref, buf_ref, in_sems, out_sems, new_sem,
                        *, layer, seqs, rows, pieces):
    b, total_rows, _ = out_ref.shape
    n_chunks = (b // seqs) * pieces

    def load(k, slot):
        block, piece = k // pieces, k % pieces
        src = prev_ref.at[pl.ds(layer * b + block * seqs, seqs), pl.ds(KV_ROWS + piece * rows, rows)]
        return pltpu.make_async_copy(src, buf_ref.at[slot], in_sems.at[slot])

    def store(k, slot):
        block, piece = k // pieces, k % pieces
        dst = out_ref.at[pl.ds(block * seqs, seqs), pl.ds(piece * rows, rows)]
        return pltpu.make_async_copy(buf_ref.at[slot], dst, out_sems.at[slot])

    new_copy = pltpu.make_async_copy(new_ref, out_ref.at[:, pl.ds(total_rows - KV_ROWS, KV_ROWS)], new_sem)
    new_copy.start()
    load(0, 0).start()

    def step(k, carry):
        slot = k % SHIFT_BUFFERS
        nxt = k + 1
        nxt_slot = nxt % SHIFT_BUFFERS

        @pl.when(nxt < n_chunks)
        def _():
            @pl.when(nxt >= SHIFT_BUFFERS)
            def _():
                store(nxt - SHIFT_BUFFERS, nxt_slot).wait()
            load(nxt, nxt_slot).start()

        load(k, slot).wait()
        store(k, slot).start()
        return carry

    lax.fori_loop(0, n_chunks, step, 0)
    for k in range(max(0, n_chunks - SHIFT_BUFFERS), n_chunks):
        store(k, k % SHIFT_BUFFERS).wait()
    new_copy.wait()


def _cache_shift(prev, layer, k_new, v_new):
    nl, b, w = prev.shape[:3]
    prev_rows = prev.reshape(nl * b, w * KV_ROWS, HEAD_DIM)
    new_rows = jnp.concatenate([k_new, v_new], axis=1)
    seqs, rows, pieces = _shift_chunking(b, (w - 1) * KV_ROWS)
    out = pl.pallas_call(
        functools.partial(_cache_shift_kernel, layer=layer, seqs=seqs, rows=rows, pieces=pieces),
        in_specs=[pl.BlockSpec(memory_space=pl.ANY),
                  pl.BlockSpec(memory_space=pltpu.VMEM)],
        out_specs=pl.BlockSpec(memory_space=pl.ANY),
        out_shape=jax.ShapeDtypeStruct((b, w * KV_ROWS, HEAD_DIM), prev.dtype),
        scratch_shapes=[pltpu.VMEM((SHIFT_BUFFERS, seqs, rows, HEAD_DIM), prev.dtype),
                        pltpu.SemaphoreType.DMA((SHIFT_BUFFERS,)),
                        pltpu.SemaphoreType.DMA((SHIFT_BUFFERS,)),
                        pltpu.SemaphoreType.DMA(())],
        compiler_params=pltpu.CompilerParams(vmem_limit_bytes=VMEM_LIMIT_BYTES),
        name="cache_shift",
    )(prev_rows, new_rows)
    return out.reshape(b, w, 2, HEADS, HEAD_DIM)


def _rotation_tables(positions):
    half = RET_DK // 2
    inv_freq = 1.0 / (ROT_BASE ** jnp.linspace(0.0, 1.0, half, dtype=F32))
    ang = positions.astype(F32)[:, None] * inv_freq[None, :]
    cos, sin = jnp.cos(ang), jnp.sin(ang)
    cos_rep = jnp.stack([cos, cos], axis=-1).reshape(-1, RET_DK)
    sin_signed = jnp.stack([-sin, sin], axis=-1).reshape(-1, RET_DK)
    return cos_rep, sin_signed


def _rotate_pairs(x, cos_rep, sin_signed):
    lanes = 128
    even = lax.broadcasted_iota(jnp.int32, (x.shape[0], lanes), 1) % 2 == 0
    parts = []
    for s in range(x.shape[1] // lanes):
        xs = x[:, s * lanes:(s + 1) * lanes]
        parts.append(jnp.where(even, pltpu.roll(xs, lanes - 1, 1), pltpu.roll(xs, 1, 1)))
    swapped = jnp.concatenate(parts, axis=-1)
    return x * cos_rep + swapped * sin_signed


def _log_gamma():
    return jnp.log1p(-jnp.exp2(-5.0 - jnp.arange(RET_HEADS, dtype=F32)))


def _group_norm_gate(o, gain, gate):
    mu = jnp.mean(o, axis=-1, keepdims=True)
    var = jnp.mean(jnp.square(o - mu), axis=-1, keepdims=True)
    return _silu(gate) * ((o - mu) * lax.rsqrt(var + NORM_EPS) * gain)


def _retention_kernel(q_ref, k_ref, v_ref, gate_ref, cos_ref, sin_ref, decay_ref, qdec_ref, kdec_ref,
                      cdec_ref, gain_ref, y_ref, s_out_ref, s_ref):
    c = pl.program_id(1)
    nb, hg = s_ref.shape[:2]

    @pl.when(c == 0)
    def _():
        s_ref[...] = jnp.zeros(s_ref.shape, F32)

    cos, sin = cos_ref[...], sin_ref[...]
    for b in range(nb):
        for hh in range(hg):
            ksl = slice(hh * RET_DK, (hh + 1) * RET_DK)
            vsl = slice(hh * RET_DV, (hh + 1) * RET_DV)
            q = _rotate_pairs(q_ref[b, :, ksl], cos, sin)
            k = _rotate_pairs(k_ref[b, :, ksl], cos, sin) * (RET_DK ** -0.5)
            v = v_ref[b, :, vsl].astype(MXU_DTYPE)
            state = s_ref[b, hh]
            scores = lax.dot_general(q.astype(MXU_DTYPE), k.astype(MXU_DTYPE), (((1,), (1,)), ((), ())),
                                     preferred_element_type=F32) * decay_ref[hh]
            o = jnp.dot(scores.astype(MXU_DTYPE), v, preferred_element_type=F32)
            o = o + jnp.dot((q * qdec_ref[hh]).astype(MXU_DTYPE), state.astype(MXU_DTYPE),
                            preferred_element_type=F32)
            kd_t = jnp.transpose(k * kdec_ref[hh]).astype(MXU_DTYPE)
            s_ref[b, hh] = cdec_ref[hh] * state + jnp.dot(kd_t, v, preferred_element_type=F32)
            y_ref[b, :, vsl] = _group_norm_gate(o, gain_ref[hh], gate_ref[b, :, vsl]).astype(y_ref.dtype)

    @pl.when(c == pl.num_programs(1) - 1)
    def _():
        s_out_ref[...] = s_ref[...]


def _retention_sequence(proj, cos_rep, sin_signed, gn_gain):
    bx, t, _ = proj.shape
    cw = RET_CHUNK
    lg = _log_gamma()
    pos = jnp.arange(cw, dtype=F32)
    diff = pos[:, None] - pos[None, :]
    decay = jnp.where(diff >= 0, jnp.exp(diff[None] * lg[:, None, None]), 0.0)
    q_decay = jnp.exp((pos[:, None] + 1.0) * lg[None, :]).T[:, :, None]
    k_decay = jnp.exp((cw - 1.0 - pos)[:, None] * lg[None, :]).T[:, :, None]
    chunk_decay = jnp.exp(cw * lg).reshape(RET_HEADS, 1, 1)
    hg = RET_HEAD_GROUP
    nqk = RET_QK_WIDTH // (hg * RET_DK)
    nv0 = 2 * RET_QK_WIDTH // (hg * RET_DV)
    ng0 = nv0 + RET_HEADS // hg
    return pl.pallas_call(
        _retention_kernel,
        grid=(RET_HEADS // hg, t // cw),
        in_specs=[pl.BlockSpec((bx, cw, hg * RET_DK), lambda h, c: (0, c, h)),
                  pl.BlockSpec((bx, cw, hg * RET_DK), lambda h, c: (0, c, nqk + h)),
                  pl.BlockSpec((bx, cw, hg * RET_DV), lambda h, c: (0, c, nv0 + h)),
                  pl.BlockSpec((bx, cw, hg * RET_DV), lambda h, c: (0, c, ng0 + h)),
                  pl.BlockSpec((cw, RET_DK), lambda h, c: (c, 0)),
                  pl.BlockSpec((cw, RET_DK), lambda h, c: (c, 0)),
                  pl.BlockSpec((hg, cw, cw), lambda h, c: (h, 0, 0)),
                  pl.BlockSpec((hg, cw, 1), lambda h, c: (h, 0, 0)),
                  pl.BlockSpec((hg, cw, 1), lambda h, c: (h, 0, 0)),
                  pl.BlockSpec((hg, 1, 1), lambda h, c: (h, 0, 0)),
                  pl.BlockSpec((hg, 1, RET_DV), lambda h, c: (h, 0, 0))],
        out_specs=[pl.BlockSpec((bx, cw, hg * RET_DV), lambda h, c: (0, c, h)),
                   pl.BlockSpec((bx, hg, RET_DK, RET_DV), lambda h, c: (0, h, 0, 0))],
        out_shape=[jax.ShapeDtypeStruct((bx, t, RET_V_WIDTH), MXU_DTYPE),
                   jax.ShapeDtypeStruct((bx, RET_HEADS, RET_DK, RET_DV), F32)],
        scratch_shapes=[pltpu.VMEM((bx, hg, RET_DK, RET_DV), F32)],
        compiler_params=_params("arbitrary", "arbitrary"),
        name="retention_sequence",
    )(proj, proj, proj, proj, cos_rep, sin_signed, decay, q_decay, k_decay, chunk_decay,
      gn_gain.reshape(RET_HEADS, 1, RET_DV))


def _retention_step_kernel(q_ref, k_ref, v_ref, gate_ref, cos_ref, sin_ref, gamma_ref, gain_ref, s_ref,
                           y_ref, s_out_ref):
    cos, sin = cos_ref[...], sin_ref[...]
    q = _rotate_pairs(q_ref[...], cos, sin)
    k = _rotate_pairs(k_ref[...], cos, sin) * (RET_DK ** -0.5)
    qk = jnp.sum(q * k, axis=-1, keepdims=True)
    q_t = jnp.transpose(q * gamma_ref[...])
    k_t = jnp.transpose(k)
    v = v_ref[...]
    gamma = gamma_ref[...]
    rows = []
    for h in range(RET_HEADS):
        state = s_ref[h]
        vh = v[h:h + 1, :]
        rows.append(qk[h:h + 1, :] * vh + jnp.sum(q_t[:, h:h + 1] * state, axis=0, keepdims=True))
        s_out_ref[h] = gamma[h:h + 1, :] * state + k_t[:, h:h + 1] * vh
    o = jnp.concatenate(rows, axis=0)
    y_ref[...] = _group_norm_gate(o, gain_ref[...], gate_ref[...]).astype(y_ref.dtype)


def _retention_step(q, k, v, gate, state, layer, cos_rep, sin_signed, gn_gain):
    b = q.shape[0]
    gamma = jnp.exp(_log_gamma()).reshape(RET_HEADS, 1)
    qspec = pl.BlockSpec((None, RET_HEADS, RET_DK), lambda i: (i, 0, 0))
    vspec = pl.BlockSpec((None, RET_HEADS, RET_DV), lambda i: (i, 0, 0))
    sspec = pl.BlockSpec((None, RET_HEADS, RET_DK, RET_DV), lambda i: (i, 0, 0, 0))
    sspec_in = pl.BlockSpec((None, None, RET_HEADS, RET_DK, RET_DV), lambda i: (layer, i, 0, 0, 0))
    return pl.pallas_call(
        _retention_step_kernel,
        grid=(b,),
        in_specs=[qspec, qspec, vspec, vspec,
                  pl.BlockSpec((1, RET_DK), lambda i: (0, 0)),
                  pl.BlockSpec((1, RET_DK), lambda i: (0, 0)),
                  pl.BlockSpec((RET_HEADS, 1), lambda i: (0, 0)),
                  pl.BlockSpec((RET_HEADS, RET_DV), lambda i: (0, 0)),
                  sspec_in],
        out_specs=[vspec, sspec],
        out_shape=[jax.ShapeDtypeStruct((b, RET_HEADS, RET_DV), MXU_DTYPE),
                   jax.ShapeDtypeStruct(state.shape[1:], F32)],
        compiler_params=_params("arbitrary"),
        name="retention_step",
    )(q, k, v, gate, cos_rep, sin_signed, gamma, gn_gain.reshape(RET_HEADS, RET_DV), state)


SIDE_POINTS = 2
SIDE_CHUNK_BYTES = 512 * 1024
SIDE_DEEP_BYTES = 10 * 1024 * 1024


class _SideCopy:
    def __init__(self, n_seqs, keep_rows, src_seq0, n_steps, deep):
        assert n_seqs + 1 < n_steps
        row_bytes = HEAD_DIM * 4
        pieces = SIDE_POINTS
        while keep_rows % (8 * pieces) or (keep_rows // pieces) * row_bytes > SIDE_CHUNK_BYTES:
            pieces += SIDE_POINTS
        self.rows = keep_rows // pieces
        self.per_point = pieces // SIDE_POINTS
        self.n_seqs, self.src_seq0, self.deep = n_seqs, src_seq0, deep

    def scratch_shapes(self):
        n = SIDE_POINTS * self.per_point * (2 if self.deep else 1)
        return [pltpu.VMEM((n, self.rows, HEAD_DIM), F32),
                pltpu.SemaphoreType.DMA((n,)), pltpu.SemaphoreType.DMA((n,)), pltpu.SemaphoreType.DMA(())]

    def bind(self, src_ref, new_ref, dst_ref, buf_ref, load_sems, store_sems, new_sem):
        self.refs = (src_ref, dst_ref, buf_ref, load_sems, store_sems)
        self.new = (new_ref, new_sem)

    def _slot(self, seq, piece):
        return piece + (seq % 2) * SIDE_POINTS * self.per_point if self.deep else piece

    def _load(self, seq, point, q):
        src_ref, _, buf_ref, load_sems, _ = self.refs
        piece = point * self.per_point + q
        slot = self._slot(seq, piece)
        src = src_ref.at[self.src_seq0 + seq, pl.ds(KV_ROWS + piece * self.rows, self.rows)]
        return pltpu.make_async_copy(src, buf_ref.at[slot], load_sems.at[slot])

    def _store(self, seq, point, q):
        _, dst_ref, buf_ref, _, store_sems = self.refs
        piece = point * self.per_point + q
        slot = self._slot(seq, piece)
        dst = dst_ref.at[seq, pl.ds(piece * self.rows, self.rows)]
        return pltpu.make_async_copy(buf_ref.at[slot], dst, store_sems.at[slot])

    def serve(self, step, point):
        if self.deep:
            out_step, out_point = step - 1, point
        else:
            out_step, out_point = (step if point > 0 else step - 1), (point - 1) % SIDE_POINTS
        retire_step = out_step - 1 if self.deep else step - 1

        @pl.when((out_step >= 0) & (out_step < self.n_seqs))
        def _():
            for q in range(self.per_point):
                self._load(out_step, out_point, q).wait()
                self._store(out_step, out_point, q).start()

        @pl.when((retire_step >= 0) & (retire_step < self.n_seqs))
        def _():
            for q in range(self.per_point):
                self._store(retire_step, point, q).wait()

        @pl.when(step < self.n_seqs)
        def _():
            for q in range(self.per_point):
                self._load(step, point, q).start()

    def append_new_rows(self):
        new_ref, new_sem = self.new
        dst_ref = self.refs[1]
        total_rows = dst_ref.shape[1]
        copy = pltpu.make_async_copy(new_ref, dst_ref.at[:, pl.ds(total_rows - KV_ROWS, KV_ROWS)], new_sem)
        copy.start()
        copy.wait()


def _ffn_up_seq_kernel(*refs, t, chunk, parts, sides):
    a_ref, wg_ref, wv_ref, cwg_ref, cwv_ref, cbg_ref, cbv_ref = refs[:7]
    ns = len(sides)
    side_in = refs[7:7 + 2 * ns]
    g_ref, rows_g_ref, rows_v_ref = refs[7 + 2 * ns:10 + 2 * ns]
    side_out = refs[10 + 2 * ns:10 + 3 * ns]
    wbf_ref, u_ref = refs[10 + 3 * ns:12 + 3 * ns]
    side_scratch = refs[12 + 3 * ns:]
    for i, side in enumerate(sides):
        side.bind(side_in[2 * i], side_in[2 * i + 1], side_out[i], *side_scratch[4 * i:4 * i + 4])
    part_id = pl.program_id(2)
    step = (pl.program_id(0) * pl.num_programs(1) + pl.program_id(1)) * parts + part_id
    n_steps = pl.num_programs(0) * pl.num_programs(1) * parts

    @pl.when(part_id == 0)
    def _():
        wbf_ref[:, :FFN_TILE] = wg_ref[...].astype(wbf_ref.dtype)
        wbf_ref[:, FFN_TILE:] = wv_ref[...].astype(wbf_ref.dtype)
        u_ref[0:8, :] = jnp.zeros((8, 2 * FFN_TILE), F32)

    cw = jnp.concatenate([cwg_ref[...], cwv_ref[...]], axis=-1)
    cb = jnp.concatenate([cbg_ref[...], cbv_ref[...]], axis=-1)
    part_rows = t // parts
    n_chunks = part_rows // chunk
    stride = max(n_chunks // SIDE_POINTS, 1)

    def rows_of_part(part):
        for s in range(n_chunks):
            r0 = part * part_rows + s * chunk
            o0 = s * chunk
            if s % stride == 0 and s // stride < SIDE_POINTS:
                for side in sides:
                    side.serve(step, s // stride)
            u = jnp.dot(a_ref[r0:r0 + chunk, :], wbf_ref[...], preferred_element_type=F32)
            u_ref[8 + r0:8 + r0 + chunk, :] = u
            z = cb + cw[0:1] * u_ref[6 + r0:6 + r0 + chunk, :]
            z = z + cw[1:2] * u_ref[7 + r0:7 + r0 + chunk, :]
            z = z + cw[2:3] * u
            g_ref[o0:o0 + chunk, :] = (_silu(z[:, :FFN_TILE]) * z[:, FFN_TILE:]).astype(g_ref.dtype)

    for part in range(parts):
        pl.when(part_id == part)(functools.partial(rows_of_part, part))

    @pl.when(part_id == parts - 1)
    def _():
        rows_g_ref[...] = u_ref[6 + t:8 + t, :FFN_TILE]
        rows_v_ref[...] = u_ref[6 + t:8 + t, FFN_TILE:]

    if sides:
        @pl.when(step == n_steps - 1)
        def _():
            for side in sides:
                side.append_new_rows()


def _ffn_up_sequence(h, w_up, conv_w, conv_b, shift=None):
    bx, t, d = h.shape
    nt = N_FFN_TILES
    parts = 2 if (shift is not None and t % 1024 == 0) else 1
    chunk = min(t // parts, 512)
    conv_b = conv_b.reshape(1, 2 * FFN_DIM)
    in_specs = [pl.BlockSpec((None, t, d), lambda b, j, p: (b, 0, 0)),
                pl.BlockSpec((d, FFN_TILE), lambda b, j, p: (0, j)),
                pl.BlockSpec((d, FFN_TILE), lambda b, j, p: (0, nt + j)),
                pl.BlockSpec((3, FFN_TILE), lambda b, j, p: (0, j)),
                pl.BlockSpec((3, FFN_TILE), lambda b, j, p: (0, nt + j)),
                pl.BlockSpec((1, FFN_TILE), lambda b, j, p: (0, j)),
                pl.BlockSpec((1, FFN_TILE), lambda b, j, p: (0, nt + j))]
    operands = [h, w_up, w_up, conv_w, conv_w, conv_b, conv_b]
    out_specs = [pl.BlockSpec((None, t // parts, FFN_TILE), lambda b, j, p: (b, p, j)),
                 pl.BlockSpec((None, 2, FFN_TILE), lambda b, j, p: (b, 0, j)),
                 pl.BlockSpec((None, 2, FFN_TILE), lambda b, j, p: (b, 0, j))]
    out_shape = [jax.ShapeDtypeStruct((bx, t, FFN_DIM), MXU_DTYPE),
                 jax.ShapeDtypeStruct((bx, 2, FFN_DIM), F32),
                 jax.ShapeDtypeStruct((bx, 2, FFN_DIM), F32)]
    scratch = [pltpu.VMEM((d, 2 * FFN_TILE), MXU_DTYPE), pltpu.VMEM((t + 8, 2 * FFN_TILE), F32)]
    sides = []
    if shift is not None:
        in_specs[0] = pl.BlockSpec((None, t, d), lambda b, j, p: (b, 0, 0), pipeline_mode=pl.Buffered(1))
    for cache_rows, src_seq0, new_rows in (shift or ()):
        n_seqs, total_rows = new_rows.shape[0], cache_rows.shape[1]
        deep = 2 * (total_rows - KV_ROWS) * HEAD_DIM * 4 <= SIDE_DEEP_BYTES
        side = _SideCopy(n_seqs, total_rows - KV_ROWS, src_seq0, bx * nt * parts, deep)
        sides.append(side)
        in_specs += [pl.BlockSpec(memory_space=pl.ANY), pl.BlockSpec(memory_space=pltpu.VMEM)]
        operands += [cache_rows, new_rows]
        out_specs.append(pl.BlockSpec(memory_space=pl.ANY))
        out_shape.append(jax.ShapeDtypeStruct((n_seqs, total_rows, HEAD_DIM), cache_rows.dtype))
    for side in sides:
        scratch += side.scratch_shapes()
    kern = functools.partial(_ffn_up_seq_kernel, t=t, chunk=chunk, parts=parts, sides=tuple(sides))
    outs = pl.pallas_call(
        kern,
        grid=(bx, nt, parts),
        in_specs=in_specs,
        out_specs=out_specs,
        out_shape=out_shape,
        scratch_shapes=scratch,
        compiler_params=_params("arbitrary", "arbitrary", "arbitrary"),
        name="ffn_up_sequence",
    )(*operands)
    rows = jnp.concatenate([outs[1], outs[2]], axis=-1)
    return (outs[0], rows) if shift is None else (outs[0], rows, list(outs[3:]))


def _ffn_up_step_kernel(a_ref, wg_ref, wv_ref, cwg_ref, cwv_ref, cbg_ref, cbv_ref, p0g_ref, p0v_ref,
                        p1g_ref, p1v_ref, g_ref, ug_ref, uv_ref):
    w = jnp.concatenate([wg_ref[...], wv_ref[...]], axis=-1).astype(MXU_DTYPE)
    u = jnp.dot(a_ref[...], w, preferred_element_type=F32)
    cw = jnp.concatenate([cwg_ref[...], cwv_ref[...]], axis=-1)
    cb = jnp.concatenate([cbg_ref[...], cbv_ref[...]], axis=-1)
    p0 = jnp.concatenate([p0g_ref[...], p0v_ref[...]], axis=-1)
    p1 = jnp.concatenate([p1g_ref[...], p1v_ref[...]], axis=-1)
    z = cb + cw[0:1] * p0
    z = z + cw[1:2] * p1
    z = z + cw[2:3] * u
    g_ref[...] = (_silu(z[:, :FFN_TILE]) * z[:, FFN_TILE:]).astype(g_ref.dtype)
    ug_ref[...] = u[:, :FFN_TILE]
    uv_ref[...] = u[:, FFN_TILE:]


def _ffn_up_step(h, w_up, conv_w, conv_b, prev0, prev1):
    b, d = h.shape
    nt = N_FFN_TILES
    conv_b = conv_b.reshape(1, 2 * FFN_DIM)
    lo = lambda j: (0, j)
    hi = lambda j: (0, nt + j)
    g, ug, uv = pl.pallas_call(
        _ffn_up_step_kernel,
        grid=(nt,),
        in_specs=[pl.BlockSpec((b, d), lambda j: (0, 0)),
                  pl.BlockSpec((d, FFN_TILE), lo), pl.BlockSpec((d, FFN_TILE), hi),
                  pl.BlockSpec((3, FFN_TILE), lo), pl.BlockSpec((3, FFN_TILE), hi),
                  pl.BlockSpec((1, FFN_TILE), lo), pl.BlockSpec((1, FFN_TILE), hi),
                  pl.BlockSpec((b, FFN_TILE), lo), pl.BlockSpec((b, FFN_TILE), hi),
                  pl.BlockSpec((b, FFN_TILE), lo), pl.BlockSpec((b, FFN_TILE), hi)],
        out_specs=[pl.BlockSpec((b, FFN_TILE), lo)] * 3,
        out_shape=[jax.ShapeDtypeStruct((b, FFN_DIM), MXU_DTYPE),
                   jax.ShapeDtypeStruct((b, FFN_DIM), F32),
                   jax.ShapeDtypeStruct((b, FFN_DIM), F32)],
        compiler_params=_params("arbitrary"),
        name="ffn_up_step",
    )(h, w_up, w_up, conv_w, conv_w, conv_b, conv_b, prev0, prev0, prev1, prev1)
    return g, jnp.concatenate([ug, uv], axis=-1)


def _qk_gains(q_gain, k_gain, g):
    return jnp.stack([q_gain[g], k_gain[g], jnp.ones_like(q_gain[g])])[:, None, :]


def _sequence_trunk(x, mods, w, tm, carried=None):
    bx, t, d = x.shape
    new_kv, new_ret, new_conv, updated = [[] for _ in range(N_GROUPS)], [], [], []
    for layer in range(DEPTH):
        m = mods[layer]
        h = _norm_mod(x, w["norm_mix"][layer], m, 0, 1, tm)
        if layer % N_MIXERS == 0:
            a = layer // N_MIXERS
            outs, lses = [], []
            for g in range(N_GROUPS):
                dil = DILATIONS[g]
                qkv = _qkv_project(h, w["attn_w_qkv"][a], _qk_gains(w["attn_q_gain"][a], w["attn_k_gain"][a], g),
                                   g, dil, tm)
                o, lse = _window_attention(qkv, _band_bias(w["rel_bias"], g), dil)
                outs.append(o)
                lses.append(lse)
                keep = min(WINDOWS[g], t)
                rows = keep // dil
                kv = qkv[:, 1:3, :, :, t // dil - rows:, :]
                kv = jnp.transpose(kv, (0, 4, 2, 1, 3, 5))
                new_kv[g].append(kv.reshape(bx, keep, 2, HEADS, HEAD_DIM))
            x = _attn_out(outs, lses, w["attn_w_o"][a], x, m, 2, min(tm, 256))
        else:
            r = layer // N_MIXERS
            proj = _matmul(h, w["ret_w_in"][r], min(2 * tm, t), 1024)
            cos_rep, sin_signed = _rotation_tables(jnp.arange(t, dtype=jnp.int32))
            y, state = _retention_sequence(proj, cos_rep, sin_signed, w["ret_gn_gain"][r])
            new_ret.append(state)
            x = _resid_matmul(y, w["ret_w_o"][r], x, m, 2, tm, 512)
        h = _norm_mod(x, w["norm_ffn"][layer], m, 3, 4, tm)
        ffn_w = (w["ffn_w_up"][layer], w["ffn_conv_w"][layer], w["ffn_conv_b"][layer])
        if carried is None or not carried[layer]:
            g_act, rows = _ffn_up_sequence(h, *ffn_w)
        else:
            shifts = []
            for cache, a, k_new, v_new in carried[layer]:
                nl, n_seqs, width = cache.shape[:3]
                cache_rows = cache.reshape(nl * n_seqs, width * KV_ROWS, HEAD_DIM)
                new_rows = jnp.concatenate([k_new, v_new], axis=1)
                shifts.append((cache_rows, a * n_seqs, new_rows))
            g_act, rows, upds = _ffn_up_sequence(h, *ffn_w, shift=shifts)
            updated += [u.reshape(c[0].shape[1:]) for u, c in zip(upds, carried[layer])]
        new_conv.append(rows)
        x = _resid_matmul(g_act, w["ffn_w_down"][layer], x, m, 5, tm, 512)
    return x, [jnp.stack(kv) for kv in new_kv], jnp.stack(new_ret), jnp.stack(new_conv), updated


def _step_qkv(x, m, w, layer):
    b = x.shape[1]
    a = layer // N_MIXERS
    h = _norm_mod(x, w["norm_mix"][layer], m, 0, 1, b)
    out = []
    for g in range(N_GROUPS):
        qkv = _qkv_project(h, w["attn_w_qkv"][a], _qk_gains(w["attn_q_gain"][a], w["attn_k_gain"][a], g), g, 1, b)
        out.append(jnp.transpose(qkv.reshape(3, HEADS, b, HEAD_DIM), (0, 2, 1, 3)))
    return out


def _step_trunk(x, mods, caches, ret_state, conv_state, position, w, qkv_first=None, updated=None):
    b, d = x.shape
    x = x[None]
    new_kv, new_ret, new_conv = [[] for _ in range(N_GROUPS)], [], []
    for layer in range(DEPTH):
        m = mods[layer]
        if layer % N_MIXERS == 0:
            a = layer // N_MIXERS
            qkvs = qkv_first if (layer == 0 and qkv_first is not None) else _step_qkv(x, m, w, layer)
            outs, lses = [], []
            for g in range(N_GROUPS):
                qkv = qkvs[g]
                o, lse = _step_attention(qkv[0], qkv[1], qkv[2], caches[g], a, _group_bias(w["rel_bias"], g),
                                         DILATIONS[g], b)
                outs.append(jnp.transpose(o, (1, 0, 2))[None])
                lses.append(jnp.transpose(lse, (1, 0, 2))[None])
                if layer == 0 and updated is not None and g in updated:
                    new_kv[g].append(updated[g])
                else:
                    new_kv[g].append(_cache_shift(caches[g], a, qkv[1], qkv[2]))
            x = _attn_out(outs, lses, w["attn_w_o"][a], x, m, 2, b)
        else:
            r = layer // N_MIXERS
            h = _norm_mod(x, w["norm_mix"][layer], m, 0, 1, b)
            proj = _matmul(h, w["ret_w_in"][r], b, 1024)[0]
            q = proj[:, :RET_QK_WIDTH].reshape(b, RET_HEADS, RET_DK)
            k = proj[:, RET_QK_WIDTH:2 * RET_QK_WIDTH].reshape(b, RET_HEADS, RET_DK)
            v = proj[:, 2 * RET_QK_WIDTH:2 * RET_QK_WIDTH + RET_V_WIDTH].reshape(b, RET_HEADS, RET_DV)
            gate = proj[:, 2 * RET_QK_WIDTH + RET_V_WIDTH:].reshape(b, RET_HEADS, RET_DV)
            cos_rep, sin_signed = _rotation_tables(position)
            y, state = _retention_step(q, k, v, gate, ret_state, r, cos_rep, sin_signed, w["ret_gn_gain"][r])
            new_ret.append(state)
            x = _resid_matmul(y.reshape(1, b, RET_V_WIDTH), w["ret_w_o"][r], x, m, 2, b, 512)
        h = _norm_mod(x, w["norm_ffn"][layer], m, 3, 4, b)
        g_act, u_new = _ffn_up_step(h[0], w["ffn_w_up"][layer], w["ffn_conv_w"][layer], w["ffn_conv_b"][layer],
                                    conv_state[layer, :, 0], conv_state[layer, :, 1])
        new_conv.append(jnp.stack([conv_state[layer, :, 1], u_new], axis=1))
        x = _resid_matmul(g_act[None], w["ffn_w_down"][layer], x, m, 5, b, 512)
    return x[0], [jnp.stack(kv) for kv in new_kv], jnp.stack(new_ret), jnp.stack(new_conv)


def kernel(x_prompt, x_sample, cache_attn_kv_w128, cache_attn_kv_w512, cache_attn_kv_w2048, state_ret,
           state_conv, c_prompt, c_sample, rel_bias, w_ada, b_ada, norm_mix, norm_ffn, attn_w_qkv,
           attn_q_gain, attn_k_gain, attn_w_o, ret_w_in, ret_gn_gain, ret_w_o, ffn_w_up, ffn_conv_w,
           ffn_conv_b, ffn_w_down):
    w = dict(rel_bias=rel_bias, norm_mix=norm_mix, norm_ffn=norm_ffn, attn_w_qkv=attn_w_qkv,
             attn_q_gain=attn_q_gain, attn_k_gain=attn_k_gain, attn_w_o=attn_w_o, ret_w_in=ret_w_in,
             ret_gn_gain=ret_gn_gain, ret_w_o=ret_w_o, ffn_w_up=ffn_w_up, ffn_conv_w=ffn_conv_w,
             ffn_conv_b=ffn_conv_b, ffn_w_down=ffn_w_down)
    bp, sp, d = x_prompt.shape
    bs = x_sample.shape[0]
    past_len = cache_attn_kv_w2048.shape[2]

    rows = bp + bs
    pad = (-rows) % 8
    c_all = jnp.concatenate([c_prompt, c_sample, jnp.zeros((pad, d), F32)], axis=0)
    mods = _ada_modulation(c_all, w_ada, b_ada)
    mods_p = mods[:, :bp].reshape(DEPTH, bp, 1, 6 * d)
    mods_s = mods[:, bp:rows].reshape(DEPTH, 1, bs, 6 * d)

    caches = (cache_attn_kv_w128, cache_attn_kv_w512, cache_attn_kv_w2048)
    x_s = x_sample[:, 0]
    qkv_first = _step_qkv(x_s[None], mods_s[0], w, 0)
    hosted = [[N_GROUPS - 1], list(range(N_GROUPS - 2, -1, -1))] + [[]] * (DEPTH - 2)
    carried = [[(caches[g], 0, qkv_first[g][1], qkv_first[g][2]) for g in groups] for groups in hosted]
    y_p, kv_p, ret_p, conv_p, updated = _sequence_trunk(x_prompt, mods_p, w, 512, carried)
    position = past_len + jnp.arange(1, dtype=jnp.int32)
    order = [g for groups in hosted for g in groups]
    y_s, kv_s, ret_s, conv_s = _step_trunk(x_s, mods_s, caches, state_ret, state_conv, position, w,
                                           qkv_first, dict(zip(order, updated)))

    return (y_p, y_s[:, None, :], kv_p[0], kv_p[1], kv_p[2], ret_p, conv_p,
            kv_s[0], kv_s[1], kv_s[2], ret_s, conv_s)
```

```python
import functools
import math

import jax
import jax.numpy as jnp
from jax import lax
from jax.experimental import pallas as pl
from jax.experimental.pallas import tpu as pltpu

D_MODEL = 2048
DEPTH = 2
N_MIXERS = 2

WINDOWS = (128, 512, 2048)
DILATIONS = (1, 4, 16)
N_GROUPS = 3
HEAD_DIM = 128
HEADS = 8
ATTN_WIDTH = HEADS * HEAD_DIM
N_KEYS = 129
KEY_BLOCK = 128
N_BUCKETS = 32
MAX_DISTANCE = 2048
NEG_INF = -1e30

RET_HEADS = 8
RET_DK = 256
RET_DV = 512
RET_QK_WIDTH = RET_HEADS * RET_DK
RET_V_WIDTH = RET_HEADS * RET_DV
RET_CHUNK = 128
RET_HEAD_GROUP = 2
ROT_BASE = 10000.0

FFN_DIM = 5504
FFN_TILE = 128
N_FFN_TILES = FFN_DIM // FFN_TILE
NORM_EPS = 1e-6

F32 = jnp.float32
MXU_DTYPE = jnp.bfloat16
VMEM_LIMIT_BYTES = 56 * 1024 * 1024


def _params(*semantics):
    return pltpu.CompilerParams(dimension_semantics=semantics, vmem_limit_bytes=VMEM_LIMIT_BYTES)


def _silu(x):
    return x * jax.nn.sigmoid(x)


def _ada_kernel(c_ref, w_ref, b_ref, o_ref):
    a = _silu(c_ref[...]).astype(MXU_DTYPE)
    w = w_ref[...].astype(MXU_DTYPE)
    o_ref[...] = jnp.dot(a, w, preferred_element_type=F32) + b_ref[...]


def _ada_modulation(c, w_ada, b_ada):
    rows = c.shape[0]
    tn = 1024
    return pl.pallas_call(
        _ada_kernel,
        grid=(DEPTH, 6 * D_MODEL // tn),
        in_specs=[pl.BlockSpec((rows, D_MODEL), lambda l, j: (0, 0)),
                  pl.BlockSpec((None, D_MODEL, tn), lambda l, j: (l, 0, j)),
                  pl.BlockSpec((None, 1, tn), lambda l, j: (l, 0, j))],
        out_specs=pl.BlockSpec((None, rows, tn), lambda l, j: (l, 0, j)),
        out_shape=jax.ShapeDtypeStruct((DEPTH, rows, 6 * D_MODEL), F32),
        compiler_params=_params("arbitrary", "arbitrary"),
        name="ada_modulation",
    )(c, w_ada, b_ada.reshape(DEPTH, 1, 6 * D_MODEL))


def _norm_mod_kernel(x_ref, g_ref, shift_ref, scale_ref, o_ref):
    x = x_ref[...]
    y = x * lax.rsqrt(jnp.mean(x * x, axis=-1, keepdims=True) + NORM_EPS) * g_ref[...]
    o_ref[...] = (y * (1.0 + scale_ref[...]) + shift_ref[...]).astype(o_ref.dtype)


def _mod_spec(mods, tm, col, n_lead):
    per_row = mods.shape[1] > 1
    rb = tm if per_row else 1

    def index(*ids):
        b, i = ids[n_lead], ids[n_lead + 1]
        return (b, i if per_row else 0, col)

    return pl.BlockSpec((None, rb, D_MODEL), index)


def _norm_mod(x, gain, mods, shift_col, scale_col, tm):
    bx, t, d = x.shape
    return pl.pallas_call(
        _norm_mod_kernel,
        grid=(bx, t // tm),
        in_specs=[pl.BlockSpec((None, tm, d), lambda b, i: (b, i, 0)),
                  pl.BlockSpec((1, d), lambda b, i: (0, 0)),
                  _mod_spec(mods, tm, shift_col, 0),
                  _mod_spec(mods, tm, scale_col, 0)],
        out_specs=pl.BlockSpec((None, tm, d), lambda b, i: (b, i, 0)),
        out_shape=jax.ShapeDtypeStruct((bx, t, d), MXU_DTYPE),
        compiler_params=_params("arbitrary", "arbitrary"),
        name="norm_mod",
    )(x, gain.reshape(1, d), mods, mods)


def _cast_weight_once(w_ref, wbf_ref):
    @pl.when((pl.program_id(1) == 0) & (pl.program_id(2) == 0))
    def _():
        wbf_ref[...] = w_ref[...].astype(wbf_ref.dtype)


def _matmul_kernel(a_ref, w_ref, o_ref, wbf_ref):
    _cast_weight_once(w_ref, wbf_ref)
    o_ref[...] = jnp.dot(a_ref[...], wbf_ref[...], preferred_element_type=F32).astype(o_ref.dtype)


def _matmul(a, w, tm, tn, out_dtype=F32):
    bx, t, k = a.shape
    n = w.shape[1]
    return pl.pallas_call(
        _matmul_kernel,
        grid=(n // tn, bx, t // tm),
        in_specs=[pl.BlockSpec((None, tm, k), lambda j, b, i: (b, i, 0)),
                  pl.BlockSpec((k, tn), lambda j, b, i: (0, j))],
        out_specs=pl.BlockSpec((None, tm, tn), lambda j, b, i: (b, i, j)),
        out_shape=jax.ShapeDtypeStruct((bx, t, n), out_dtype),
        scratch_shapes=[pltpu.VMEM((k, tn), MXU_DTYPE)],
        compiler_params=_params("arbitrary", "arbitrary", "arbitrary"),
        name="matmul",
    )(a, w)


def _resid_matmul_kernel(a_ref, w_ref, x_ref, gate_ref, o_ref, wbf_ref):
    _cast_weight_once(w_ref, wbf_ref)
    acc = jnp.dot(a_ref[...], wbf_ref[...], preferred_element_type=F32)
    o_ref[...] = x_ref[...] + gate_ref[...] * acc


def _resid_matmul(a, w, x, mods, gate_col, tm, tn):
    bx, t, k = a.shape
    n = w.shape[1]
    cols_per_group = D_MODEL // tn
    per_row = mods.shape[1] > 1
    rb = tm if per_row else 1
    gate_spec = pl.BlockSpec(
        (None, rb, tn), lambda j, b, i: (b, i if per_row else 0, gate_col * cols_per_group + j))
    return pl.pallas_call(
        _resid_matmul_kernel,
        grid=(n // tn, bx, t // tm),
        in_specs=[pl.BlockSpec((None, tm, k), lambda j, b, i: (b, i, 0)),
                  pl.BlockSpec((k, tn), lambda j, b, i: (0, j)),
                  pl.BlockSpec((None, tm, tn), lambda j, b, i: (b, i, j)),
                  gate_spec],
        out_specs=pl.BlockSpec((None, tm, tn), lambda j, b, i: (b, i, j)),
        out_shape=jax.ShapeDtypeStruct((bx, t, n), F32),
        scratch_shapes=[pltpu.VMEM((k, tn), MXU_DTYPE)],
        compiler_params=_params("arbitrary", "arbitrary", "arbitrary"),
        name="resid_matmul",
    )(a, w, x, mods)


def _qkv_kernel(a_ref, w_ref, gain_ref, o_ref, wbf_ref, y_ref, *, dilation, tm):
    _cast_weight_once(w_ref, wbf_ref)
    normed = pl.program_id(0) < 2
    gain = gain_ref[...]
    halves = 2 if tm % (16 * dilation) == 0 else 1
    hm = tm // halves
    rows = hm // dilation
    for half in range(halves):
        acc = jnp.dot(a_ref[half * hm:(half + 1) * hm, :], wbf_ref[...], preferred_element_type=F32)
        for h in range(HEADS):
            xh = acc[:, h * HEAD_DIM:(h + 1) * HEAD_DIM]
            yh = xh * lax.rsqrt(jnp.mean(xh * xh, axis=-1, keepdims=True) + NORM_EPS) * gain
            yh = jnp.where(normed, yh, xh)
            if dilation == 1:
                o_ref[0, h, half * hm:(half + 1) * hm, :] = yh
            else:
                y_ref[h, half * hm:(half + 1) * hm, :] = yh
        if dilation > 1:
            for c in range(dilation):
                for h in range(HEADS):
                    o_ref[c, h, half * rows:(half + 1) * rows, :] = (
                        y_ref[h, pl.ds(half * hm + c, rows, stride=dilation), :])


def _qkv_project(h, w_qkv, gains, group, dilation, tm):
    bx, t, d = h.shape
    td = t // dilation
    kern = functools.partial(_qkv_kernel, dilation=dilation, tm=tm)
    return pl.pallas_call(
        kern,
        grid=(3, bx, t // tm),
        in_specs=[pl.BlockSpec((None, tm, d), lambda p, b, i: (b, i, 0)),
                  pl.BlockSpec((d, ATTN_WIDTH), lambda p, b, i: (0, group * 3 + p)),
                  pl.BlockSpec((None, 1, HEAD_DIM), lambda p, b, i: (p, 0, 0))],
        out_specs=pl.BlockSpec((None, None, dilation, HEADS, tm // dilation, HEAD_DIM),
                               lambda p, b, i: (b, p, 0, 0, i, 0)),
        out_shape=jax.ShapeDtypeStruct((bx, 3, dilation, HEADS, td, HEAD_DIM), F32),
        scratch_shapes=[pltpu.VMEM((d, ATTN_WIDTH), MXU_DTYPE), pltpu.VMEM((HEADS, tm, HEAD_DIM), F32)],
        compiler_params=_params("arbitrary", "arbitrary", "arbitrary"),
        name=f"qkv_project_g{group}",
    )(h, w_qkv, gains)


def _t5_causal_bucket(dist):
    max_exact = N_BUCKETS // 2
    d = jnp.maximum(dist, 1).astype(F32)
    large = max_exact + (jnp.log(d / max_exact) / math.log(MAX_DISTANCE / max_exact)
                         * (N_BUCKETS - max_exact)).astype(jnp.int32)
    return jnp.where(dist < max_exact, dist, jnp.minimum(large, N_BUCKETS - 1))


def _bucket_lookup(rel_bias, g, bucket, fill):
    cols = rel_bias[:, g * HEADS:(g + 1) * HEADS].astype(F32)
    expand = (slice(None),) + (None,) * bucket.ndim
    out = jnp.full((HEADS,) + bucket.shape, fill, F32)
    for b in range(N_BUCKETS):
        out = jnp.where(bucket[None] == b, cols[b][expand], out)
    return out


def _group_bias(rel_bias, g):
    dist = DILATIONS[g] * jnp.arange(N_KEYS, dtype=jnp.int32)
    return _bucket_lookup(rel_bias, g, _t5_causal_bucket(dist), 0.0)


def _band_bias(rel_bias, g):
    a = jnp.arange(KEY_BLOCK, dtype=jnp.int32)[:, None]
    c = jnp.arange(2 * KEY_BLOCK, dtype=jnp.int32)[None, :]
    rel = a + KEY_BLOCK - c
    valid = (rel >= 0) & (rel <= KEY_BLOCK)
    bucket = jnp.where(valid, _t5_causal_bucket(DILATIONS[g] * jnp.clip(rel, 0, KEY_BLOCK)), -1)
    return _bucket_lookup(rel_bias, g, bucket, NEG_INF)


def _window_attn_kernel(q_ref, kp_ref, kc_ref, vp_ref, vc_ref, bias_ref, o_ref, lse_ref,
                        *, dilation, heads):
    i = pl.program_id(1)
    c = pl.program_id(3)
    col = lax.broadcasted_iota(jnp.int32, (KEY_BLOCK, 2 * KEY_BLOCK), 1)
    no_prev = (i == 0) & (col < KEY_BLOCK)
    for h in range(heads):
        q = q_ref[h].astype(MXU_DTYPE)
        k = jnp.concatenate([kp_ref[h], kc_ref[h]], axis=0).astype(MXU_DTYPE)
        v = jnp.concatenate([vp_ref[h], vc_ref[h]], axis=0).astype(MXU_DTYPE)
        s = lax.dot_general(q, k, (((1,), (1,)), ((), ())), preferred_element_type=F32)
        s = s * (HEAD_DIM ** -0.5) + bias_ref[h]
        s = jnp.where(no_prev, NEG_INF, s)
        m = jnp.max(s, axis=-1, keepdims=True)
        p = jnp.exp(s - m)
        l = jnp.sum(p, axis=-1, keepdims=True)
        o = jnp.dot(p.astype(MXU_DTYPE), v, preferred_element_type=F32) / l
        lse = jnp.broadcast_to(m + jnp.log(l), (KEY_BLOCK, HEAD_DIM))
        if dilation == 1:
            o_ref[h] = o
            lse_ref[h] = lse
        else:
            o_ref[h, pl.ds(c, KEY_BLOCK, stride=dilation), :] = o
            lse_ref[h, pl.ds(c, KEY_BLOCK, stride=dilation), :] = lse


def _window_attention(qkv, band_bias, dilation):
    bx, _, _, _, td, _ = qkv.shape
    t = td * dilation
    nblk = td // KEY_BLOCK
    hb = 1
    heads = HEADS // hb

    def spec(part, prev):
        def index(b, i, hq, c):
            return (b, part, c, hq, jnp.maximum(i - 1, 0) if prev else i, 0)
        return pl.BlockSpec((None, None, None, heads, KEY_BLOCK, HEAD_DIM), index)

    out_spec = pl.BlockSpec((None, heads, KEY_BLOCK * dilation, HEAD_DIM), lambda b, i, hq, c: (b, hq, i, 0))
    kern = functools.partial(_window_attn_kernel, dilation=dilation, heads=heads)
    return pl.pallas_call(
        kern,
        grid=(bx, nblk, hb, dilation),
        in_specs=[spec(0, False), spec(1, True), spec(1, False), spec(2, True), spec(2, False),
                  pl.BlockSpec((heads, KEY_BLOCK, 2 * KEY_BLOCK), lambda b, i, hq, c: (hq, 0, 0))],
        out_specs=[out_spec, out_spec],
        out_shape=[jax.ShapeDtypeStruct((bx, HEADS, t, HEAD_DIM), F32)] * 2,
        compiler_params=_params("arbitrary", "arbitrary", "arbitrary", "arbitrary"),
        name=f"window_attention_d{dilation}",
    )(qkv, qkv, qkv, qkv, qkv, band_bias)


STEP_ROWS = 8
STEP_SEQS = 8


def _step_attn_kernel(*refs):
    q_ref, kn_ref, vn_ref = refs[:3]
    kv_refs = refs[3:3 + STEP_ROWS]
    bias0_ref, bias_ref, o_ref, lse_ref, m_ref, l_ref, acc_ref = refs[3 + STEP_ROWS:]
    j = pl.program_id(1)
    scale = HEAD_DIM ** -0.5

    def chunk(ci, carry):
        sl = pl.ds(pl.multiple_of(ci * STEP_SEQS, STEP_SEQS), STEP_SEQS)
        q = q_ref[sl]

        @pl.when(j == 0)
        def _():
            s0 = jnp.sum(q * kn_ref[sl], axis=-1, keepdims=True) * scale + bias0_ref[...]
            m_ref[sl] = jnp.broadcast_to(s0, q.shape)
            l_ref[sl] = jnp.ones(q.shape, F32)
            acc_ref[sl] = vn_ref[sl]

        scores = [jnp.sum(q * kv_refs[r][sl, 0], axis=-1, keepdims=True) * scale + bias_ref[r]
                  for r in range(STEP_ROWS)]
        m_old = m_ref[sl]
        m_new = m_old
        for s in scores:
            m_new = jnp.maximum(m_new, s)
        alpha = jnp.exp(m_old - m_new)
        l_new = alpha * l_ref[sl]
        acc = alpha * acc_ref[sl]
        for r in range(STEP_ROWS):
            p = jnp.exp(scores[r] - m_new)
            l_new = l_new + p
            acc = acc + p * kv_refs[r][sl, 1]
        m_ref[sl] = m_new
        l_ref[sl] = l_new
        acc_ref[sl] = acc
        return carry

    n_chunks = q_ref.shape[0] // STEP_SEQS
    lax.fori_loop(0, n_chunks, chunk, 0, unroll=2 if n_chunks % 2 == 0 else 1)

    @pl.when(j == pl.num_programs(1) - 1)
    def _():
        o_ref[...] = acc_ref[...] / l_ref[...]
        lse_ref[...] = m_ref[...] + jnp.log(l_ref[...])


def _step_attention(q, k_new, v_new, cache, layer, gb, dilation, bt):
    b = q.shape[0]
    nk = N_KEYS - 1
    lanes = jnp.broadcast_to(gb.T[:, :, None], (N_KEYS, HEADS, HEAD_DIM))
    bias0 = lanes[0]
    bias_rows = lanes[:0:-1]
    qspec = pl.BlockSpec((bt, HEADS, HEAD_DIM), lambda bi, j: (bi, 0, 0))

    def row_spec(r):
        return pl.BlockSpec((None, bt, None, 2, HEADS, HEAD_DIM),
                            lambda bi, j: (layer, bi, (j * STEP_ROWS + r) * dilation, 0, 0, 0))

    return pl.pallas_call(
        _step_attn_kernel,
        grid=(b // bt, nk // STEP_ROWS),
        in_specs=[qspec, qspec, qspec] + [row_spec(r) for r in range(STEP_ROWS)] + [
            pl.BlockSpec((HEADS, HEAD_DIM), lambda bi, j: (0, 0)),
            pl.BlockSpec((STEP_ROWS, HEADS, HEAD_DIM), lambda bi, j: (j, 0, 0))],
        out_specs=[qspec, qspec],
        out_shape=[jax.ShapeDtypeStruct((b, HEADS, HEAD_DIM), F32)] * 2,
        scratch_shapes=[pltpu.VMEM((bt, HEADS, HEAD_DIM), F32)] * 3,
        compiler_params=_params("arbitrary", "arbitrary"),
        name=f"step_attention_d{dilation}",
    )(q, k_new, v_new, *([cache] * STEP_ROWS), bias0, bias_rows)


def _attn_out_kernel(o0_ref, o1_ref, o2_ref, l0_ref, l1_ref, l2_ref, w_ref, x_ref, gate_ref, out_ref,
                     wbf_ref):
    _cast_weight_once(w_ref, wbf_ref)
    heads = []
    for h in range(HEADS):
        l0, l1, l2 = l0_ref[h], l1_ref[h], l2_ref[h]
        m = jnp.maximum(jnp.maximum(l0, l1), l2)
        e0, e1, e2 = jnp.exp(l0 - m), jnp.exp(l1 - m), jnp.exp(l2 - m)
        tot = e0 + e1 + e2
        merged = (e0 / tot) * o0_ref[h] + (e1 / tot) * o1_ref[h] + (e2 / tot) * o2_ref[h]
        heads.append(merged.astype(MXU_DTYPE))
    acc = jnp.dot(jnp.concatenate(heads, axis=-1), wbf_ref[...], preferred_element_type=F32)
    out_ref[...] = x_ref[...] + gate_ref[...] * acc


def _attn_out(outs, lses, w_o, x, mods, gate_col, tm):
    bx, t, d = x.shape
    per_row = mods.shape[1] > 1
    rb = tm if per_row else 1
    aspec = pl.BlockSpec((None, HEADS, tm, HEAD_DIM), lambda j, b, i: (b, 0, i, 0))
    xspec = pl.BlockSpec((None, tm, d), lambda j, b, i: (b, i, 0))
    return pl.pallas_call(
        _attn_out_kernel,
        grid=(1, bx, t // tm),
        in_specs=[aspec] * 6 + [
            pl.BlockSpec((ATTN_WIDTH, d), lambda j, b, i: (0, 0)),
            xspec,
            pl.BlockSpec((None, rb, d), lambda j, b, i: (b, i if per_row else 0, gate_col))],
        out_specs=xspec,
        out_shape=jax.ShapeDtypeStruct((bx, t, d), F32),
        scratch_shapes=[pltpu.VMEM((ATTN_WIDTH, d), MXU_DTYPE)],
        compiler_params=_params("arbitrary", "arbitrary", "arbitrary"),
        name="attn_out",
    )(*outs, *lses, w_o, x, mods)


KV_ROWS = 2 * HEADS
SHIFT_BUFFERS = 3
SHIFT_CHUNK_BYTES = 8 * 1024 * 1024


def _shift_chunking(b, keep_rows):
    row_bytes = HEAD_DIM * 4
    pieces = 1
    while (keep_rows // pieces) * row_bytes > SHIFT_CHUNK_BYTES or keep_rows % (8 * pieces):
        pieces += 1
    rows = keep_rows // pieces
    seqs = 1
    while seqs * 2 * rows * row_bytes <= SHIFT_CHUNK_BYTES // 2 and b % (seqs * 2) == 0:
        seqs *= 2
    return seqs, rows, pieces


def _cache_shift_kernel(prev_ref, new_ref, out_ref, buf_ref, in_sems, out_sems, new_sem,
                        *, layer, seqs, rows, pieces):
    b, total_rows, _ = out_ref.shape
    n_chunks = (b // seqs) * pieces

    def load(k, slot):
        block, piece = k // pieces, k % pieces
        src = prev_ref.at[pl.ds(layer * b + block * seqs, seqs), pl.ds(KV_ROWS + piece * rows, rows)]
        return pltpu.make_async_copy(src, buf_ref.at[slot], in_sems.at[slot])

    def store(k, slot):
        block, piece = k // pieces, k % pieces
        dst = out_ref.at[pl.ds(block * seqs, seqs), pl.ds(piece * rows, rows)]
        return pltpu.make_async_copy(buf_ref.at[slot], dst, out_sems.at[slot])

    new_copy = pltpu.make_async_copy(new_ref, out_ref.at[:, pl.ds(total_rows - KV_ROWS, KV_ROWS)], new_sem)
    new_copy.start()
    load(0, 0).start()

    def step(k, carry):
        slot = k % SHIFT_BUFFERS
        nxt = k + 1
        nxt_slot = nxt % SHIFT_BUFFERS

        @pl.when(nxt < n_chunks)
        def _():
            @pl.when(nxt >= SHIFT_BUFFERS)
            def _():
                store(nxt - SHIFT_BUFFERS, nxt_slot).wait()
            load(nxt, nxt_slot).start()

        load(k, slot).wait()
        store(k, slot).start()
        return carry

    lax.fori_loop(0, n_chunks, step, 0)
    for k in range(max(0, n_chunks - SHIFT_BUFFERS), n_chunks):
        store(k, k % SHIFT_BUFFERS).wait()
    new_copy.wait()


def _cache_shift(prev, layer, k_new, v_new):
    nl, b, w = prev.shape[:3]
    prev_rows = prev.reshape(nl * b, w * KV_ROWS, HEAD_DIM)
    new_rows = jnp.concatenate([k_new, v_new], axis=1)
    seqs, rows, pieces = _shift_chunking(b, (w - 1) * KV_ROWS)
    out = pl.pallas_call(
        functools.partial(_cache_shift_kernel, layer=layer, seqs=seqs, rows=rows, pieces=pieces),
        in_specs=[pl.BlockSpec(memory_space=pl.ANY),
                  pl.BlockSpec(memory_space=pltpu.VMEM)],
        out_specs=pl.BlockSpec(memory_space=pl.ANY),
        out_shape=jax.ShapeDtypeStruct((b, w * KV_ROWS, HEAD_DIM), prev.dtype),
        scratch_shapes=[pltpu.VMEM((SHIFT_BUFFERS, seqs, rows, HEAD_DIM), prev.dtype),
                        pltpu.SemaphoreType.DMA((SHIFT_BUFFERS,)),
                        pltpu.SemaphoreType.DMA((SHIFT_BUFFERS,)),
                        pltpu.SemaphoreType.DMA(())],
        compiler_params=pltpu.CompilerParams(vmem_limit_bytes=VMEM_LIMIT_BYTES),
        name="cache_shift",
    )(prev_rows, new_rows)
    return out.reshape(b, w, 2, HEADS, HEAD_DIM)


def _rotation_tables(positions):
    half = RET_DK // 2
    inv_freq = 1.0 / (ROT_BASE ** jnp.linspace(0.0, 1.0, half, dtype=F32))
    ang = positions.astype(F32)[:, None] * inv_freq[None, :]
    cos, sin = jnp.cos(ang), jnp.sin(ang)
    cos_rep = jnp.stack([cos, cos], axis=-1).reshape(-1, RET_DK)
    sin_signed = jnp.stack([-sin, sin], axis=-1).reshape(-1, RET_DK)
    return cos_rep, sin_signed


def _rotate_pairs(x, cos_rep, sin_signed):
    lanes = 128
    even = lax.broadcasted_iota(jnp.int32, (x.shape[0], lanes), 1) % 2 == 0
    parts = []
    for s in range(x.shape[1] // lanes):
        xs = x[:, s * lanes:(s + 1) * lanes]
        parts.append(jnp.where(even, pltpu.roll(xs, lanes - 1, 1), pltpu.roll(xs, 1, 1)))
    swapped = jnp.concatenate(parts, axis=-1)
    return x * cos_rep + swapped * sin_signed


def _log_gamma():
    return jnp.log1p(-jnp.exp2(-5.0 - jnp.arange(RET_HEADS, dtype=F32)))


def _group_norm_gate(o, gain, gate):
    mu = jnp.mean(o, axis=-1, keepdims=True)
    var = jnp.mean(jnp.square(o - mu), axis=-1, keepdims=True)
    return _silu(gate) * ((o - mu) * lax.rsqrt(var + NORM_EPS) * gain)


def _retention_kernel(q_ref, k_ref, v_ref, gate_ref, cos_ref, sin_ref, decay_ref, qdec_ref, kdec_ref,
                      cdec_ref, gain_ref, y_ref, s_out_ref, s_ref):
    c = pl.program_id(1)
    nb, hg = s_ref.shape[:2]

    @pl.when(c == 0)
    def _():
        s_ref[...] = jnp.zeros(s_ref.shape, F32)

    cos, sin = cos_ref[...], sin_ref[...]
    for b in range(nb):
        for hh in range(hg):
            ksl = slice(hh * RET_DK, (hh + 1) * RET_DK)
            vsl = slice(hh * RET_DV, (hh + 1) * RET_DV)
            q = _rotate_pairs(q_ref[b, :, ksl], cos, sin)
            k = _rotate_pairs(k_ref[b, :, ksl], cos, sin) * (RET_DK ** -0.5)
            v = v_ref[b, :, vsl].astype(MXU_DTYPE)
            state = s_ref[b, hh]
            scores = lax.dot_general(q.astype(MXU_DTYPE), k.astype(MXU_DTYPE), (((1,), (1,)), ((), ())),
                                     preferred_element_type=F32) * decay_ref[hh]
            o = jnp.dot(scores.astype(MXU_DTYPE), v, preferred_element_type=F32)
            o = o + jnp.dot((q * qdec_ref[hh]).astype(MXU_DTYPE), state.astype(MXU_DTYPE),
                            preferred_element_type=F32)
            kd_t = jnp.transpose(k * kdec_ref[hh]).astype(MXU_DTYPE)
            s_ref[b, hh] = cdec_ref[hh] * state + jnp.dot(kd_t, v, preferred_element_type=F32)
            y_ref[b, :, vsl] = _group_norm_gate(o, gain_ref[hh], gate_ref[b, :, vsl]).astype(y_ref.dtype)

    @pl.when(c == pl.num_programs(1) - 1)
    def _():
        s_out_ref[...] = s_ref[...]


def _retention_sequence(proj, cos_rep, sin_signed, gn_gain):
    bx, t, _ = proj.shape
    cw = RET_CHUNK
    lg = _log_gamma()
    pos = jnp.arange(cw, dtype=F32)
    diff = pos[:, None] - pos[None, :]
    decay = jnp.where(diff >= 0, jnp.exp(diff[None] * lg[:, None, None]), 0.0)
    q_decay = jnp.exp((pos[:, None] + 1.0) * lg[None, :]).T[:, :, None]
    k_decay = jnp.exp((cw - 1.0 - pos)[:, None] * lg[None, :]).T[:, :, None]
    chunk_decay = jnp.exp(cw * lg).reshape(RET_HEADS, 1, 1)
    hg = RET_HEAD_GROUP
    nqk = RET_QK_WIDTH // (hg * RET_DK)
    nv0 = 2 * RET_QK_WIDTH // (hg * RET_DV)
    ng0 = nv0 + RET_HEADS // hg
    return pl.pallas_call(
        _retention_kernel,
        grid=(RET_HEADS // hg, t // cw),
        in_specs=[pl.BlockSpec((bx, cw, hg * RET_DK), lambda h, c: (0, c, h)),
                  pl.BlockSpec((bx, cw, hg * RET_DK), lambda h, c: (0, c, nqk + h)),
                  pl.BlockSpec((bx, cw, hg * RET_DV), lambda h, c: (0, c, nv0 + h)),
                  pl.BlockSpec((bx, cw, hg * RET_DV), lambda h, c: (0, c, ng0 + h)),
                  pl.BlockSpec((cw, RET_DK), lambda h, c: (c, 0)),
                  pl.BlockSpec((cw, RET_DK), lambda h, c: (c, 0)),
                  pl.BlockSpec((hg, cw, cw), lambda h, c: (h, 0, 0)),
                  pl.BlockSpec((hg, cw, 1), lambda h, c: (h, 0, 0)),
                  pl.BlockSpec((hg, cw, 1), lambda h, c: (h, 0, 0)),
                  pl.BlockSpec((hg, 1, 1), lambda h, c: (h, 0, 0)),
                  pl.BlockSpec((hg, 1, RET_DV), lambda h, c: (h, 0, 0))],
        out_specs=[pl.BlockSpec((bx, cw, hg * RET_DV), lambda h, c: (0, c, h)),
                   pl.BlockSpec((bx, hg, RET_DK, RET_DV), lambda h, c: (0, h, 0, 0))],
        out_shape=[jax.ShapeDtypeStruct((bx, t, RET_V_WIDTH), MXU_DTYPE),
                   jax.ShapeDtypeStruct((bx, RET_HEADS, RET_DK, RET_DV), F32)],
        scratch_shapes=[pltpu.VMEM((bx, hg, RET_DK, RET_DV), F32)],
        compiler_params=_params("arbitrary", "arbitrary"),
        name="retention_sequence",
    )(proj, proj, proj, proj, cos_rep, sin_signed, decay, q_decay, k_decay, chunk_decay,
      gn_gain.reshape(RET_HEADS, 1, RET_DV))


def _retention_step_kernel(q_ref, k_ref, v_ref, gate_ref, cos_ref, sin_ref, gamma_ref, gain_ref, s_ref,
                           y_ref, s_out_ref):
    cos, sin = cos_ref[...], sin_ref[...]
    q = _rotate_pairs(q_ref[...], cos, sin)
    k = _rotate_pairs(k_ref[...], cos, sin) * (RET_DK ** -0.5)
    qk = jnp.sum(q * k, axis=-1, keepdims=True)
    q_t = jnp.transpose(q * gamma_ref[...])
    k_t = jnp.transpose(k)
    v = v_ref[...]
    gamma = gamma_ref[...]
    rows = []
    for h in range(RET_HEADS):
        state = s_ref[h]
        vh = v[h:h + 1, :]
        rows.append(qk[h:h + 1, :] * vh + jnp.sum(q_t[:, h:h + 1] * state, axis=0, keepdims=True))
        s_out_ref[h] = gamma[h:h + 1, :] * state + k_t[:, h:h + 1] * vh
    o = jnp.concatenate(rows, axis=0)
    y_ref[...] = _group_norm_gate(o, gain_ref[...], gate_ref[...]).astype(y_ref.dtype)


def _retention_step(q, k, v, gate, state, layer, cos_rep, sin_signed, gn_gain):
    b = q.shape[0]
    gamma = jnp.exp(_log_gamma()).reshape(RET_HEADS, 1)
    qspec = pl.BlockSpec((None, RET_HEADS, RET_DK), lambda i: (i, 0, 0))
    vspec = pl.BlockSpec((None, RET_HEADS, RET_DV), lambda i: (i, 0, 0))
    sspec = pl.BlockSpec((None, RET_HEADS, RET_DK, RET_DV), lambda i: (i, 0, 0, 0))
    sspec_in = pl.BlockSpec((None, None, RET_HEADS, RET_DK, RET_DV), lambda i: (layer, i, 0, 0, 0))
    return pl.pallas_call(
        _retention_step_kernel,
        grid=(b,),
        in_specs=[qspec, qspec, vspec, vspec,
                  pl.BlockSpec((1, RET_DK), lambda i: (0, 0)),
                  pl.BlockSpec((1, RET_DK), lambda i: (0, 0)),
                  pl.BlockSpec((RET_HEADS, 1), lambda i: (0, 0)),
                  pl.BlockSpec((RET_HEADS, RET_DV), lambda i: (0, 0)),
                  sspec_in],
        out_specs=[vspec, sspec],
        out_shape=[jax.ShapeDtypeStruct((b, RET_HEADS, RET_DV), MXU_DTYPE),
                   jax.ShapeDtypeStruct(state.shape[1:], F32)],
        compiler_params=_params("arbitrary"),
        name="retention_step",
    )(q, k, v, gate, cos_rep, sin_signed, gamma, gn_gain.reshape(RET_HEADS, RET_DV), state)


SIDE_POINTS = 2
SIDE_CHUNK_BYTES = 512 * 1024
SIDE_DEEP_BYTES = 10 * 1024 * 1024


class _SideCopy:
    def __init__(self, n_seqs, keep_rows, src_seq0, n_steps, deep):
        assert n_seqs + 1 < n_steps
        row_bytes = HEAD_DIM * 4
        pieces = SIDE_POINTS
        while keep_rows % (8 * pieces) or (keep_rows // pieces) * row_bytes > SIDE_CHUNK_BYTES:
            pieces += SIDE_POINTS
        self.rows = keep_rows // pieces
        self.per_point = pieces // SIDE_POINTS
        self.n_seqs, self.src_seq0, self.deep = n_seqs, src_seq0, deep

    def scratch_shapes(self):
        n = SIDE_POINTS * self.per_point * (2 if self.deep else 1)
        return [pltpu.VMEM((n, self.rows, HEAD_DIM), F32),
                pltpu.SemaphoreType.DMA((n,)), pltpu.SemaphoreType.DMA((n,)), pltpu.SemaphoreType.DMA(())]

    def bind(self, src_ref, new_ref, dst_ref, buf_ref, load_sems, store_sems, new_sem):
        self.refs = (src_ref, dst_ref, buf_ref, load_sems, store_sems)
        self.new = (new_ref, new_sem)

    def _slot(self, seq, piece):
        return piece + (seq % 2) * SIDE_POINTS * self.per_point if self.deep else piece

    def _load(self, seq, point, q):
        src_ref, _, buf_ref, load_sems, _ = self.refs
        piece = point * self.per_point + q
        slot = self._slot(seq, piece)
        src = src_ref.at[self.src_seq0 + seq, pl.ds(KV_ROWS + piece * self.rows, self.rows)]
        return pltpu.make_async_copy(src, buf_ref.at[slot], load_sems.at[slot])

    def _store(self, seq, point, q):
        _, dst_ref, buf_ref, _, store_sems = self.refs
        piece = point * self.per_point + q
        slot = self._slot(seq, piece)
        dst = dst_ref.at[seq, pl.ds(piece * self.rows, self.rows)]
        return pltpu.make_async_copy(buf_ref.at[slot], dst, store_sems.at[slot])

    def serve(self, step, point):
        if self.deep:
            out_step, out_point = step - 1, point
        else:
            out_step, out_point = (step if point > 0 else step - 1), (point - 1) % SIDE_POINTS
        retire_step = out_step - 1 if self.deep else step - 1

        @pl.when((out_step >= 0) & (out_step < self.n_seqs))
        def _():
            for q in range(self.per_point):
                self._load(out_step, out_point, q).wait()
                self._store(out_step, out_point, q).start(priority=1)

        @pl.when((retire_step >= 0) & (retire_step < self.n_seqs))
        def _():
            for q in range(self.per_point):
                self._store(retire_step, point, q).wait()

        @pl.when(step < self.n_seqs)
        def _():
            for q in range(self.per_point):
                self._load(step, point, q).start()

    def append_new_rows(self):
        new_ref, new_sem = self.new
        dst_ref = self.refs[1]
        total_rows = dst_ref.shape[1]
        copy = pltpu.make_async_copy(new_ref, dst_ref.at[:, pl.ds(total_rows - KV_ROWS, KV_ROWS)], new_sem)
        copy.start()
        copy.wait()


def _ffn_up_seq_kernel(*refs, t, chunk, parts, sides):
    a_ref, wg_ref, wv_ref, cwg_ref, cwv_ref, cbg_ref, cbv_ref = refs[:7]
    ns = len(sides)
    side_in = refs[7:7 + 2 * ns]
    g_ref, rows_g_ref, rows_v_ref = refs[7 + 2 * ns:10 + 2 * ns]
    side_out = refs[10 + 2 * ns:10 + 3 * ns]
    wbf_ref, u_ref = refs[10 + 3 * ns:12 + 3 * ns]
    side_scratch = refs[12 + 3 * ns:]
    for i, side in enumerate(sides):
        side.bind(side_in[2 * i], side_in[2 * i + 1], side_out[i], *side_scratch[4 * i:4 * i + 4])
    part_id = pl.program_id(2)
    step = (pl.program_id(0) * pl.num_programs(1) + pl.program_id(1)) * parts + part_id
    n_steps = pl.num_programs(0) * pl.num_programs(1) * parts

    @pl.when(part_id == 0)
    def _():
        wbf_ref[:, :FFN_TILE] = wg_ref[...].astype(wbf_ref.dtype)
        wbf_ref[:, FFN_TILE:] = wv_ref[...].astype(wbf_ref.dtype)
        u_ref[0:8, :] = jnp.zeros((8, 2 * FFN_TILE), F32)

    cw = jnp.concatenate([cwg_ref[...], cwv_ref[...]], axis=-1)
    cb = jnp.concatenate([cbg_ref[...], cbv_ref[...]], axis=-1)
    part_rows = t // parts
    n_chunks = part_rows // chunk
    stride = max(n_chunks // SIDE_POINTS, 1)

    def rows_of_part(part):
        for s in range(n_chunks):
            r0 = part * part_rows + s * chunk
            o0 = s * chunk
            if s % stride == 0 and s // stride < SIDE_POINTS:
                for side in sides:
                    side.serve(step, s // stride)
            u = jnp.dot(a_ref[r0:r0 + chunk, :], wbf_ref[...], preferred_element_type=F32)
            u_ref[8 + r0:8 + r0 + chunk, :] = u
            z = cb + cw[0:1] * u_ref[6 + r0:6 + r0 + chunk, :]
            z = z + cw[1:2] * u_ref[7 + r0:7 + r0 + chunk, :]
            z = z + cw[2:3] * u
            g_ref[o0:o0 + chunk, :] = (_silu(z[:, :FFN_TILE]) * z[:, FFN_TILE:]).astype(g_ref.dtype)

    for part in range(parts):
        pl.when(part_id == part)(functools.partial(rows_of_part, part))

    @pl.when(part_id == parts - 1)
    def _():
        rows_g_ref[...] = u_ref[6 + t:8 + t, :FFN_TILE]
        rows_v_ref[...] = u_ref[6 + t:8 + t, FFN_TILE:]

    if sides:
        @pl.when(step == n_steps - 1)
        def _():
            for side in sides:
                side.append_new_rows()


def _ffn_up_sequence(h, w_up, conv_w, conv_b, shift=None):
    bx, t, d = h.shape
    nt = N_FFN_TILES
    parts = 2 if (shift is not None and t % 1024 == 0) else 1
    chunk = min(t // parts, 512)
    conv_b = conv_b.reshape(1, 2 * FFN_DIM)
    in_specs = [pl.BlockSpec((None, t, d), lambda b, j, p: (b, 0, 0)),
                pl.BlockSpec((d, FFN_TILE), lambda b, j, p: (0, j)),
                pl.BlockSpec((d, FFN_TILE), lambda b, j, p: (0, nt + j)),
                pl.BlockSpec((3, FFN_TILE), lambda b, j, p: (0, j)),
                pl.BlockSpec((3, FFN_TILE), lambda b, j, p: (0, nt + j)),
                pl.BlockSpec((1, FFN_TILE), lambda b, j, p: (0, j)),
                pl.BlockSpec((1, FFN_TILE), lambda b, j, p: (0, nt + j))]
    operands = [h, w_up, w_up, conv_w, conv_w, conv_b, conv_b]
    out_specs = [pl.BlockSpec((None, t // parts, FFN_TILE), lambda b, j, p: (b, p, j)),
                 pl.BlockSpec((None, 2, FFN_TILE), lambda b, j, p: (b, 0, j)),
                 pl.BlockSpec((None, 2, FFN_TILE), lambda b, j, p: (b, 0, j))]
    out_shape = [jax.ShapeDtypeStruct((bx, t, FFN_DIM), MXU_DTYPE),
                 jax.ShapeDtypeStruct((bx, 2, FFN_DIM), F32),
                 jax.ShapeDtypeStruct((bx, 2, FFN_DIM), F32)]
    scratch = [pltpu.VMEM((d, 2 * FFN_TILE), MXU_DTYPE), pltpu.VMEM((t + 8, 2 * FFN_TILE), F32)]
    sides = []
    if shift is not None:
        in_specs[0] = pl.BlockSpec((None, t, d), lambda b, j, p: (b, 0, 0), pipeline_mode=pl.Buffered(1))
    for cache_rows, src_seq0, new_rows in (shift or ()):
        n_seqs, total_rows = new_rows.shape[0], cache_rows.shape[1]
        deep = 2 * (total_rows - KV_ROWS) * HEAD_DIM * 4 <= SIDE_DEEP_BYTES
        side = _SideCopy(n_seqs, total_rows - KV_ROWS, src_seq0, bx * nt * parts, deep)
        sides.append(side)
        in_specs += [pl.BlockSpec(memory_space=pl.ANY), pl.BlockSpec(memory_space=pltpu.VMEM)]
        operands += [cache_rows, new_rows]
        out_specs.append(pl.BlockSpec(memory_space=pl.ANY))
        out_shape.append(jax.ShapeDtypeStruct((n_seqs, total_rows, HEAD_DIM), cache_rows.dtype))
    for side in sides:
        scratch += side.scratch_shapes()
    kern = functools.partial(_ffn_up_seq_kernel, t=t, chunk=chunk, parts=parts, sides=tuple(sides))
    outs = pl.pallas_call(
        kern,
        grid=(bx, nt, parts),
        in_specs=in_specs,
        out_specs=out_specs,
        out_shape=out_shape,
        scratch_shapes=scratch,
        compiler_params=_params("arbitrary", "arbitrary", "arbitrary"),
        name="ffn_up_sequence",
    )(*operands)
    rows = jnp.concatenate([outs[1], outs[2]], axis=-1)
    return (outs[0], rows) if shift is None else (outs[0], rows, list(outs[3:]))


def _ffn_up_step_kernel(a_ref, wg_ref, wv_ref, cwg_ref, cwv_ref, cbg_ref, cbv_ref, p0g_ref, p0v_ref,
                        p1g_ref, p1v_ref, g_ref, ug_ref, uv_ref):
    w = jnp.concatenate([wg_ref[...], wv_ref[...]], axis=-1).astype(MXU_DTYPE)
    u = jnp.dot(a_ref[...], w, preferred_element_type=F32)
    cw = jnp.concatenate([cwg_ref[...], cwv_ref[...]], axis=-1)
    cb = jnp.concatenate([cbg_ref[...], cbv_ref[...]], axis=-1)
    p0 = jnp.concatenate([p0g_ref[...], p0v_ref[...]], axis=-1)
    p1 = jnp.concatenate([p1g_ref[...], p1v_ref[...]], axis=-1)
    z = cb + cw[0:1] * p0
    z = z + cw[1:2] * p1
    z = z + cw[2:3] * u
    g_ref[...] = (_silu(z[:, :FFN_TILE]) * z[:, FFN_TILE:]).astype(g_ref.dtype)
    ug_ref[...] = u[:, :FFN_TILE]
    uv_ref[...] = u[:, FFN_TILE:]


def _ffn_up_step(h, w_up, conv_w, conv_b, prev0, prev1):
    b, d = h.shape
    nt = N_FFN_TILES
    conv_b = conv_b.reshape(1, 2 * FFN_DIM)
    lo = lambda j: (0, j)
    hi = lambda j: (0, nt + j)
    g, ug, uv = pl.pallas_call(
        _ffn_up_step_kernel,
        grid=(nt,),
        in_specs=[pl.BlockSpec((b, d), lambda j: (0, 0)),
                  pl.BlockSpec((d, FFN_TILE), lo), pl.BlockSpec((d, FFN_TILE), hi),
                  pl.BlockSpec((3, FFN_TILE), lo), pl.BlockSpec((3, FFN_TILE), hi),
                  pl.BlockSpec((1, FFN_TILE), lo), pl.BlockSpec((1, FFN_TILE), hi),
                  pl.BlockSpec((b, FFN_TILE), lo), pl.BlockSpec((b, FFN_TILE), hi),
                  pl.BlockSpec((b, FFN_TILE), lo), pl.BlockSpec((b, FFN_TILE), hi)],
        out_specs=[pl.BlockSpec((b, FFN_TILE), lo)] * 3,
        out_shape=[jax.ShapeDtypeStruct((b, FFN_DIM), MXU_DTYPE),
                   jax.ShapeDtypeStruct((b, FFN_DIM), F32),
                   jax.ShapeDtypeStruct((b, FFN_DIM), F32)],
        compiler_params=_params("arbitrary"),
        name="ffn_up_step",
    )(h, w_up, w_up, conv_w, conv_w, conv_b, conv_b, prev0, prev0, prev1, prev1)
    return g, jnp.concatenate([ug, uv], axis=-1)


def _qk_gains(q_gain, k_gain, g):
    return jnp.stack([q_gain[g], k_gain[g], jnp.ones_like(q_gain[g])])[:, None, :]


def _sequence_trunk(x, mods, w, tm, carried=None):
    bx, t, d = x.shape
    new_kv, new_ret, new_conv, updated = [[] for _ in range(N_GROUPS)], [], [], []
    for layer in range(DEPTH):
        m = mods[layer]
        h = _norm_mod(x, w["norm_mix"][layer], m, 0, 1, tm)
        if layer % N_MIXERS == 0:
            a = layer // N_MIXERS
            outs, lses = [], []
            for g in range(N_GROUPS):
                dil = DILATIONS[g]
                qkv = _qkv_project(h, w["attn_w_qkv"][a], _qk_gains(w["attn_q_gain"][a], w["attn_k_gain"][a], g),
                                   g, dil, tm)
                o, lse = _window_attention(qkv, _band_bias(w["rel_bias"], g), dil)
                outs.append(o)
                lses.append(lse)
                keep = min(WINDOWS[g], t)
                rows = keep // dil
                kv = qkv[:, 1:3, :, :, t // dil - rows:, :]
                kv = jnp.transpose(kv, (0, 4, 2, 1, 3, 5))
                new_kv[g].append(kv.reshape(bx, keep, 2, HEADS, HEAD_DIM))
            x = _attn_out(outs, lses, w["attn_w_o"][a], x, m, 2, min(tm, 256))
        else:
            r = layer // N_MIXERS
            proj = _matmul(h, w["ret_w_in"][r], min(2 * tm, t), 1024)
            cos_rep, sin_signed = _rotation_tables(jnp.arange(t, dtype=jnp.int32))
            y, state = _retention_sequence(proj, cos_rep, sin_signed, w["ret_gn_gain"][r])
            new_ret.append(state)
            x = _resid_matmul(y, w["ret_w_o"][r], x, m, 2, tm, 512)
        h = _norm_mod(x, w["norm_ffn"][layer], m, 3, 4, tm)
        ffn_w = (w["ffn_w_up"][layer], w["ffn_conv_w"][layer], w["ffn_conv_b"][layer])
        if carried is None or not carried[layer]:
            g_act, rows = _ffn_up_sequence(h, *ffn_w)
        else:
            shifts = []
            for cache, a, k_new, v_new in carried[layer]:
                nl, n_seqs, width = cache.shape[:3]
                cache_rows = cache.reshape(nl * n_seqs, width * KV_ROWS, HEAD_DIM)
                new_rows = jnp.concatenate([k_new, v_new], axis=1)
                shifts.append((cache_rows, a * n_seqs, new_rows))
            g_act, rows, upds = _ffn_up_sequence(h, *ffn_w, shift=shifts)
            updated += [u.reshape(c[0].shape[1:]) for u, c in zip(upds, carried[layer])]
        new_conv.append(rows)
        x = _resid_matmul(g_act, w["ffn_w_down"][layer], x, m, 5, tm, 512)
    return x, [jnp.stack(kv) for kv in new_kv], jnp.stack(new_ret), jnp.stack(new_conv), updated


def _step_qkv(x, m, w, layer):
    b = x.shape[1]
    a = layer // N_MIXERS
    h = _norm_mod(x, w["norm_mix"][layer], m, 0, 1, b)
    out = []
    for g in range(N_GROUPS):
        qkv = _qkv_project(h, w["attn_w_qkv"][a], _qk_gains(w["attn_q_gain"][a], w["attn_k_gain"][a], g), g, 1, b)
        out.append(jnp.transpose(qkv.reshape(3, HEADS, b, HEAD_DIM), (0, 2, 1, 3)))
    return out


def _step_trunk(x, mods, caches, ret_state, conv_state, position, w, qkv_first=None, updated=None):
    b, d = x.shape
    x = x[None]
    new_kv, new_ret, new_conv = [[] for _ in range(N_GROUPS)], [], []
    for layer in range(DEPTH):
        m = mods[layer]
        if layer % N_MIXERS == 0:
            a = layer // N_MIXERS
            qkvs = qkv_first if (layer == 0 and qkv_first is not None) else _step_qkv(x, m, w, layer)
            outs, lses = [], []
            for g in range(N_GROUPS):
                qkv = qkvs[g]
                o, lse = _step_attention(qkv[0], qkv[1], qkv[2], caches[g], a, _group_bias(w["rel_bias"], g),
                                         DILATIONS[g], b)
                outs.append(jnp.transpose(o, (1, 0, 2))[None])
                lses.append(jnp.transpose(lse, (1, 0, 2))[None])
                if layer == 0 and updated is not None and g in updated:
                    new_kv[g].append(updated[g])
                else:
                    new_kv[g].append(_cache_shift(caches[g], a, qkv[1], qkv[2]))
            x = _attn_out(outs, lses, w["attn_w_o"][a], x, m, 2, b)
        else:
            r = layer // N_MIXERS
            h = _norm_mod(x, w["norm_mix"][layer], m, 0, 1, b)
            proj = _matmul(h, w["ret_w_in"][r], b, 1024)[0]
            q = proj[:, :RET_QK_WIDTH].reshape(b, RET_HEADS, RET_DK)
            k = proj[:, RET_QK_WIDTH:2 * RET_QK_WIDTH].reshape(b, RET_HEADS, RET_DK)
            v = proj[:, 2 * RET_QK_WIDTH:2 * RET_QK_WIDTH + RET_V_WIDTH].reshape(b, RET_HEADS, RET_DV)
            gate = proj[:, 2 * RET_QK_WIDTH + RET_V_WIDTH:].reshape(b, RET_HEADS, RET_DV)
            cos_rep, sin_signed = _rotation_tables(position)
            y, state = _retention_step(q, k, v, gate, ret_state, r, cos_rep, sin_signed, w["ret_gn_gain"][r])
            new_ret.append(state)
            x = _resid_matmul(y.reshape(1, b, RET_V_WIDTH), w["ret_w_o"][r], x, m, 2, b, 512)
        h = _norm_mod(x, w["norm_ffn"][layer], m, 3, 4, b)
        g_act, u_new = _ffn_up_step(h[0], w["ffn_w_up"][layer], w["ffn_conv_w"][layer], w["ffn_conv_b"][layer],
                                    conv_state[layer, :, 0], conv_state[layer, :, 1])
        new_conv.append(jnp.stack([conv_state[layer, :, 1], u_new], axis=1))
        x = _resid_matmul(g_act[None], w["ffn_w_down"][layer], x, m, 5, b, 512)
    return x[0], [jnp.stack(kv) for kv in new_kv], jnp.stack(new_ret), jnp.stack(new_conv)


def kernel(x_prompt, x_sample, cache_attn_kv_w128, cache_attn_kv_w512, cache_attn_kv_w2048, state_ret,
           state_conv, c_prompt, c_sample, rel_bias, w_ada, b_ada, norm_mix, norm_ffn, attn_w_qkv,
           attn_q_gain, attn_k_gain, attn_w_o, ret_w_in, ret_gn_gain, ret_w_o, ffn_w_up, ffn_conv_w,
           ffn_conv_b, ffn_w_down):
    w = dict(rel_bias=rel_bias, norm_mix=norm_mix, norm_ffn=norm_ffn, attn_w_qkv=attn_w_qkv,
             attn_q_gain=attn_q_gain, attn_k_gain=attn_k_gain, attn_w_o=attn_w_o, ret_w_in=ret_w_in,
             ret_gn_gain=ret_gn_gain, ret_w_o=ret_w_o, ffn_w_up=ffn_w_up, ffn_conv_w=ffn_conv_w,
             ffn_conv_b=ffn_conv_b, ffn_w_down=ffn_w_down)
    bp, sp, d = x_prompt.shape
    bs = x_sample.shape[0]
    past_len = cache_attn_kv_w2048.shape[2]

    rows = bp + bs
    pad = (-rows) % 8
    c_all = jnp.concatenate([c_prompt, c_sample, jnp.zeros((pad, d), F32)], axis=0)
    mods = _ada_modulation(c_all, w_ada, b_ada)
    mods_p = mods[:, :bp].reshape(DEPTH, bp, 1, 6 * d)
    mods_s = mods[:, bp:rows].reshape(DEPTH, 1, bs, 6 * d)

    caches = (cache_attn_kv_w128, cache_attn_kv_w512, cache_attn_kv_w2048)
    x_s = x_sample[:, 0]
    qkv_first = _step_qkv(x_s[None], mods_s[0], w, 0)
    hosted = [[N_GROUPS - 1], list(range(N_GROUPS - 2, -1, -1))] + [[]] * (DEPTH - 2)
    carried = [[(caches[g], 0, qkv_first[g][1], qkv_first[g][2]) for g in groups] for groups in hosted]
    y_p, kv_p, ret_p, conv_p, updated = _sequence_trunk(x_prompt, mods_p, w, 512, carried)
    position = past_len + jnp.arange(1, dtype=jnp.int32)
    order = [g for groups in hosted for g in groups]
    y_s, kv_s, ret_s, conv_s = _step_trunk(x_s, mods_s, caches, state_ret, state_conv, position, w,
                                           qkv_first, dict(zip(order, updated)))

    return (y_p, y_s[:, None, :], kv_p[0], kv_p[1], kv_p[2], ret_p, conv_p,
            kv_s[0], kv_s[1], kv_s[2], ret_s, conv_s)
```
